```python
import math
import numpy as np
import jax
import jax.numpy as jnp
from jax import lax

D_MODEL = 1024
BATCH = 4
SEQ = 8192
DEPTH = 1

NSA_HEADS = 8
NSA_GROUPS = 2
NSA_HPG = NSA_HEADS // NSA_GROUPS
NSA_DH = 64
CMP_LEN = 32
CMP_STRIDE = 16
CMP_HID = 256
SLC_LEN = 64
SLC_TOPN = 16
WIN = 512
MLA_HEADS = 8
MLA_NOPE = 64
MLA_ROPE = 32
MLA_DV = 64
MLA_Q_LORA = 768
MLA_KV_LORA = 256
ROPE_THETA = 10000.0
MEM_LEN = 256
MEM_HEADS = 4
MEM_DH = 128
N_BRANCH = 3
BRANCH_W = NSA_HEADS * NSA_DH
REL_BUCKETS = 32
REL_MAX_DIST = 128
N_EXPERTS = 256
TOP_K = 8
N_EXPERT_GROUPS = 8
TOPK_GROUPS = 4
D_EXPERT = 256
D_SHARED = 256
ROUTE_SCALE = 2.5
EXPERT_BLOCK = 128
Q_BLOCK = 128
LN_EPS = 1e-5
RMS_EPS = 1e-6
NEG = -1e30
BIG = 1e30
IN_SPLITS = (NSA_HEADS * NSA_DH, 6 * NSA_GROUPS * NSA_DH, 3 * NSA_HEADS, MLA_Q_LORA, MLA_KV_LORA, MLA_ROPE, MEM_HEADS * MEM_DH, N_BRANCH * D_MODEL)
D_IN = sum(IN_SPLITS)

kernel_name = 'hybrid_nsa_mla_mem_moe_deepnorm'


def _split_points():
    pts, acc = [], 0
    for n in IN_SPLITS[:-1]:
        acc += n
        pts.append(acc)
    return pts


def layer_norm(x, g, b):
    xf = x.astype(jnp.float32)
    mu = jnp.mean(xf, -1, keepdims=True)
    var = jnp.mean(jnp.square(xf - mu), -1, keepdims=True)
    return ((xf - mu) * lax.rsqrt(var + LN_EPS) * g + b).astype(x.dtype)


def rms_norm(x, g):
    xf = x.astype(jnp.float32)
    return (xf * lax.rsqrt(jnp.mean(jnp.square(xf), -1, keepdims=True) + RMS_EPS) * g).astype(x.dtype)


def masked_softmax(s, mask):
    s = jnp.where(mask, s.astype(jnp.float32), NEG)
    p = jax.nn.softmax(s, axis=-1)
    return jnp.where(mask, p, 0.0)


def rel_bucket(dist):
    n = jnp.maximum(dist, 0)
    max_exact = REL_BUCKETS // 2
    nf = jnp.maximum(n, 1).astype(jnp.float32)
    large = max_exact + (jnp.log(nf / max_exact) / math.log(REL_MAX_DIST / max_exact) * (REL_BUCKETS - max_exact)).astype(jnp.int32)
    large = jnp.minimum(large, REL_BUCKETS - 1)
    return jnp.where(n < max_exact, n, large)


def rope(x, pos):
    half = x.shape[-1] // 2
    freq = ROPE_THETA ** (-jnp.arange(half, dtype=jnp.float32) / half)
    ang = pos.astype(jnp.float32)[:, None] * freq[None, :]
    cos = jnp.cos(ang)[:, None, :]
    sin = jnp.sin(ang)[:, None, :]
    xf = x.astype(jnp.float32)
    x1, x2 = xf[..., :half], xf[..., half:]
    return jnp.concatenate([x1 * cos - x2 * sin, x2 * cos + x1 * sin], -1).astype(x.dtype)


def swiglu(x, w1, w3, w2):
    return (jax.nn.silu(x @ w1) * (x @ w3)) @ w2


def cmp_to_slc_overlap(n_cmp, n_slc):
    cs = np.arange(n_cmp) * CMP_STRIDE
    ce = cs + CMP_LEN - 1
    js = np.arange(n_slc) * SLC_LEN
    je = js + SLC_LEN - 1
    return ((cs[:, None] <= je[None, :]) & (ce[:, None] >= js[None, :])).astype(np.float32)


def nsa_compress(k, pos_emb, w1, w2):
    B, S, G, DH = k.shape
    n_cmp = (S - CMP_LEN) // CMP_STRIDE + 1
    idx = np.arange(n_cmp)[:, None] * CMP_STRIDE + np.arange(CMP_LEN)[None, :]
    blk = k[:, idx] + pos_emb[:, None, :]
    blk = blk.transpose(0, 1, 3, 2, 4).reshape(B, n_cmp, G, CMP_LEN * DH)
    return jax.nn.gelu(blk @ w1) @ w2


def nsa_attend(q, kc, vc, k_slc, v_slc, k_win, v_win, gates, rel_table):
    B, S = q.shape[0], q.shape[1]
    G, HPG, DH = NSA_GROUPS, NSA_HPG, NSA_DH
    n_cmp = kc.shape[1]
    n_slc = S // SLC_LEN
    n_top = min(SLC_TOPN, n_slc)
    qg = q.reshape(B, S, G, HPG, DH) * (DH ** -0.5)
    cmp_end = jnp.arange(n_cmp) * CMP_STRIDE + CMP_LEN - 1
    overlap = jnp.asarray(cmp_to_slc_overlap(n_cmp, n_slc))
    ks_blk = k_slc.reshape(B, n_slc, SLC_LEN, G, DH).transpose(0, 3, 1, 2, 4)
    vs_blk = v_slc.reshape(B, n_slc, SLC_LEN, G, DH).transpose(0, 3, 1, 2, 4)
    kw_pad = jnp.pad(k_win, ((0, 0), (WIN, 0), (0, 0), (0, 0)))
    vw_pad = jnp.pad(v_win, ((0, 0), (WIN, 0), (0, 0), (0, 0)))
    tab = rel_table.reshape(REL_BUCKETS, G, HPG)
    b_idx = jnp.arange(B)[:, None, None, None]
    g_idx = jnp.arange(G)[None, :, None, None]
    j_all = jnp.arange(n_slc)

    def block(i):
        qs = i * Q_BLOCK
        t = qs + jnp.arange(Q_BLOCK)
        qb = lax.dynamic_slice_in_dim(qg, qs, Q_BLOCK, axis=1)
        gb = lax.dynamic_slice_in_dim(gates, qs, Q_BLOCK, axis=1)

        dist_c = t[:, None] - cmp_end[None, :]
        mask_c = dist_c >= 0
        bias_c = rel_table[rel_bucket(dist_c)].reshape(Q_BLOCK, n_cmp, G, HPG).transpose(2, 3, 0, 1)
        s_c = jnp.einsum('bqghd,bngd->bghqn', qb, kc) + bias_c
        p_c = masked_softmax(s_c, mask_c)
        o_c = jnp.einsum('bghqn,bngd->bqghd', p_c.astype(vc.dtype), vc)

        imp = jnp.einsum('bghqn,nj->bgqj', p_c, overlap)
        cur = t // SLC_LEN
        forced = (j_all[None, :] == 0) | (j_all[None, :] == cur[:, None]) | (j_all[None, :] == cur[:, None] - 1)
        future = j_all[None, :] * SLC_LEN > t[:, None]
        imp = jnp.where(future, NEG, jnp.where(forced, BIG, imp))
        top_score, top_idx = lax.top_k(imp, n_top)
        blk_ok = top_score > 0.5 * NEG
        ks = ks_blk[b_idx, g_idx, top_idx].reshape(B, G, Q_BLOCK, n_top * SLC_LEN, DH)
        vs = vs_blk[b_idx, g_idx, top_idx].reshape(B, G, Q_BLOCK, n_top * SLC_LEN, DH)
        pos_s = top_idx[..., None] * SLC_LEN + jnp.arange(SLC_LEN)
        dist_s = t[None, None, :, None, None] - pos_s
        mask_s = (blk_ok[..., None] & (dist_s >= 0)).reshape(B, G, 1, Q_BLOCK, n_top * SLC_LEN)
        bias_s = tab[rel_bucket(dist_s), g_idx[..., None]]
        bias_s = bias_s.reshape(B, G, Q_BLOCK, n_top * SLC_LEN, HPG).transpose(0, 1, 4, 2, 3)
        s_s = jnp.einsum('bqghd,bgqkd->bghqk', qb, ks) + bias_s
        p_s = masked_softmax(s_s, mask_s)
        o_s = jnp.einsum('bghqk,bgqkd->bqghd', p_s.astype(vs.dtype), vs)

        kw = lax.dynamic_slice_in_dim(kw_pad, qs, Q_BLOCK + WIN, axis=1)
        vw = lax.dynamic_slice_in_dim(vw_pad, qs, Q_BLOCK + WIN, axis=1)
        s_pos = qs - WIN + jnp.arange(Q_BLOCK + WIN)
        dist_w = t[:, None] - s_pos[None, :]
        mask_w = (dist_w >= 0) & (dist_w < WIN) & (s_pos[None, :] >= 0)
        bias_w = rel_table[rel_bucket(dist_w)].reshape(Q_BLOCK, Q_BLOCK + WIN, G, HPG).transpose(2, 3, 0, 1)
        s_w = jnp.einsum('bqghd,bkgd->bghqk', qb, kw) + bias_w
        p_w = masked_softmax(s_w, mask_w)
        o_w = jnp.einsum('bghqk,bkgd->bqghd', p_w.astype(vw.dtype), vw)

        g = jax.nn.sigmoid(gb.astype(jnp.float32)).astype(q.dtype).reshape(B, Q_BLOCK, G, HPG, 3)
        o = g[..., 0:1] * o_c + g[..., 1:2] * o_s + g[..., 2:3] * o_w
        return o.reshape(B, Q_BLOCK, NSA_HEADS * DH)

    out = lax.map(block, jnp.arange(S // Q_BLOCK))
    return out.transpose(1, 0, 2, 3).reshape(B, S, NSA_HEADS * DH)


def mla_attend(c_q, c_kv, k_rope_raw, q_norm, w_uq, kv_norm, w_ukv):
    B, S = c_q.shape[0], c_q.shape[1]
    H = MLA_HEADS
    pos = jnp.arange(S)
    q = (rms_norm(c_q, q_norm) @ w_uq).reshape(B, S, H, MLA_NOPE + MLA_ROPE)
    q = jnp.concatenate([q[..., :MLA_NOPE], rope(q[..., MLA_NOPE:], pos)], -1) * ((MLA_NOPE + MLA_ROPE) ** -0.5)
    kv = (rms_norm(c_kv, kv_norm) @ w_ukv).reshape(B, S, H, MLA_NOPE + MLA_DV)
    k_rope = rope(k_rope_raw[:, :, None, :], pos)
    k = jnp.concatenate([kv[..., :MLA_NOPE], jnp.broadcast_to(k_rope, (B, S, H, MLA_ROPE))], -1)
    v = kv[..., MLA_NOPE:]
    key_pos = jnp.arange(S)

    def block(i):
        qs = i * Q_BLOCK
        t = qs + jnp.arange(Q_BLOCK)
        qb = lax.dynamic_slice_in_dim(q, qs, Q_BLOCK, axis=1)
        s = jnp.einsum('bqhd,bkhd->bhqk', qb, k)
        p = masked_softmax(s, key_pos[None, :] <= t[:, None])
        return jnp.einsum('bhqk,bkhd->bqhd', p.astype(v.dtype), v).reshape(B, Q_BLOCK, H * MLA_DV)

    out = lax.map(block, jnp.arange(S // Q_BLOCK))
    return out.transpose(1, 0, 2, 3).reshape(B, S, H * MLA_DV)


def mem_attend(q, mem, w_mem_kv):
    B, S = q.shape[0], q.shape[1]
    M = mem.shape[1]
    kv = (mem @ w_mem_kv).reshape(B, M, 2, MEM_HEADS, MEM_DH)
    q = q.reshape(B, S, MEM_HEADS, MEM_DH) * (MEM_DH ** -0.5)
    s = jnp.einsum('bshd,bmhd->bhsm', q, kv[:, :, 0]).astype(jnp.float32)
    p = jax.nn.softmax(s, axis=-1)
    o = jnp.einsum('bhsm,bmhd->bshd', p.astype(kv.dtype), kv[:, :, 1])
    return o.reshape(B, S, MEM_HEADS * MEM_DH)


def moe_ffn(h, router_w, router_b, w1, w3, w2, sw1, sw3, sw2):
    B, S, D = h.shape
    N = B * S
    t = h.reshape(N, D)
    s = jax.nn.sigmoid((t @ router_w).astype(jnp.float32))
    sb = s + router_b.astype(jnp.float32)
    grp = sb.reshape(N, N_EXPERT_GROUPS, N_EXPERTS // N_EXPERT_GROUPS)
    grp_score = jnp.sum(lax.top_k(grp, 2)[0], -1)
    _, gsel = lax.top_k(grp_score, TOPK_GROUPS)
    gmask = jnp.any(gsel[..., None] == jnp.arange(N_EXPERT_GROUPS), axis=1)
    emask = jnp.repeat(gmask, N_EXPERTS // N_EXPERT_GROUPS, axis=1)
    _, eidx = lax.top_k(jnp.where(emask, sb, NEG), TOP_K)
    wsel = jnp.take_along_axis(s, eidx, axis=1)
    wsel = wsel / jnp.sum(wsel, -1, keepdims=True) * ROUTE_SCALE

    A = N * TOP_K
    flat_e = eidx.reshape(A)
    order = jnp.argsort(flat_e)
    sorted_e = flat_e[order]
    counts = jnp.bincount(flat_e, length=N_EXPERTS)
    padded = (counts + EXPERT_BLOCK - 1) // EXPERT_BLOCK * EXPERT_BLOCK
    pad_end = jnp.cumsum(padded)
    pad_start = pad_end - padded
    grp_start = jnp.cumsum(counts) - counts
    dest = pad_start[sorted_e] + jnp.arange(A) - grp_start[sorted_e]
    n_blocks = -(-A // EXPERT_BLOCK) + N_EXPERTS
    P = n_blocks * EXPERT_BLOCK
    row_tok = jnp.full((P,), N, jnp.int32).at[dest].set((order // TOP_K).astype(jnp.int32))
    row_w = jnp.zeros((P,), jnp.float32).at[dest].set(wsel.reshape(A)[order])
    blk_e = jnp.minimum(jnp.searchsorted(pad_end, jnp.arange(n_blocks) * EXPERT_BLOCK, side='right'), N_EXPERTS - 1)
    t_pad = jnp.concatenate([t, jnp.zeros((1, D), t.dtype)], 0)

    def run(args):
        rows, e = args
        return swiglu(t_pad[rows], w1[e], w3[e], w2[e])

    y = lax.map(run, (row_tok.reshape(n_blocks, EXPERT_BLOCK), blk_e)).reshape(P, D)
    routed = jax.ops.segment_sum(y * row_w[:, None].astype(y.dtype), row_tok, num_segments=N + 1)[:N]
    return (routed + swiglu(t, sw1, sw3, sw2)).reshape(B, S, D)


def setup_inputs(seed: int = 0) -> dict:
    key = jax.random.key(seed)
    ks = iter(jax.random.split(key, 40))
    beta = (8 * DEPTH) ** -0.25
    f32 = jnp.float32

    def w(shape, fan_in, scale=1.0):
        return jax.random.normal(next(ks), shape, f32) * (scale * fan_in ** -0.5)

    def gain(shape):
        return 1.0 + 0.02 * jax.random.normal(next(ks), shape, f32)

    def small(shape, scale):
        return scale * jax.random.normal(next(ks), shape, f32)

    L, D = DEPTH, D_MODEL
    return {
        'x': jax.random.normal(next(ks), (BATCH, SEQ, D), f32),
        'mem': jax.random.normal(next(ks), (BATCH, MEM_LEN, D), f32),
        'ln0_g': gain((D,)),
        'ln0_b': small((D,), 0.02),
        'rel_bias': small((REL_BUCKETS, NSA_HEADS), 0.2),
        'w_in': w((L, D, D_IN), D),
        'cmp_pos_k': small((L, CMP_LEN, NSA_DH), 0.1),
        'cmp_pos_v': small((L, CMP_LEN, NSA_DH), 0.1),
        'cmp_k_w1': w((L, CMP_LEN * NSA_DH, CMP_HID), CMP_LEN * NSA_DH),
        'cmp_k_w2': w((L, CMP_HID, NSA_DH), CMP_HID),
        'cmp_v_w1': w((L, CMP_LEN * NSA_DH, CMP_HID), CMP_LEN * NSA_DH),
        'cmp_v_w2': w((L, CMP_HID, NSA_DH), CMP_HID),
        'mla_q_norm': gain((L, MLA_Q_LORA)),
        'mla_w_uq': w((L, MLA_Q_LORA, MLA_HEADS * (MLA_NOPE + MLA_ROPE)), MLA_Q_LORA),
        'mla_kv_norm': gain((L, MLA_KV_LORA)),
        'mla_w_ukv': w((L, MLA_KV_LORA, MLA_HEADS * (MLA_NOPE + MLA_DV)), MLA_KV_LORA),
        'mem_w_kv': w((L, D, 2 * MEM_HEADS * MEM_DH), D),
        'w_branch': w((L, N_BRANCH, BRANCH_W, D), BRANCH_W, beta),
        'w_out': w((L, D, D), D, beta),
        'ln1_g': gain((L, D)),
        'ln1_b': small((L, D), 0.02),
        'router_w': w((L, D, N_EXPERTS), D),
        'router_b': small((L, N_EXPERTS), 0.01),
        'exp_w1': w((L, N_EXPERTS, D, D_EXPERT), D),
        'exp_w3': w((L, N_EXPERTS, D, D_EXPERT), D),
        'exp_w2': w((L, N_EXPERTS, D_EXPERT, D), D_EXPERT, beta),
        'sh_w1': w((L, D, D_SHARED), D),
        'sh_w3': w((L, D, D_SHARED), D),
        'sh_w2': w((L, D_SHARED, D), D_SHARED, beta),
        'ln2_g': gain((L, D)),
        'ln2_b': small((L, D), 0.02),
    }


def reference(x, mem, ln0_g, ln0_b, rel_bias, w_in, cmp_pos_k, cmp_pos_v, cmp_k_w1, cmp_k_w2, cmp_v_w1, cmp_v_w2, mla_q_norm, mla_w_uq, mla_kv_norm, mla_w_ukv, mem_w_kv, w_branch, w_out, ln1_g, ln1_b, router_w, router_b, exp_w1, exp_w3, exp_w2, sh_w1, sh_w3, sh_w2, ln2_g, ln2_b):
    alpha = (2 * DEPTH) ** 0.25
    B, S, D = x.shape
    h = layer_norm(x, ln0_g, ln0_b)
    for l in range(DEPTH):
        proj = h @ w_in[l]
        q_nsa, kv_nsa, g_nsa, c_q, c_kv, k_rope, q_mem, g_merge = jnp.split(proj, _split_points(), axis=-1)
        kv_nsa = kv_nsa.reshape(B, S, 6, NSA_GROUPS, NSA_DH)
        kc = nsa_compress(kv_nsa[:, :, 0], cmp_pos_k[l], cmp_k_w1[l], cmp_k_w2[l])
        vc = nsa_compress(kv_nsa[:, :, 1], cmp_pos_v[l], cmp_v_w1[l], cmp_v_w2[l])
        o_nsa = nsa_attend(q_nsa.reshape(B, S, NSA_HEADS, NSA_DH), kc, vc,
                           kv_nsa[:, :, 2], kv_nsa[:, :, 3], kv_nsa[:, :, 4], kv_nsa[:, :, 5],
                           g_nsa.reshape(B, S, NSA_HEADS, 3), rel_bias)
        o_mla = mla_attend(c_q, c_kv, k_rope, mla_q_norm[l], mla_w_uq[l], mla_kv_norm[l], mla_w_ukv[l])
        o_mem = mem_attend(q_mem, mem, mem_w_kv[l])
        gates = jax.nn.sigmoid(g_merge.astype(jnp.float32)).astype(h.dtype).reshape(B, S, N_BRANCH, D)
        merged = (gates[:, :, 0] * (o_nsa @ w_branch[l, 0])
                  + gates[:, :, 1] * (o_mla @ w_branch[l, 1])
                  + gates[:, :, 2] * (o_mem @ w_branch[l, 2]))
        h = layer_norm(alpha * h + merged @ w_out[l], ln1_g[l], ln1_b[l])
        ffn = moe_ffn(h, router_w[l], router_b[l], exp_w1[l], exp_w3[l], exp_w2[l], sh_w1[l], sh_w3[l], sh_w2[l])
        h = layer_norm(alpha * h + ffn, ln2_g[l], ln2_b[l])
    return h
```

```python
import functools
import math

import numpy as np
import jax
import jax.numpy as jnp
from jax import lax
from jax.experimental import pallas as pl
from jax.experimental.pallas import tpu as pltpu

BF = jnp.bfloat16
F32 = jnp.float32
I32 = jnp.int32

D_MODEL = 1024
DEPTH = 1
NSA_HEADS = 8
NSA_GROUPS = 2
NSA_HPG = NSA_HEADS // NSA_GROUPS
NSA_DH = 64
CMP_LEN = 32
CMP_STRIDE = 16
CMP_HID = 256
SLC_LEN = 64
SLC_TOPN = 16
WIN = 512
MLA_HEADS = 8
MLA_NOPE = 64
MLA_ROPE = 32
MLA_DV = 64
MLA_Q_LORA = 768
MLA_KV_LORA = 256
ROPE_THETA = 10000.0
MEM_HEADS = 4
MEM_DH = 128
N_BRANCH = 3
BRANCH_W = NSA_HEADS * NSA_DH
REL_BUCKETS = 32
REL_MAX_DIST = 128
N_EXPERTS = 256
TOP_K = 8
N_EXPERT_GROUPS = 8
TOPK_GROUPS = 4
D_EXPERT = 256
ROUTE_SCALE = 2.5
EXPERT_BLOCK = 128
LN_EPS = 1e-5
RMS_EPS = 1e-6
NEG = -1e30
BIG = 1e30
ALPHA = (2 * DEPTH) ** 0.25
IN_SPLITS = (NSA_HEADS * NSA_DH, 6 * NSA_GROUPS * NSA_DH, 3 * NSA_HEADS, MLA_Q_LORA, MLA_KV_LORA,
             MLA_ROPE, MEM_HEADS * MEM_DH, N_BRANCH * D_MODEL)

LANE = 128
HEAD_PAD = 128
ROPE_LANE0 = MLA_NOPE
ROPE_HALF = MLA_ROPE // 2
VMEM_LIMIT = 56 * 1024 * 1024
NSA_TQ = 128
FAR_CHUNK = 512


def _cparams(sem):
    return pltpu.CompilerParams(dimension_semantics=sem, vmem_limit_bytes=VMEM_LIMIT)


def _bucket_starts():
    max_exact = REL_BUCKETS // 2
    d = np.arange(0, 4 * REL_MAX_DIST)
    nf = np.maximum(d, 1).astype(np.float32)
    large = max_exact + (np.log(nf / np.float32(max_exact)) / np.float32(math.log(REL_MAX_DIST / max_exact))
                         * np.float32(REL_BUCKETS - max_exact)).astype(np.int32)
    large = np.minimum(large, REL_BUCKETS - 1)
    bucket = np.where(d < max_exact, d, large)
    return [int(np.argmax(bucket >= b)) for b in range(REL_BUCKETS)]


BUCKET_START = _bucket_starts()
FAR_DIST = BUCKET_START[REL_BUCKETS - 1]


def _rel_bias(dist, tab_ref, head):
    val = jnp.full(dist.shape, tab_ref[REL_BUCKETS - 1, head], F32)
    for b in range(REL_BUCKETS - 2, -1, -1):
        val = jnp.where(dist < BUCKET_START[b + 1], tab_ref[b, head], val)
    return val


def _layer_norm(x, g, b):
    mu = jnp.mean(x, -1, keepdims=True)
    xc = x - mu
    var = jnp.mean(xc * xc, -1, keepdims=True)
    return xc * lax.rsqrt(var + LN_EPS) * g + b


def _dot(a, b):
    return jnp.dot(a, b, preferred_element_type=F32)


def _dot_nt(a, b):
    return lax.dot_general(a, b, (((1,), (1,)), ((), ())), preferred_element_type=F32)


def _ln_inproj_kernel(x_ref, g_ref, b_ref, wq, wkv, wg, wcq, wckv, wkr, wqm, wgm,
                      h_ref, oq, okv, og, ocq, ockv, okr, oqm, ogm):
    h = _layer_norm(x_ref[...], g_ref[...], b_ref[...])
    h_ref[...] = h
    hb = h.astype(BF)
    for w, o in ((wq, oq), (wkv, okv), (wg, og), (wcq, ocq), (wckv, ockv), (wkr, okr), (wgm, ogm)):
        o[...] = _dot(hb, w[...]).astype(o.dtype)
    oqm[...] = (_dot(hb, wqm[...]) * (MEM_DH ** -0.5)).astype(oqm.dtype)


def _ln_inproj(x2, g, b, ws, tm):
    n = x2.shape[0]
    row = lambda i: (i, 0)
    const = lambda i: (0, 0)
    in_specs = [pl.BlockSpec((tm, D_MODEL), row), pl.BlockSpec((1, D_MODEL), const), pl.BlockSpec((1, D_MODEL), const)]
    in_specs += [pl.BlockSpec(w.shape, const) for w in ws]
    out_shape = [jax.ShapeDtypeStruct((n, D_MODEL), F32)]
    out_shape += [jax.ShapeDtypeStruct((n, w.shape[1]), BF) for w in ws]
    out_specs = [pl.BlockSpec((tm, D_MODEL), row)] + [pl.BlockSpec((tm, w.shape[1]), row) for w in ws]
    return pl.pallas_call(
        _ln_inproj_kernel, grid=(n // tm,), in_specs=in_specs, out_specs=out_specs, out_shape=out_shape,
        compiler_params=_cparams(("parallel",)),
    )(x2, g, b, *ws)


def _rope_lanes(x, c, s1, s2):
    return x * c + pltpu.roll(x, LANE - ROPE_HALF, 1) * s1 + pltpu.roll(x, ROPE_HALF, 1) * s2


def _mla_prep_kernel(cq_ref, ckv_ref, kr_ref, qn_ref, kvn_ref, wuq, wuk, wuv,
                     cq_t, s1q_t, s2q_t, ck_t, s1k_t, s2k_t, q_out, k_out, v_out):
    cq = cq_ref[...].astype(F32)
    rq = cq * lax.rsqrt(jnp.mean(cq * cq, -1, keepdims=True) + RMS_EPS) * qn_ref[...]
    q = _dot(rq.astype(BF), wuq[...])
    ckv = ckv_ref[...].astype(F32)
    rkv = (ckv * lax.rsqrt(jnp.mean(ckv * ckv, -1, keepdims=True) + RMS_EPS) * kvn_ref[...]).astype(BF)
    kn = _dot(rkv, wuk[...])
    v_out[...] = _dot(rkv, wuv[...]).astype(v_out.dtype)
    kr = _rope_lanes(kr_ref[...].astype(F32), ck_t[...], s1k_t[...], s2k_t[...])
    cq_c, s1q, s2q = cq_t[...], s1q_t[...], s2q_t[...]
    for h in range(MLA_HEADS):
        sl = slice(h * HEAD_PAD, (h + 1) * HEAD_PAD)
        q_out[:, sl] = _rope_lanes(q[:, sl], cq_c, s1q, s2q).astype(q_out.dtype)
        k_out[:, sl] = (kn[:, sl] + kr).astype(k_out.dtype)


def _rope_tables(seq, scale):
    freq = ROPE_THETA ** (-jnp.arange(ROPE_HALF, dtype=F32) / ROPE_HALF)
    ang = jnp.arange(seq, dtype=F32)[:, None] * freq[None, :]
    cos, sin = jnp.cos(ang) * scale, jnp.sin(ang) * scale
    z = lambda w: jnp.zeros((seq, w), F32)
    tail = HEAD_PAD - ROPE_LANE0 - MLA_ROPE
    c = jnp.concatenate([jnp.full((seq, ROPE_LANE0), scale, F32), cos, cos, z(tail)], 1)
    s1 = jnp.concatenate([z(ROPE_LANE0), -sin, z(ROPE_HALF + tail)], 1)
    s2 = jnp.concatenate([z(ROPE_LANE0 + ROPE_HALF), sin, z(tail)], 1)
    return c, s1, s2


def _mla_prep(cq, ckv, kr, q_norm, kv_norm, wuq, wuk, wuv, seq, tm):
    n = cq.shape[0]
    nt = seq // tm
    row = lambda i: (i, 0)
    const = lambda i: (0, 0)
    pos = lambda i: (i % nt, 0)
    tabs = _rope_tables(seq, (MLA_NOPE + MLA_ROPE) ** -0.5) + _rope_tables(seq, 1.0)
    in_specs = [pl.BlockSpec((tm, MLA_Q_LORA), row), pl.BlockSpec((tm, MLA_KV_LORA), row), pl.BlockSpec((tm, LANE), row),
                pl.BlockSpec((1, MLA_Q_LORA), const), pl.BlockSpec((1, MLA_KV_LORA), const),
                pl.BlockSpec(wuq.shape, const), pl.BlockSpec(wuk.shape, const), pl.BlockSpec(wuv.shape, const)]
    in_specs += [pl.BlockSpec((tm, LANE), pos)] * 6
    hq = MLA_HEADS * HEAD_PAD
    hv = MLA_HEADS * MLA_DV
    return pl.pallas_call(
        _mla_prep_kernel, grid=(n // tm,), in_specs=in_specs,
        out_specs=[pl.BlockSpec((tm, hq), row), pl.BlockSpec((tm, hq), row), pl.BlockSpec((tm, hv), row)],
        out_shape=[jax.ShapeDtypeStruct((n, hq), BF), jax.ShapeDtypeStruct((n, hq), BF), jax.ShapeDtypeStruct((n, hv), BF)],
        compiler_params=_cparams(("parallel",)),
    )(cq, ckv, kr, q_norm, kv_norm, wuq, wuk, wuv, *tabs)


def _mla_attn_kernel(qi_tab, ki_tab, q_ref, k_ref, v_ref, o_ref, m_scr, l_scr, acc_scr, *, tq, tk):
    p = pl.program_id(2)
    qi = qi_tab[p]
    ki = ki_tab[p]

    @pl.when(ki == 0)
    def _():
        m_scr[...] = jnp.full(m_scr.shape, NEG, F32)
        l_scr[...] = jnp.zeros(l_scr.shape, F32)
        acc_scr[...] = jnp.zeros(acc_scr.shape, F32)

    rows = qi * tq + lax.broadcasted_iota(I32, (tq, tk), 0)
    cols = ki * tk + lax.broadcasted_iota(I32, (tq, tk), 1)
    causal = cols <= rows
    v = v_ref[...]
    for hh in range(2):
        sl = slice(hh * HEAD_PAD, (hh + 1) * HEAD_PAD)
        s = jnp.where(causal, _dot_nt(q_ref[:, sl], k_ref[:, sl]), NEG)
        m_prev = m_scr[hh]
        m_new = jnp.maximum(m_prev, jnp.max(s, 1, keepdims=True))
        a = jnp.exp(m_prev - m_new)
        e = jnp.exp(s - m_new)
        l_scr[hh] = a * l_scr[hh] + jnp.sum(e, 1, keepdims=True)
        acc_scr[hh] = a * acc_scr[hh] + _dot(e.astype(BF), v)
        m_scr[hh] = m_new

    @pl.when(ki == qi)
    def _():
        lane = lax.broadcasted_iota(I32, (tq, 2 * MLA_DV), 1)
        o = jnp.where(lane < MLA_DV, acc_scr[0] / l_scr[0], acc_scr[1] / l_scr[1])
        o_ref[...] = o.astype(o_ref.dtype)


def _mla_attn(q, k, v, batch, seq, t):
    n = q.shape[0]
    nt = seq // t
    pairs = [(a, b) for a in range(nt) for b in range(a + 1)]
    qi_tab = jnp.asarray([a for a, _ in pairs], I32)
    ki_tab = jnp.asarray([b for _, b in pairs], I32)
    qmap = lambda b, hp, p, qt, kt: (b * nt + qt[p], hp)
    kmap = lambda b, hp, p, qt, kt: (b * nt + kt[p], hp)
    grid_spec = pltpu.PrefetchScalarGridSpec(
        num_scalar_prefetch=2, grid=(batch, MLA_HEADS // 2, len(pairs)),
        in_specs=[pl.BlockSpec((t, 2 * HEAD_PAD), qmap), pl.BlockSpec((t, 2 * HEAD_PAD), kmap),
                  pl.BlockSpec((t, 2 * MLA_DV), kmap)],
        out_specs=pl.BlockSpec((t, 2 * MLA_DV), qmap),
        scratch_shapes=[pltpu.VMEM((2, t, 1), F32), pltpu.VMEM((2, t, 1), F32), pltpu.VMEM((2, t, 2 * MLA_DV), F32)])
    return pl.pallas_call(
        functools.partial(_mla_attn_kernel, tq=t, tk=t), grid_spec=grid_spec,
        out_shape=jax.ShapeDtypeStruct((n, MLA_HEADS * MLA_DV), BF),
        compiler_params=_cparams(("parallel", "parallel", "arbitrary")),
    )(qi_tab, ki_tab, q, k, v)


def _nsa_compress_kernel(c_ref, pos_ref, w1_ref, w2_ref, o_ref):
    nc = c_ref.shape[0]
    half = CMP_STRIDE * NSA_DH
    c = c_ref[...]
    top = _dot(c, w1_ref[:half, :])
    bot = _dot(c, w1_ref[half:, :])
    posb = _dot(jnp.broadcast_to(pos_ref[...], (8, 2 * half)).astype(BF), w1_ref[...])[:1]
    hid = top + pltpu.roll(bot, nc - 1, 0) + posb
    o_ref[...] = _dot(jax.nn.gelu(hid).astype(BF), w2_ref[...]).astype(o_ref.dtype)


def _nsa_compress(cc, pos, w1, w2):
    b, _, nc, half = cc.shape
    sq = pl.Squeezed()
    return pl.pallas_call(
        _nsa_compress_kernel, grid=(b, 2 * NSA_GROUPS),
        in_specs=[pl.BlockSpec((sq, sq, nc, half), lambda i, c: (i, c, 0, 0)),
                  pl.BlockSpec((sq, 1, 2 * half), lambda i, c: (c // NSA_GROUPS, 0, 0)),
                  pl.BlockSpec((sq, 2 * half, CMP_HID), lambda i, c: (c // NSA_GROUPS, 0, 0)),
                  pl.BlockSpec((sq, CMP_HID, NSA_DH), lambda i, c: (c // NSA_GROUPS, 0, 0))],
        out_specs=pl.BlockSpec((sq, sq, nc, NSA_DH), lambda i, c: (i, c, 0, 0)),
        out_shape=jax.ShapeDtypeStruct((b, 2 * NSA_GROUPS, nc, NSA_DH), BF),
        compiler_params=_cparams(("parallel", "parallel")),
    )(cc, pos, w1, w2)


CMP_BIAS_COLS = 16
CMP_BIAS_BACK = 9


def _nsa_cmp_kernel(tab_ref, q_ref, kc_ref, vc_ref, ov_ref, oc_ref, sel_ref, e_scr, *, n_top):
    b, g, i = pl.program_id(0), pl.program_id(1), pl.program_id(2)
    tq = NSA_TQ
    nc = kc_ref.shape[0]
    n_slc = ov_ref.shape[1]
    qs = i * tq

    @pl.when((b == 0) & (g == 0) & (i == 0))
    def _():
        q_io = lax.broadcasted_iota(I32, (tq, LANE), 0)
        lane = lax.broadcasted_iota(I32, (tq, LANE), 1)
        jj = lane % CMP_BIAS_COLS
        dist = q_io - CMP_STRIDE * (jj - CMP_BIAS_BACK) - (CMP_LEN - 1)
        live = (dist >= 0) & (lane < 2 * CMP_BIAS_COLS)
        for h in range(NSA_HEADS):
            e = jnp.where(live, _rel_bias(dist, tab_ref, h) - tab_ref[REL_BUCKETS - 1, h], 0.0)
            hi = e.astype(BF)
            lo = (e - hi.astype(F32)).astype(BF)
            e_scr[h] = jnp.where(lane < CMP_BIAS_COLS, hi, lo)

    n0 = qs // CMP_STRIDE
    jrow = lax.broadcasted_iota(I32, (LANE, nc), 0)
    ncol = lax.broadcasted_iota(I32, (LANE, nc), 1)
    ft = jnp.where((ncol == n0 + (jrow % CMP_BIAS_COLS) - CMP_BIAS_BACK) & (jrow < 2 * CMP_BIAS_COLS), 1.0, 0.0).astype(BF)

    t = qs + lax.broadcasted_iota(I32, (tq, nc), 0)
    n_io = lax.broadcasted_iota(I32, (tq, nc), 1)
    mask = (t >= n_io * CMP_STRIDE + (CMP_LEN - 1)) & (n_io < nc - 1)
    kc = kc_ref[...]
    vc = vc_ref[...]
    psum = jnp.zeros((tq, nc), F32)
    for h in range(NSA_HPG):
        head = g * NSA_HPG + h
        qh = q_ref[:, h * NSA_DH:(h + 1) * NSA_DH]
        s = _dot_nt(qh, kc) + _dot(e_scr[head], ft) + tab_ref[REL_BUCKETS - 1, head]
        s = jnp.where(mask, s, NEG)
        m = jnp.max(s, 1, keepdims=True)
        e = jnp.where(mask, jnp.exp(s - m), 0.0)
        l = jnp.sum(e, 1, keepdims=True)
        p = e * jnp.where(l > 0.0, 1.0 / l, 0.0)
        oc_ref[:, h * NSA_DH:(h + 1) * NSA_DH] = _dot(p.astype(BF), vc).astype(oc_ref.dtype)
        psum = psum + p

    ov = ov_ref[...]
    p0 = psum.astype(BF)
    r1 = psum - p0.astype(F32)
    p1 = r1.astype(BF)
    p2 = (r1 - p1.astype(F32)).astype(BF)
    imp = _dot(p0, ov) + _dot(p1, ov) + _dot(p2, ov)

    tj = qs + lax.broadcasted_iota(I32, (tq, n_slc), 0)
    j = lax.broadcasted_iota(I32, (tq, n_slc), 1)
    cur = tj // SLC_LEN
    forced = (j == 0) | (j == cur) | (j == cur - 1)
    work = jnp.where(j * SLC_LEN > tj, NEG, jnp.where(forced, BIG, imp))
    sel = jnp.zeros((tq, n_slc), F32)
    for _ in range(n_top):
        mx = jnp.max(work, 1, keepdims=True)
        first = jnp.min(jnp.where(work == mx, j, n_slc), 1, keepdims=True)
        hit = j == first
        sel = jnp.where(hit & (mx > 0.5 * NEG), 1.0, sel)
        work = jnp.where(hit, -jnp.inf, work)
    sel_ref[...] = sel.astype(sel_ref.dtype)


def _nsa_cmp(tab, qn, kvc, overlap, batch, seq):
    n = qn.shape[0]
    tq = NSA_TQ
    nq = seq // tq
    nc = kvc.shape[2]
    n_slc = overlap.shape[1]
    sq = pl.Squeezed()
    gw = NSA_HPG * NSA_DH
    return pl.pallas_call(
        functools.partial(_nsa_cmp_kernel, n_top=min(SLC_TOPN, n_slc)), grid=(batch, NSA_GROUPS, nq),
        in_specs=[pl.BlockSpec(memory_space=pltpu.SMEM),
                  pl.BlockSpec((tq, gw), lambda b, g, i: (b * nq + i, g)),
                  pl.BlockSpec((sq, sq, nc, NSA_DH), lambda b, g, i: (b, g, 0, 0)),
                  pl.BlockSpec((sq, sq, nc, NSA_DH), lambda b, g, i: (b, NSA_GROUPS + g, 0, 0)),
                  pl.BlockSpec((nc, n_slc), lambda b, g, i: (0, 0))],
        out_specs=[pl.BlockSpec((tq, gw), lambda b, g, i: (b * nq + i, g)),
                   pl.BlockSpec((sq, tq, n_slc), lambda b, g, i: (g, b * nq + i, 0))],
        out_shape=[jax.ShapeDtypeStruct((n, NSA_HEADS * NSA_DH), BF),
                   jax.ShapeDtypeStruct((NSA_GROUPS, n, n_slc), BF)],
        scratch_shapes=[pltpu.VMEM((NSA_HEADS, tq, LANE), BF)],
        compiler_params=_cparams(("arbitrary", "arbitrary", "arbitrary")),
    )(tab, qn, kvc, kvc, overlap)


def _nsa_attn_kernel(tab_ref, q_ref, ks_ref, vs_ref, kw_ref, vw_ref, sel_ref, gate_ref, oc_ref, o_ref,
                     d_scr, m_scr, l_scr, acc_scr):
    b, i = pl.program_id(0), pl.program_id(1)
    tq = NSA_TQ
    rows = NSA_HPG * tq
    n_slc = sel_ref.shape[2]
    qs = i * tq
    near_w = 2 * tq
    far_w = FAR_CHUNK
    win_far_w = WIN - tq

    @pl.when((b == 0) & (i == 0))
    def _():
        q_io = lax.broadcasted_iota(I32, (tq, near_w), 0)
        kk = lax.broadcasted_iota(I32, (tq, near_w), 1)
        dist = jnp.maximum(q_io + tq - kk, 0)
        for h in range(NSA_HEADS):
            d_scr[h] = _rel_bias(dist, tab_ref, h)

    q_io = lax.broadcasted_iota(I32, (tq, near_w), 0)
    kk = lax.broadcasted_iota(I32, (tq, near_w), 1)
    kpos_near = qs - tq + kk
    near_mask = (kpos_near >= 0) & (kpos_near <= qs + q_io)
    start_a = pl.multiple_of(qs, tq)
    start_b = pl.multiple_of(jnp.maximum(qs - tq, 0), tq)
    j_near = lax.broadcasted_iota(I32, (n_slc, near_w), 0)
    k_near = lax.broadcasted_iota(I32, (n_slc, near_w), 1)
    ex_near = jnp.where(j_near == 2 * i - 2 + k_near // SLC_LEN, 1.0, 0.0).astype(BF)

    def q4(g):
        return jnp.concatenate([q_ref[:, (g * NSA_HPG + h) * NSA_DH:(g * NSA_HPG + h + 1) * NSA_DH]
                                for h in range(NSA_HPG)], axis=0)

    def update(slot, s, mask, v):
        w = s.shape[-1]
        s = jnp.where(mask[None], s, NEG).reshape(rows, w)
        m_prev = m_scr[slot]
        m_new = jnp.maximum(m_prev, jnp.max(s, 1, keepdims=True))
        a = jnp.exp(m_prev - m_new)
        e = jnp.exp(s - m_new)
        l_scr[slot] = a * l_scr[slot] + jnp.sum(e, 1, keepdims=True)
        acc_scr[slot] = a * acc_scr[slot] + _dot(e.astype(BF), v)
        m_scr[slot] = m_new

    def far_logits(g, k):
        s = _dot_nt(q4(g), k)
        s = jnp.concatenate([s[h * tq:(h + 1) * tq] + tab_ref[REL_BUCKETS - 1, g * NSA_HPG + h]
                             for h in range(NSA_HPG)], axis=0)
        return s.reshape(NSA_HPG, tq, k.shape[0])

    m_scr[...] = jnp.full(m_scr.shape, NEG, F32)
    l_scr[...] = jnp.zeros(l_scr.shape, F32)
    acc_scr[...] = jnp.zeros(acc_scr.shape, F32)

    for g in range(NSA_GROUPS):
        gl = slice(g * NSA_DH, (g + 1) * NSA_DH)
        qg = q4(g)
        bias = d_scr[g * NSA_HPG:(g + 1) * NSA_HPG]
        sel_near = _dot(sel_ref[g], ex_near) > 0.5
        for br, (k_ref, v_ref) in enumerate(((ks_ref, vs_ref), (kw_ref, vw_ref))):
            k = jnp.concatenate([k_ref[pl.ds(start_b, tq), gl], k_ref[pl.ds(start_a, tq), gl]], axis=0)
            v = jnp.concatenate([v_ref[pl.ds(start_b, tq), gl], v_ref[pl.ds(start_a, tq), gl]], axis=0)
            s = _dot_nt(qg, k).reshape(NSA_HPG, tq, near_w) + bias
            mask = (near_mask & sel_near) if br == 0 else near_mask
            update(2 * g + br, s, mask, v)

    ws = pl.multiple_of(jnp.maximum(qs - WIN, 0), tq)
    kpos_w = ws + lax.broadcasted_iota(I32, (tq, win_far_w), 1)
    t_w = qs + lax.broadcasted_iota(I32, (tq, win_far_w), 0)
    mask_w = (kpos_w < qs - tq) & (kpos_w > t_w - WIN)
    for g in range(NSA_GROUPS):
        gl = slice(g * NSA_DH, (g + 1) * NSA_DH)
        update(2 * g + 1, far_logits(g, kw_ref[pl.ds(ws, win_far_w), gl]), mask_w, vw_ref[pl.ds(ws, win_far_w), gl])

    n_far = (jnp.maximum(i - 1, 0) + (far_w // tq - 1)) // (far_w // tq)
    j_far = lax.broadcasted_iota(I32, (n_slc, far_w), 0)
    k_far = lax.broadcasted_iota(I32, (n_slc, far_w), 1) // SLC_LEN
    kcol = lax.broadcasted_iota(I32, (tq, far_w), 1)

    def far_body(c, carry):
        base = pl.multiple_of(c * far_w, far_w)
        ex = jnp.where(j_far - c * (far_w // SLC_LEN) == k_far, 1.0, 0.0).astype(BF)
        in_range = base + kcol < qs - tq
        for g in range(NSA_GROUPS):
            gl = slice(g * NSA_DH, (g + 1) * NSA_DH)
            mask = (_dot(sel_ref[g], ex) > 0.5) & in_range
            update(2 * g, far_logits(g, ks_ref[pl.ds(base, far_w), gl]), mask, vs_ref[pl.ds(base, far_w), gl])
        return carry

    lax.fori_loop(0, n_far, far_body, 0)

    gates = jax.nn.sigmoid(gate_ref[...].astype(F32))
    for g in range(NSA_GROUPS):
        o_s = acc_scr[2 * g] / l_scr[2 * g]
        o_w = acc_scr[2 * g + 1] / l_scr[2 * g + 1]
        for h in range(NSA_HPG):
            head = g * NSA_HPG + h
            hl = slice(head * NSA_DH, (head + 1) * NSA_DH)
            hr = slice(h * tq, (h + 1) * tq)
            o = (gates[:, 3 * head:3 * head + 1] * oc_ref[:, hl].astype(F32)
                 + gates[:, 3 * head + 1:3 * head + 2] * o_s[hr]
                 + gates[:, 3 * head + 2:3 * head + 3] * o_w[hr])
            o_ref[:, hl] = o.astype(o_ref.dtype)


def _nsa_attn(tab, qn, kvn, selm, gn, oc, batch, seq):
    n = qn.shape[0]
    tq = NSA_TQ
    nq = seq // tq
    n_slc = selm.shape[2]
    hw = NSA_HEADS * NSA_DH
    gw = NSA_GROUPS * NSA_DH
    row = lambda b, i: (b * nq + i, 0)
    kv = lambda kind: pl.BlockSpec((seq, gw), lambda b, i: (b, kind))
    rows = NSA_HPG * tq
    return pl.pallas_call(
        _nsa_attn_kernel, grid=(batch, nq),
        in_specs=[pl.BlockSpec(memory_space=pltpu.SMEM), pl.BlockSpec((tq, hw), row),
                  kv(2), kv(3), kv(4), kv(5),
                  pl.BlockSpec((NSA_GROUPS, tq, n_slc), lambda b, i: (0, b * nq + i, 0)),
                  pl.BlockSpec((tq, LANE), row), pl.BlockSpec((tq, hw), row)],
        out_specs=pl.BlockSpec((tq, hw), row),
        out_shape=jax.ShapeDtypeStruct((n, hw), BF),
        scratch_shapes=[pltpu.VMEM((NSA_HEADS, tq, 2 * tq), F32),
                        pltpu.VMEM((2 * NSA_GROUPS, rows, 1), F32), pltpu.VMEM((2 * NSA_GROUPS, rows, 1), F32),
                        pltpu.VMEM((2 * NSA_GROUPS, rows, NSA_DH), F32)],
        compiler_params=_cparams(("arbitrary", "arbitrary")),
    )(tab, qn, kvn, kvn, kvn, kvn, selm, gn, oc)


def _mem_attn_kernel(q_ref, mem_ref, w_ref, o_ref, kv_scr):
    @pl.when(pl.program_id(1) == 0)
    def _():
        kv_scr[...] = _dot(mem_ref[...].astype(BF), w_ref[...]).astype(BF)

    hw = MEM_HEADS * MEM_DH
    for h in range(MEM_HEADS):
        sl = slice(h * MEM_DH, (h + 1) * MEM_DH)
        s = _dot_nt(q_ref[:, sl], kv_scr[:, sl])
        e = jnp.exp(s - jnp.max(s, 1, keepdims=True))
        p = e / jnp.sum(e, 1, keepdims=True)
        o_ref[:, sl] = _dot(p.astype(BF), kv_scr[:, hw + h * MEM_DH:hw + (h + 1) * MEM_DH]).astype(o_ref.dtype)


def _mem_attn(qm, mem2, w_kv, batch, seq, tq):
    n = qm.shape[0]
    nq = seq // tq
    m = mem2.shape[0] // batch
    hw = MEM_HEADS * MEM_DH
    return pl.pallas_call(
        _mem_attn_kernel, grid=(batch, nq),
        in_specs=[pl.BlockSpec((tq, hw), lambda b, i: (b * nq + i, 0)),
                  pl.BlockSpec((m, D_MODEL), lambda b, i: (b, 0)),
                  pl.BlockSpec((D_MODEL, 2 * hw), lambda b, i: (0, 0))],
        out_specs=pl.BlockSpec((tq, hw), lambda b, i: (b * nq + i, 0)),
        out_shape=jax.ShapeDtypeStruct((n, hw), BF),
        scratch_shapes=[pltpu.VMEM((m, 2 * hw), BF)],
        compiler_params=_cparams(("arbitrary", "arbitrary")),
    )(qm, mem2, w_kv)


def _merge_kernel(on_ref, ol_ref, om_ref, gm_ref, h_ref, wb_ref, wo_ref, g_ref, b_ref, h1_ref):
    merged = None
    for c, o_ref in enumerate((on_ref, ol_ref, om_ref)):
        gate = jax.nn.sigmoid(gm_ref[:, c * D_MODEL:(c + 1) * D_MODEL].astype(F32))
        term = gate * _dot(o_ref[...], wb_ref[c])
        merged = term if merged is None else merged + term
    y = ALPHA * h_ref[...] + _dot(merged.astype(BF), wo_ref[...])
    h1_ref[...] = _layer_norm(y, g_ref[...], b_ref[...])


def _merge(o_nsa, o_mla, o_mem, gm, h, wb, wo, g, b, tm):
    n = h.shape[0]
    row = lambda i: (i, 0)
    const = lambda i: (0, 0)
    return pl.pallas_call(
        _merge_kernel, grid=(n // tm,),
        in_specs=[pl.BlockSpec((tm, BRANCH_W), row)] * 3 + [
            pl.BlockSpec((tm, N_BRANCH * D_MODEL), row), pl.BlockSpec((tm, D_MODEL), row),
            pl.BlockSpec(wb.shape, lambda i: (0, 0, 0)), pl.BlockSpec(wo.shape, const),
            pl.BlockSpec((1, D_MODEL), const), pl.BlockSpec((1, D_MODEL), const)],
        out_specs=pl.BlockSpec((tm, D_MODEL), row),
        out_shape=jax.ShapeDtypeStruct((n, D_MODEL), F32),
        compiler_params=_cparams(("parallel",)),
    )(o_nsa, o_mla, o_mem, gm, h, wb, wo, g, b)


def _first_max(vals, idx, limit):
    mx = jnp.max(vals, 0, keepdims=True)
    first = jnp.min(jnp.where(vals == mx, idx, limit), 0, keepdims=True)
    return mx, first


def _router_kernel(h_ref, whi_ref, wlo_ref, b_ref, idx_ref, w_ref):
    h = h_ref[...]
    hhi = h.astype(BF)
    hlo = (h - hhi.astype(F32)).astype(BF)
    whi = whi_ref[...]
    logits = _dot_nt(whi, hhi) + _dot_nt(whi, hlo) + _dot_nt(wlo_ref[...], hhi)
    s = jax.nn.sigmoid(logits)
    sb = s + b_ref[...]
    tm = s.shape[1]
    gsz = N_EXPERTS // N_EXPERT_GROUPS
    e_io = lax.broadcasted_iota(I32, (gsz, tm), 0)
    scores = []
    for g in range(N_EXPERT_GROUPS):
        vals = sb[g * gsz:(g + 1) * gsz]
        m1, first = _first_max(vals, e_io, gsz)
        m2 = jnp.max(jnp.where(e_io == first, -jnp.inf, vals), 0, keepdims=True)
        scores.append(m1 + m2)
    gs = jnp.concatenate(scores, axis=0)
    g_io = lax.broadcasted_iota(I32, (N_EXPERT_GROUPS, tm), 0)
    x_io = lax.broadcasted_iota(I32, (N_EXPERTS, tm), 0)
    allowed = jnp.zeros((N_EXPERTS, tm), jnp.bool_)
    for _ in range(TOPK_GROUPS):
        _, first = _first_max(gs, g_io, N_EXPERT_GROUPS)
        gs = jnp.where(g_io == first, -jnp.inf, gs)
        allowed = allowed | (x_io // gsz == first)
    work = jnp.where(allowed, sb, NEG)
    idxs, ws = [], []
    for _ in range(TOP_K):
        _, first = _first_max(work, x_io, N_EXPERTS)
        hit = x_io == first
        idxs.append(first)
        ws.append(jnp.sum(jnp.where(hit, s, 0.0), 0, keepdims=True))
        work = jnp.where(hit, -jnp.inf, work)
    wsel = jnp.concatenate(ws, axis=0)
    idx_ref[...] = jnp.concatenate(idxs, axis=0)
    w_ref[...] = wsel / jnp.sum(wsel, 0, keepdims=True) * ROUTE_SCALE


def _router(h1, whi, wlo, rb, tm):
    n = h1.shape[0]
    return pl.pallas_call(
        _router_kernel, grid=(n // tm,),
        in_specs=[pl.BlockSpec((tm, D_MODEL), lambda i: (i, 0)),
                  pl.BlockSpec((N_EXPERTS, D_MODEL), lambda i: (0, 0)), pl.BlockSpec((N_EXPERTS, D_MODEL), lambda i: (0, 0)),
                  pl.BlockSpec((N_EXPERTS, 1), lambda i: (0, 0))],
        out_specs=[pl.BlockSpec((TOP_K, tm), lambda i: (0, i)), pl.BlockSpec((TOP_K, tm), lambda i: (0, i))],
        out_shape=[jax.ShapeDtypeStruct((TOP_K, n), I32), jax.ShapeDtypeStruct((TOP_K, n), F32)],
        compiler_params=_cparams(("parallel",)),
    )(h1, whi, wlo, rb)


IDX_SLOTS = 4
ROW_UNROLL = 8


def _experts_kernel(blk_e_ref, nvalid_ref, nused_ref, tok_hbm, dst_hbm, x_hbm, w1_ref, w3_ref, w2_ref, y_hbm,
                    tok_smem, dst_smem, xbuf, ybuf, w1b, w3b, w2b, idx_sem, gat_sem, sct_sem):
    j = pl.program_id(0)
    nused = nused_ref[0]
    rb = EXPERT_BLOCK

    def idx_copies(blk):
        slot = blk % IDX_SLOTS
        return (pltpu.make_async_copy(tok_hbm.at[blk], tok_smem.at[slot], idx_sem.at[0, slot]),
                pltpu.make_async_copy(dst_hbm.at[blk], dst_smem.at[slot], idx_sem.at[1, slot]))

    def start_gather(blk):
        islot = blk % IDX_SLOTS
        xslot = blk % 2

        def body(r, c):
            pltpu.make_async_copy(x_hbm.at[pl.ds(tok_smem[islot, r], 1)], xbuf.at[xslot, pl.ds(r, 1)],
                                  gat_sem.at[xslot]).start()
            return c

        lax.fori_loop(0, rb, body, 0, unroll=ROW_UNROLL)

    def wait_gather(slot):
        pltpu.make_async_copy(xbuf.at[slot], xbuf.at[slot], gat_sem.at[slot]).wait()

    def wait_scatter(slot, count):
        @pl.when(count == rb)
        def _():
            pltpu.make_async_copy(ybuf.at[slot], ybuf.at[slot], sct_sem.at[slot]).wait()

        @pl.when(count < rb)
        def _():
            p = rb // 2
            while p >= 1:
                @pl.when((count & p) != 0)
                def _(p=p):
                    pltpu.make_async_copy(ybuf.at[slot, pl.ds(0, p)], ybuf.at[slot, pl.ds(0, p)], sct_sem.at[slot]).wait()
                p //= 2

    @pl.when((j == 0) & (nused > 0))
    def _():
        for c in idx_copies(0):
            c.start()
        for c in idx_copies(0):
            c.wait()
        start_gather(0)

        @pl.when(nused > 1)
        def _():
            for c in idx_copies(1):
                c.start()

    @pl.when(j < nused)
    def _():
        xslot = j % 2

        @pl.when(j + 2 < nused)
        def _():
            for c in idx_copies(j + 2):
                c.start()

        @pl.when(j + 1 < nused)
        def _():
            for c in idx_copies(j + 1):
                c.wait()
            start_gather(j + 1)

        first_of_expert = (j == 0) | (blk_e_ref[j] != blk_e_ref[jnp.maximum(j - 1, 0)])

        @pl.when(first_of_expert)
        def _():
            w1b[...] = w1_ref[...].astype(BF)
            w3b[...] = w3_ref[...].astype(BF)
            w2b[...] = w2_ref[...].astype(BF)

        wait_gather(xslot)
        x = xbuf[xslot].astype(BF)
        a = _dot(x, w1b[...])
        y = _dot((a * jax.nn.sigmoid(a) * _dot(x, w3b[...])).astype(BF), w2b[...])

        @pl.when(j >= 2)
        def _():
            wait_scatter(xslot, nvalid_ref[jnp.maximum(j - 2, 0)])

        ybuf[xslot] = y
        islot = j % IDX_SLOTS
        nv = nvalid_ref[j]

        def body(r, c):
            pltpu.make_async_copy(ybuf.at[xslot, pl.ds(r, 1)], y_hbm.at[pl.ds(dst_smem[islot, r], 1)],
                                  sct_sem.at[xslot]).start()
            return c

        @pl.when(nv == rb)
        def _():
            lax.fori_loop(0, rb, body, 0, unroll=ROW_UNROLL)

        @pl.when(nv < rb)
        def _():
            lax.fori_loop(0, nv, body, 0)

        @pl.when(j == nused - 1)
        def _():
            wait_scatter(xslot, nv)

            @pl.when(j >= 1)
            def _():
                wait_scatter(1 - xslot, nvalid_ref[jnp.maximum(j - 1, 0)])


def _experts(blk_e, nvalid, nused, row_tok, row_dst, h1, w1, w3, w2, n_rows_out):
    n_blocks = blk_e.shape[0]
    rb = EXPERT_BLOCK
    sq = pl.Squeezed()
    wmap = lambda j, be, nv, nu: (be[j], 0, 0)
    anyspec = pl.BlockSpec(memory_space=pl.ANY)
    grid_spec = pltpu.PrefetchScalarGridSpec(
        num_scalar_prefetch=3, grid=(n_blocks,),
        in_specs=[anyspec, anyspec, anyspec,
                  pl.BlockSpec((sq, D_MODEL, D_EXPERT), wmap), pl.BlockSpec((sq, D_MODEL, D_EXPERT), wmap),
                  pl.BlockSpec((sq, D_EXPERT, D_MODEL), wmap)],
        out_specs=anyspec,
        scratch_shapes=[pltpu.SMEM((IDX_SLOTS, rb), I32), pltpu.SMEM((IDX_SLOTS, rb), I32),
                        pltpu.VMEM((2, rb, D_MODEL), F32), pltpu.VMEM((2, rb, D_MODEL), F32),
                        pltpu.VMEM((D_MODEL, D_EXPERT), BF), pltpu.VMEM((D_MODEL, D_EXPERT), BF),
                        pltpu.VMEM((D_EXPERT, D_MODEL), BF),
                        pltpu.SemaphoreType.DMA((2, IDX_SLOTS)), pltpu.SemaphoreType.DMA((2,)),
                        pltpu.SemaphoreType.DMA((2,))])
    return pl.pallas_call(
        _experts_kernel, grid_spec=grid_spec,
        out_shape=jax.ShapeDtypeStruct((n_rows_out, D_MODEL), F32),
        compiler_params=_cparams(("arbitrary",)),
    )(blk_e, nvalid, nused, row_tok, row_dst, h1, w1, w3, w2)


def _combine_kernel(*refs):
    ya_refs = refs[:TOP_K]
    w_ref, h_ref, s1_ref, s3_ref, s2_ref, g_ref, b_ref, o_ref = refs[TOP_K:]
    h = h_ref[...]
    hb = h.astype(BF)
    a = _dot(hb, s1_ref[...])
    y = ALPHA * h + _dot((a * jax.nn.sigmoid(a) * _dot(hb, s3_ref[...])).astype(BF), s2_ref[...])
    w = w_ref[...]
    for k in range(TOP_K):
        y = y + w[:, k:k + 1] * ya_refs[k][...]
    o_ref[...] = _layer_norm(y, g_ref[...], b_ref[...])


def _combine(ya, w, h1, s1, s3, s2, g, b, tm):
    n = h1.shape[0]
    nt = n // tm
    row = lambda i: (i, 0)
    const = lambda i: (0, 0)
    slot_spec = lambda k: pl.BlockSpec((tm, D_MODEL), lambda i: (k * nt + i, 0))
    return pl.pallas_call(
        _combine_kernel, grid=(nt,),
        in_specs=[slot_spec(k) for k in range(TOP_K)] + [
            pl.BlockSpec((tm, TOP_K), row), pl.BlockSpec((tm, D_MODEL), row),
            pl.BlockSpec(s1.shape, const), pl.BlockSpec(s3.shape, const), pl.BlockSpec(s2.shape, const),
            pl.BlockSpec((1, D_MODEL), const), pl.BlockSpec((1, D_MODEL), const)],
        out_specs=pl.BlockSpec((tm, D_MODEL), row),
        out_shape=jax.ShapeDtypeStruct((n, D_MODEL), F32),
        compiler_params=_cparams(("parallel",)),
    )(*([ya] * TOP_K), w, h1, s1, s3, s2, g, b)


def _overlap_matrix(nc, n_slc):
    cs = np.arange(nc) * CMP_STRIDE
    ce = cs + CMP_LEN - 1
    js = np.arange(n_slc) * SLC_LEN
    je = js + SLC_LEN - 1
    ov = ((cs[:, None] <= je[None, :]) & (ce[:, None] >= js[None, :])).astype(np.float32)
    ov[nc - 1] = 0.0
    return ov


def _dispatch(eidx_t, n):
    a_tot = TOP_K * n
    rb = EXPERT_BLOCK
    flat_e = eidx_t.reshape(a_tot)
    order = jnp.argsort(flat_e).astype(I32)
    sorted_e = flat_e[order]
    counts = jnp.bincount(flat_e, length=N_EXPERTS).astype(I32)
    padded = (counts + rb - 1) // rb * rb
    pad_end = jnp.cumsum(padded)
    pad_start = pad_end - padded
    grp_start = jnp.cumsum(counts) - counts
    dest = pad_start[sorted_e] + jnp.arange(a_tot, dtype=I32) - grp_start[sorted_e]
    n_blocks = -(-a_tot // rb) + N_EXPERTS
    p_rows = n_blocks * rb
    blk_start = jnp.arange(n_blocks, dtype=I32) * rb
    blk_e = jnp.minimum(jnp.searchsorted(pad_end, blk_start, side='right'), N_EXPERTS - 1).astype(I32)
    nvalid = jnp.clip(pad_start[blk_e] + counts[blk_e] - blk_start, 0, rb).astype(I32)
    row_tok = jnp.zeros((p_rows,), I32).at[dest].set(order % n)
    row_dst = jnp.zeros((p_rows,), I32).at[dest].set(order)
    nused = (pad_end[-1] // rb).astype(I32).reshape(1)
    return blk_e, nvalid, nused, row_tok.reshape(n_blocks, rb), row_dst.reshape(n_blocks, rb)


def kernel(x, mem, ln0_g, ln0_b, rel_bias, w_in, cmp_pos_k, cmp_pos_v, cmp_k_w1, cmp_k_w2, cmp_v_w1, cmp_v_w2, mla_q_norm, mla_w_uq, mla_kv_norm, mla_w_ukv, mem_w_kv, w_branch, w_out, ln1_g, ln1_b, router_w, router_b, exp_w1, exp_w3, exp_w2, sh_w1, sh_w3, sh_w2, ln2_g, ln2_b):
    batch, seq, d = x.shape
    n = batch * seq
    l = 0
    row2 = lambda v: v.reshape(1, -1)
    tm = min(256, seq)

    pts = np.cumsum((0,) + IN_SPLITS)
    wcol = lambda k: w_in[l][:, pts[k]:pts[k + 1]]
    pad_cols = lambda w, lo, tot: jnp.pad(w, ((0, 0), (lo, tot - lo - w.shape[1])))
    ws = [wcol(0) * (NSA_DH ** -0.5), wcol(1), pad_cols(wcol(2), 0, LANE), wcol(3), wcol(4),
          pad_cols(wcol(5), ROPE_LANE0, LANE), wcol(6), wcol(7)]
    ws = [w.astype(BF) for w in ws]
    hd = MLA_NOPE + MLA_ROPE
    wuq = jnp.pad(mla_w_uq[l].reshape(MLA_Q_LORA, MLA_HEADS, hd), ((0, 0), (0, 0), (0, HEAD_PAD - hd)))
    wuq = wuq.reshape(MLA_Q_LORA, MLA_HEADS * HEAD_PAD).astype(BF)
    wukv = mla_w_ukv[l].reshape(MLA_KV_LORA, MLA_HEADS, MLA_NOPE + MLA_DV)
    wuk = jnp.pad(wukv[:, :, :MLA_NOPE], ((0, 0), (0, 0), (0, HEAD_PAD - MLA_NOPE)))
    wuk = wuk.reshape(MLA_KV_LORA, MLA_HEADS * HEAD_PAD).astype(BF)
    wuv = wukv[:, :, MLA_NOPE:].reshape(MLA_KV_LORA, MLA_HEADS * MLA_DV).astype(BF)

    h, qn, kvn, gn, cq, ckv, kr, qm, gm = _ln_inproj(x.reshape(n, d), row2(ln0_g), row2(ln0_b), ws, tm)

    q_mla, k_mla, v_mla = _mla_prep(cq, ckv, kr, row2(mla_q_norm[l]), row2(mla_kv_norm[l]), wuq, wuk, wuv, seq, tm)
    o_mla = _mla_attn(q_mla, k_mla, v_mla, batch, seq, min(512, seq))

    nc = seq // CMP_STRIDE
    n_slc = seq // SLC_LEN
    gw = NSA_GROUPS * NSA_DH
    cc = kvn[:, :2 * gw].reshape(batch, nc, CMP_STRIDE, 2 * NSA_GROUPS, NSA_DH)
    cc = cc.transpose(0, 3, 1, 2, 4).reshape(batch, 2 * NSA_GROUPS, nc, CMP_STRIDE * NSA_DH)
    pos = jnp.stack([cmp_pos_k[l], cmp_pos_v[l]]).reshape(2, 1, CMP_LEN * NSA_DH)
    w1c = jnp.stack([cmp_k_w1[l], cmp_v_w1[l]]).astype(BF)
    w2c = jnp.stack([cmp_k_w2[l], cmp_v_w2[l]]).astype(BF)
    kvc = _nsa_compress(cc, pos, w1c, w2c)
    overlap = jnp.asarray(_overlap_matrix(nc, n_slc), BF)
    o_cmp, selm = _nsa_cmp(rel_bias, qn, kvc, overlap, batch, seq)
    o_nsa = _nsa_attn(rel_bias, qn, kvn, selm, gn, o_cmp, batch, seq)

    o_mem = _mem_attn(qm, mem.reshape(-1, d), mem_w_kv[l].astype(BF), batch, seq, min(512, seq))

    h1 = _merge(o_nsa, o_mla, o_mem, gm, h, w_branch[l].astype(BF), w_out[l].astype(BF),
                row2(ln1_g[l]), row2(ln1_b[l]), tm)

    rw_t = router_w[l].T
    rw_hi = rw_t.astype(BF)
    rw_lo = (rw_t - rw_hi.astype(F32)).astype(BF)
    eidx_t, w_t = _router(h1, rw_hi, rw_lo, router_b[l].reshape(N_EXPERTS, 1), tm)
    blk_e, nvalid, nused, row_tok, row_dst = _dispatch(eidx_t, n)
    ya = _experts(blk_e, nvalid, nused, row_tok, row_dst, h1, exp_w1[l], exp_w3[l], exp_w2[l], TOP_K * n)
    out = _combine(ya, w_t.T, h1, sh_w1[l].astype(BF), sh_w3[l].astype(BF), sh_w2[l].astype(BF),
                   row2(ln2_g[l]), row2(ln2_b[l]), min(128, seq))
    return out.reshape(batch, seq, d)
```

```python
import functools
import math

import numpy as np
import jax
import jax.numpy as jnp
from jax import lax
from jax.experimental import pallas as pl
from jax.experimental.pallas import tpu as pltpu

BF = jnp.bfloat16
F32 = jnp.float32
I32 = jnp.int32

D_MODEL = 1024
DEPTH = 1
NSA_HEADS = 8
NSA_GROUPS = 2
NSA_HPG = NSA_HEADS // NSA_GROUPS
NSA_DH = 64
CMP_LEN = 32
CMP_STRIDE = 16
CMP_HID = 256
SLC_LEN = 64
SLC_TOPN = 16
WIN = 512
MLA_HEADS = 8
MLA_NOPE = 64
MLA_ROPE = 32
MLA_DV = 64
MLA_Q_LORA = 768
MLA_KV_LORA = 256
ROPE_THETA = 10000.0
MEM_HEADS = 4
MEM_DH = 128
N_BRANCH = 3
BRANCH_W = NSA_HEADS * NSA_DH
REL_BUCKETS = 32
REL_MAX_DIST = 128
N_EXPERTS = 256
TOP_K = 8
N_EXPERT_GROUPS = 8
TOPK_GROUPS = 4
D_EXPERT = 256
ROUTE_SCALE = 2.5
EXPERT_BLOCK = 128
LN_EPS = 1e-5
RMS_EPS = 1e-6
NEG = -1e30
BIG = 1e30
ALPHA = (2 * DEPTH) ** 0.25
IN_SPLITS = (NSA_HEADS * NSA_DH, 6 * NSA_GROUPS * NSA_DH, 3 * NSA_HEADS, MLA_Q_LORA, MLA_KV_LORA,
             MLA_ROPE, MEM_HEADS * MEM_DH, N_BRANCH * D_MODEL)

LANE = 128
SUBLANES = 8
HEAD_PAD = 128
ROPE_LANE0 = MLA_NOPE
ROPE_HALF = MLA_ROPE // 2
VMEM_LIMIT = 56 * 1024 * 1024
NSA_TQ = 128
FAR_CHUNK = 512


def _cparams(sem):
    return pltpu.CompilerParams(dimension_semantics=sem, vmem_limit_bytes=VMEM_LIMIT)


def _bucket_starts():
    max_exact = REL_BUCKETS // 2
    d = np.arange(0, 4 * REL_MAX_DIST)
    nf = np.maximum(d, 1).astype(np.float32)
    large = max_exact + (np.log(nf / np.float32(max_exact)) / np.float32(math.log(REL_MAX_DIST / max_exact))
                         * np.float32(REL_BUCKETS - max_exact)).astype(np.int32)
    large = np.minimum(large, REL_BUCKETS - 1)
    bucket = np.where(d < max_exact, d, large)
    return [int(np.argmax(bucket >= b)) for b in range(REL_BUCKETS)]


BUCKET_START = _bucket_starts()
FAR_DIST = BUCKET_START[REL_BUCKETS - 1]


def _rel_bias(dist, tab_ref, head):
    val = jnp.full(dist.shape, tab_ref[REL_BUCKETS - 1, head], F32)
    for b in range(REL_BUCKETS - 2, -1, -1):
        val = jnp.where(dist < BUCKET_START[b + 1], tab_ref[b, head], val)
    return val


def _layer_norm(x, g, b):
    mu = jnp.mean(x, -1, keepdims=True)
    xc = x - mu
    var = jnp.mean(xc * xc, -1, keepdims=True)
    return xc * lax.rsqrt(var + LN_EPS) * g + b


def _dot(a, b):
    return jnp.dot(a, b, preferred_element_type=F32)


def _dot_nt(a, b):
    return lax.dot_general(a, b, (((1,), (1,)), ((), ())), preferred_element_type=F32)


def _ln_inproj_kernel(x_ref, g_ref, b_ref, wq, wkv, wg, wcq, wckv, wkr, wqm, wgm,
                      h_ref, oq, okv, og, ocq, ockv, okr, oqm, ogm):
    h = _layer_norm(x_ref[...], g_ref[...], b_ref[...])
    h_ref[...] = h
    hb = h.astype(BF)
    for w, o in ((wq, oq), (wkv, okv), (wg, og), (wcq, ocq), (wckv, ockv), (wkr, okr), (wgm, ogm)):
        o[...] = _dot(hb, w[...]).astype(o.dtype)
    oqm[...] = (_dot(hb, wqm[...]) * (MEM_DH ** -0.5)).astype(oqm.dtype)


def _ln_inproj(x2, g, b, ws, tm):
    n = x2.shape[0]
    row = lambda i: (i, 0)
    const = lambda i: (0, 0)
    in_specs = [pl.BlockSpec((tm, D_MODEL), row), pl.BlockSpec((1, D_MODEL), const), pl.BlockSpec((1, D_MODEL), const)]
    in_specs += [pl.BlockSpec(w.shape, const) for w in ws]
    out_shape = [jax.ShapeDtypeStruct((n, D_MODEL), F32)]
    out_shape += [jax.ShapeDtypeStruct((n, w.shape[1]), BF) for w in ws]
    out_specs = [pl.BlockSpec((tm, D_MODEL), row)] + [pl.BlockSpec((tm, w.shape[1]), row) for w in ws]
    return pl.pallas_call(
        _ln_inproj_kernel, grid=(n // tm,), in_specs=in_specs, out_specs=out_specs, out_shape=out_shape,
        compiler_params=_cparams(("parallel",)),
    )(x2, g, b, *ws)


def _rope_lanes(x, c, s1, s2):
    return x * c + pltpu.roll(x, LANE - ROPE_HALF, 1) * s1 + pltpu.roll(x, ROPE_HALF, 1) * s2


def _mla_prep_kernel(cq_ref, ckv_ref, kr_ref, qn_ref, kvn_ref, wuq, wuk, wuv,
                     cq_t, s1q_t, s2q_t, ck_t, s1k_t, s2k_t, q_out, k_out, v_out):
    cq = cq_ref[...].astype(F32)
    rq = cq * lax.rsqrt(jnp.mean(cq * cq, -1, keepdims=True) + RMS_EPS) * qn_ref[...]
    q = _dot(rq.astype(BF), wuq[...])
    ckv = ckv_ref[...].astype(F32)
    rkv = (ckv * lax.rsqrt(jnp.mean(ckv * ckv, -1, keepdims=True) + RMS_EPS) * kvn_ref[...]).astype(BF)
    kn = _dot(rkv, wuk[...])
    v_out[...] = _dot(rkv, wuv[...]).astype(v_out.dtype)
    kr = _rope_lanes(kr_ref[...].astype(F32), ck_t[...], s1k_t[...], s2k_t[...])
    cq_c, s1q, s2q = cq_t[...], s1q_t[...], s2q_t[...]
    for h in range(MLA_HEADS):
        sl = slice(h * HEAD_PAD, (h + 1) * HEAD_PAD)
        q_out[:, sl] = _rope_lanes(q[:, sl], cq_c, s1q, s2q).astype(q_out.dtype)
        k_out[:, sl] = (kn[:, sl] + kr).astype(k_out.dtype)


def _rope_tables(seq, scale):
    freq = ROPE_THETA ** (-jnp.arange(ROPE_HALF, dtype=F32) / ROPE_HALF)
    ang = jnp.arange(seq, dtype=F32)[:, None] * freq[None, :]
    cos, sin = jnp.cos(ang) * scale, jnp.sin(ang) * scale
    z = lambda w: jnp.zeros((seq, w), F32)
    tail = HEAD_PAD - ROPE_LANE0 - MLA_ROPE
    c = jnp.concatenate([jnp.full((seq, ROPE_LANE0), scale, F32), cos, cos, z(tail)], 1)
    s1 = jnp.concatenate([z(ROPE_LANE0), -sin, z(ROPE_HALF + tail)], 1)
    s2 = jnp.concatenate([z(ROPE_LANE0 + ROPE_HALF), sin, z(tail)], 1)
    return c, s1, s2


def _mla_prep(cq, ckv, kr, q_norm, kv_norm, wuq, wuk, wuv, seq, tm):
    n = cq.shape[0]
    nt = seq // tm
    row = lambda i: (i, 0)
    const = lambda i: (0, 0)
    pos = lambda i: (i % nt, 0)
    tabs = _rope_tables(seq, (MLA_NOPE + MLA_ROPE) ** -0.5 * math.log2(math.e)) + _rope_tables(seq, 1.0)
    in_specs = [pl.BlockSpec((tm, MLA_Q_LORA), row), pl.BlockSpec((tm, MLA_KV_LORA), row), pl.BlockSpec((tm, LANE), row),
                pl.BlockSpec((1, MLA_Q_LORA), const), pl.BlockSpec((1, MLA_KV_LORA), const),
                pl.BlockSpec(wuq.shape, const), pl.BlockSpec(wuk.shape, const), pl.BlockSpec(wuv.shape, const)]
    in_specs += [pl.BlockSpec((tm, LANE), pos)] * 6
    hq = MLA_HEADS * HEAD_PAD
    hv = MLA_HEADS * MLA_DV
    return pl.pallas_call(
        _mla_prep_kernel, grid=(n // tm,), in_specs=in_specs,
        out_specs=[pl.BlockSpec((tm, hq), row), pl.BlockSpec((tm, hq), row), pl.BlockSpec((tm, hv), row)],
        out_shape=[jax.ShapeDtypeStruct((n, hq), BF), jax.ShapeDtypeStruct((n, hq), BF), jax.ShapeDtypeStruct((n, hv), BF)],
        compiler_params=_cparams(("parallel",)),
    )(cq, ckv, kr, q_norm, kv_norm, wuq, wuk, wuv, *tabs)


def _mla_attn_kernel(q_ref, k_ref, v_ref, o_ref, m_scr, l_scr, acc_scr):
    i = pl.program_id(2)
    t = q_ref.shape[0]
    reps = t // LANE
    m_scr[...] = jnp.full(m_scr.shape, NEG, F32)
    l_scr[...] = jnp.zeros(l_scr.shape, F32)
    acc_scr[...] = jnp.zeros(acc_scr.shape, F32)

    def tile(kstart, diagonal):
        v = v_ref[pl.ds(kstart, t), :]
        for hh in range(2):
            sl = slice(hh * HEAD_PAD, (hh + 1) * HEAD_PAD)
            s = _dot_nt(q_ref[:, sl], k_ref[pl.ds(kstart, t), sl])
            if diagonal:
                row = lax.broadcasted_iota(I32, (t, t), 0)
                col = lax.broadcasted_iota(I32, (t, t), 1)
                s = jnp.where(col <= row, s, NEG)
            m_prev = m_scr[hh]
            m_new = jnp.maximum(m_prev, jnp.max(s, 1, keepdims=True))
            a = jnp.exp2(m_prev - m_new)
            e = jnp.exp2(s - jnp.tile(m_new, (1, reps)))
            l_scr[hh] = a * l_scr[hh] + jnp.sum(e, 1, keepdims=True)
            acc_scr[hh] = a * acc_scr[hh] + _dot(e.astype(BF), v)
            m_scr[hh] = m_new

    def body(c, carry):
        tile(pl.multiple_of(c * t, t), False)
        return carry

    lax.fori_loop(0, i, body, 0)
    tile(pl.multiple_of(i * t, t), True)
    lane = lax.broadcasted_iota(I32, (t, 2 * MLA_DV), 1)
    o = jnp.where(lane < MLA_DV, acc_scr[0] / l_scr[0], acc_scr[1] / l_scr[1])
    o_ref[...] = o.astype(o_ref.dtype)


def _mla_attn(q, k, v, batch, seq, t):
    n = q.shape[0]
    nt = seq // t
    qmap = lambda b, hp, i: (b * nt + i, hp)
    kmap = lambda b, hp, i: (b, hp)
    return pl.pallas_call(
        _mla_attn_kernel, grid=(batch, MLA_HEADS // 2, nt),
        in_specs=[pl.BlockSpec((t, 2 * HEAD_PAD), qmap), pl.BlockSpec((seq, 2 * HEAD_PAD), kmap),
                  pl.BlockSpec((seq, 2 * MLA_DV), kmap)],
        out_specs=pl.BlockSpec((t, 2 * MLA_DV), qmap),
        out_shape=jax.ShapeDtypeStruct((n, MLA_HEADS * MLA_DV), BF),
        scratch_shapes=[pltpu.VMEM((2, t, LANE), F32), pltpu.VMEM((2, t, LANE), F32), pltpu.VMEM((2, t, 2 * MLA_DV), F32)],
        compiler_params=_cparams(("parallel", "parallel", "arbitrary")),
    )(q, k, v)


def _nsa_compress_kernel(c_ref, pos_ref, w1_ref, w2_ref, o_ref):
    nc = c_ref.shape[0]
    half = CMP_STRIDE * NSA_DH
    c = c_ref[...]
    top = _dot(c, w1_ref[:half, :])
    bot = _dot(c, w1_ref[half:, :])
    posb = _dot(jnp.broadcast_to(pos_ref[...], (8, 2 * half)).astype(BF), w1_ref[...])[:1]
    hid = top + pltpu.roll(bot, nc - 1, 0) + posb
    o_ref[...] = _dot(jax.nn.gelu(hid).astype(BF), w2_ref[...]).astype(o_ref.dtype)


def _nsa_compress(cc, pos, w1, w2):
    b, _, nc, half = cc.shape
    sq = pl.Squeezed()
    return pl.pallas_call(
        _nsa_compress_kernel, grid=(b, 2 * NSA_GROUPS),
        in_specs=[pl.BlockSpec((sq, sq, nc, half), lambda i, c: (i, c, 0, 0)),
                  pl.BlockSpec((sq, 1, 2 * half), lambda i, c: (c // NSA_GROUPS, 0, 0)),
                  pl.BlockSpec((sq, 2 * half, CMP_HID), lambda i, c: (c // NSA_GROUPS, 0, 0)),
                  pl.BlockSpec((sq, CMP_HID, NSA_DH), lambda i, c: (c // NSA_GROUPS, 0, 0))],
        out_specs=pl.BlockSpec((sq, sq, nc, NSA_DH), lambda i, c: (i, c, 0, 0)),
        out_shape=jax.ShapeDtypeStruct((b, 2 * NSA_GROUPS, nc, NSA_DH), BF),
        compiler_params=_cparams(("parallel", "parallel")),
    )(cc, pos, w1, w2)


CMP_TQ = 512
CMP_BIAS_COLS = LANE // 2
CMP_BIAS_BACK = -(-(FAR_DIST + CMP_LEN - 1) // CMP_STRIDE)
assert (CMP_TQ - CMP_LEN) // CMP_STRIDE + CMP_BIAS_BACK < CMP_BIAS_COLS


def _nsa_cmp_kernel(tab_ref, q_ref, kc_ref, vc_ref, ov_ref, oc_ref, sel_ref, e_scr, *, n_top):
    b, g, i = pl.program_id(0), pl.program_id(1), pl.program_id(2)
    tq = q_ref.shape[0]
    nc = kc_ref.shape[0]
    n_slc = ov_ref.shape[1]
    qs = i * tq

    @pl.when((b == 0) & (g == 0) & (i == 0))
    def _():
        q_io = lax.broadcasted_iota(I32, (tq, LANE), 0)
        lane = lax.broadcasted_iota(I32, (tq, LANE), 1)
        jj = lane % CMP_BIAS_COLS
        dist = q_io - CMP_STRIDE * (jj - CMP_BIAS_BACK) - (CMP_LEN - 1)
        live = (dist >= 0) & (lane < 2 * CMP_BIAS_COLS)
        for h in range(NSA_HEADS):
            e = jnp.where(live, _rel_bias(dist, tab_ref, h) - tab_ref[REL_BUCKETS - 1, h], 0.0)
            hi = e.astype(BF)
            lo = (e - hi.astype(F32)).astype(BF)
            e_scr[h] = jnp.where(lane < CMP_BIAS_COLS, hi, lo)

    n0 = qs // CMP_STRIDE
    jrow = lax.broadcasted_iota(I32, (LANE, nc), 0)
    ncol = lax.broadcasted_iota(I32, (LANE, nc), 1)
    ft = jnp.where((ncol == n0 + (jrow % CMP_BIAS_COLS) - CMP_BIAS_BACK) & (jrow < 2 * CMP_BIAS_COLS), 1.0, 0.0).astype(BF)

    t = qs + lax.broadcasted_iota(I32, (tq, nc), 0)
    n_io = lax.broadcasted_iota(I32, (tq, nc), 1)
    mask = (t >= n_io * CMP_STRIDE + (CMP_LEN - 1)) & (n_io < nc - 1)
    kc = kc_ref[...]
    vc = vc_ref[...]
    psum = jnp.zeros((tq, nc), F32)
    for h in range(NSA_HPG):
        head = g * NSA_HPG + h
        qh = q_ref[:, h * NSA_DH:(h + 1) * NSA_DH]
        s = _dot_nt(qh, kc) + _dot(e_scr[head], ft) + tab_ref[REL_BUCKETS - 1, head]
        s = jnp.where(mask, s, NEG)
        m = jnp.max(s, 1, keepdims=True)
        e = jnp.where(mask, jnp.exp(s - m), 0.0)
        l = jnp.sum(e, 1, keepdims=True)
        p = e * jnp.where(l > 0.0, 1.0 / l, 0.0)
        oc_ref[:, h * NSA_DH:(h + 1) * NSA_DH] = _dot(p.astype(BF), vc).astype(oc_ref.dtype)
        psum = psum + p

    ov = ov_ref[...]
    p0 = psum.astype(BF)
    r1 = psum - p0.astype(F32)
    p1 = r1.astype(BF)
    p2 = (r1 - p1.astype(F32)).astype(BF)
    imp = _dot(p0, ov) + _dot(p1, ov) + _dot(p2, ov)

    tj = qs + lax.broadcasted_iota(I32, (tq, n_slc), 0)
    j = lax.broadcasted_iota(I32, (tq, n_slc), 1)
    cur = tj // SLC_LEN
    forced = (j == 0) | (j == cur) | (j == cur - 1)
    work = jnp.where(j * SLC_LEN > tj, NEG, jnp.where(forced, BIG, imp))
    sel = jnp.zeros((tq, n_slc), F32)
    for _ in range(n_top):
        mx = jnp.max(work, 1, keepdims=True)
        first = jnp.min(jnp.where(work == mx, j, n_slc), 1, keepdims=True)
        hit = j == first
        sel = jnp.where(hit & (mx > 0.5 * NEG), 1.0, sel)
        work = jnp.where(hit, -jnp.inf, work)
    sel_ref[...] = sel.astype(sel_ref.dtype)


def _nsa_cmp(tab, qn, kvc, overlap, batch, seq):
    n = qn.shape[0]
    tq = min(CMP_TQ, seq)
    nq = seq // tq
    nc = kvc.shape[2]
    n_slc = overlap.shape[1]
    sq = pl.Squeezed()
    gw = NSA_HPG * NSA_DH
    return pl.pallas_call(
        functools.partial(_nsa_cmp_kernel, n_top=min(SLC_TOPN, n_slc)), grid=(batch, NSA_GROUPS, nq),
        in_specs=[pl.BlockSpec(memory_space=pltpu.SMEM),
                  pl.BlockSpec((tq, gw), lambda b, g, i: (b * nq + i, g)),
                  pl.BlockSpec((sq, sq, nc, NSA_DH), lambda b, g, i: (b, g, 0, 0)),
                  pl.BlockSpec((sq, sq, nc, NSA_DH), lambda b, g, i: (b, NSA_GROUPS + g, 0, 0)),
                  pl.BlockSpec((nc, n_slc), lambda b, g, i: (0, 0))],
        out_specs=[pl.BlockSpec((tq, gw), lambda b, g, i: (b * nq + i, g)),
                   pl.BlockSpec((sq, tq, n_slc), lambda b, g, i: (g, b * nq + i, 0))],
        out_shape=[jax.ShapeDtypeStruct((n, NSA_HEADS * NSA_DH), BF),
                   jax.ShapeDtypeStruct((NSA_GROUPS, n, n_slc), BF)],
        scratch_shapes=[pltpu.VMEM((NSA_HEADS, tq, LANE), BF)],
        compiler_params=_cparams(("arbitrary", "arbitrary", "arbitrary")),
    )(tab, qn, kvc, kvc, overlap)


def _nsa_attn_kernel(tab_ref, q_ref, ks_ref, vs_ref, kw_ref, vw_ref, sel_ref, gate_ref, oc_ref, o_ref,
                     d_scr, m_scr, l_scr, acc_scr):
    b, i = pl.program_id(0), pl.program_id(1)
    tq = NSA_TQ
    rows = NSA_HPG * tq
    n_slc = sel_ref.shape[2]
    qs = i * tq
    near_w = 2 * tq
    far_w = FAR_CHUNK
    win_far_w = WIN - tq

    @pl.when((b == 0) & (i == 0))
    def _():
        q_io = lax.broadcasted_iota(I32, (tq, near_w), 0)
        kk = lax.broadcasted_iota(I32, (tq, near_w), 1)
        dist = jnp.maximum(q_io + tq - kk, 0)
        for h in range(NSA_HEADS):
            d_scr[h] = _rel_bias(dist, tab_ref, h)

    q_io = lax.broadcasted_iota(I32, (tq, near_w), 0)
    kk = lax.broadcasted_iota(I32, (tq, near_w), 1)
    kpos_near = qs - tq + kk
    near_mask = (kpos_near >= 0) & (kpos_near <= qs + q_io)
    start_a = pl.multiple_of(qs, tq)
    start_b = pl.multiple_of(jnp.maximum(qs - tq, 0), tq)
    j_near = lax.broadcasted_iota(I32, (n_slc, near_w), 0)
    k_near = lax.broadcasted_iota(I32, (n_slc, near_w), 1)
    ex_near = jnp.where(j_near == 2 * i - 2 + k_near // SLC_LEN, 1.0, 0.0).astype(BF)

    def q4(g):
        return jnp.concatenate([q_ref[:, (g * NSA_HPG + h) * NSA_DH:(g * NSA_HPG + h + 1) * NSA_DH]
                                for h in range(NSA_HPG)], axis=0)

    def update(slot, s, mask, v):
        w = s.shape[-1]
        s = jnp.where(mask[None], s, NEG).reshape(rows, w)
        m_prev = m_scr[slot]
        m_new = jnp.maximum(m_prev, jnp.max(s, 1, keepdims=True))
        a = jnp.exp(m_prev - m_new)
        e = jnp.exp(s - m_new)
        l_scr[slot] = a * l_scr[slot] + jnp.sum(e, 1, keepdims=True)
        acc_scr[slot] = a * acc_scr[slot] + _dot(e.astype(BF), v)
        m_scr[slot] = m_new

    def far_logits(g, k):
        s = _dot_nt(q4(g), k)
        s = jnp.concatenate([s[h * tq:(h + 1) * tq] + tab_ref[REL_BUCKETS - 1, g * NSA_HPG + h]
                             for h in range(NSA_HPG)], axis=0)
        return s.reshape(NSA_HPG, tq, k.shape[0])

    m_scr[...] = jnp.full(m_scr.shape, NEG, F32)
    l_scr[...] = jnp.zeros(l_scr.shape, F32)
    acc_scr[...] = jnp.zeros(acc_scr.shape, F32)

    for g in range(NSA_GROUPS):
        gl = slice(g * NSA_DH, (g + 1) * NSA_DH)
        qg = q4(g)
        bias = d_scr[g * NSA_HPG:(g + 1) * NSA_HPG]
        sel_near = _dot(sel_ref[g], ex_near) > 0.5
        for br, (k_ref, v_ref) in enumerate(((ks_ref, vs_ref), (kw_ref, vw_ref))):
            k = jnp.concatenate([k_ref[pl.ds(start_b, tq), gl], k_ref[pl.ds(start_a, tq), gl]], axis=0)
            v = jnp.concatenate([v_ref[pl.ds(start_b, tq), gl], v_ref[pl.ds(start_a, tq), gl]], axis=0)
            s = _dot_nt(qg, k).reshape(NSA_HPG, tq, near_w) + bias
            mask = (near_mask & sel_near) if br == 0 else near_mask
            update(2 * g + br, s, mask, v)

    ws = pl.multiple_of(jnp.maximum(qs - WIN, 0), tq)
    kpos_w = ws + lax.broadcasted_iota(I32, (tq, win_far_w), 1)
    t_w = qs + lax.broadcasted_iota(I32, (tq, win_far_w), 0)
    mask_w = (kpos_w < qs - tq) & (kpos_w > t_w - WIN)
    for g in range(NSA_GROUPS):
        gl = slice(g * NSA_DH, (g + 1) * NSA_DH)
        update(2 * g + 1, far_logits(g, kw_ref[pl.ds(ws, win_far_w), gl]), mask_w, vw_ref[pl.ds(ws, win_far_w), gl])

    n_far = (jnp.maximum(i - 1, 0) + (far_w // tq - 1)) // (far_w // tq)
    j_far = lax.broadcasted_iota(I32, (n_slc, far_w), 0)
    k_far = lax.broadcasted_iota(I32, (n_slc, far_w), 1) // SLC_LEN
    kcol = lax.broadcasted_iota(I32, (tq, far_w), 1)

    def far_body(c, carry):
        base = pl.multiple_of(c * far_w, far_w)
        ex = jnp.where(j_far - c * (far_w // SLC_LEN) == k_far, 1.0, 0.0).astype(BF)
        in_range = base + kcol < qs - tq
        for g in range(NSA_GROUPS):
            gl = slice(g * NSA_DH, (g + 1) * NSA_DH)
            mask = (_dot(sel_ref[g], ex) > 0.5) & in_range
            update(2 * g, far_logits(g, ks_ref[pl.ds(base, far_w), gl]), mask, vs_ref[pl.ds(base, far_w), gl])
        return carry

    lax.fori_loop(0, n_far, far_body, 0)

    gates = jax.nn.sigmoid(gate_ref[...].astype(F32))
    for g in range(NSA_GROUPS):
        o_s = acc_scr[2 * g] / l_scr[2 * g]
        o_w = acc_scr[2 * g + 1] / l_scr[2 * g + 1]
        for h in range(NSA_HPG):
            head = g * NSA_HPG + h
            hl = slice(head * NSA_DH, (head + 1) * NSA_DH)
            hr = slice(h * tq, (h + 1) * tq)
            o = (gates[:, 3 * head:3 * head + 1] * oc_ref[:, hl].astype(F32)
                 + gates[:, 3 * head + 1:3 * head + 2] * o_s[hr]
                 + gates[:, 3 * head + 2:3 * head + 3] * o_w[hr])
            o_ref[:, hl] = o.astype(o_ref.dtype)


def _nsa_attn(tab, qn, kvn, selm, gn, oc, batch, seq):
    n = qn.shape[0]
    tq = NSA_TQ
    nq = seq // tq
    n_slc = selm.shape[2]
    hw = NSA_HEADS * NSA_DH
    gw = NSA_GROUPS * NSA_DH
    row = lambda b, i: (b * nq + i, 0)
    kv = lambda kind: pl.BlockSpec((seq, gw), lambda b, i: (b, kind))
    rows = NSA_HPG * tq
    return pl.pallas_call(
        _nsa_attn_kernel, grid=(batch, nq),
        in_specs=[pl.BlockSpec(memory_space=pltpu.SMEM), pl.BlockSpec((tq, hw), row),
                  kv(2), kv(3), kv(4), kv(5),
                  pl.BlockSpec((NSA_GROUPS, tq, n_slc), lambda b, i: (0, b * nq + i, 0)),
                  pl.BlockSpec((tq, LANE), row), pl.BlockSpec((tq, hw), row)],
        out_specs=pl.BlockSpec((tq, hw), row),
        out_shape=jax.ShapeDtypeStruct((n, hw), BF),
        scratch_shapes=[pltpu.VMEM((NSA_HEADS, tq, 2 * tq), F32),
                        pltpu.VMEM((2 * NSA_GROUPS, rows, 1), F32), pltpu.VMEM((2 * NSA_GROUPS, rows, 1), F32),
                        pltpu.VMEM((2 * NSA_GROUPS, rows, NSA_DH), F32)],
        compiler_params=_cparams(("arbitrary", "arbitrary")),
    )(tab, qn, kvn, kvn, kvn, kvn, selm, gn, oc)


def _mem_attn_kernel(q_ref, mem_ref, w_ref, o_ref, kv_scr):
    @pl.when(pl.program_id(1) == 0)
    def _():
        kv_scr[...] = _dot(mem_ref[...].astype(BF), w_ref[...]).astype(BF)

    hw = MEM_HEADS * MEM_DH
    for h in range(MEM_HEADS):
        sl = slice(h * MEM_DH, (h + 1) * MEM_DH)
        s = _dot_nt(q_ref[:, sl], kv_scr[:, sl])
        e = jnp.exp(s - jnp.max(s, 1, keepdims=True))
        p = e / jnp.sum(e, 1, keepdims=True)
        o_ref[:, sl] = _dot(p.astype(BF), kv_scr[:, hw + h * MEM_DH:hw + (h + 1) * MEM_DH]).astype(o_ref.dtype)


def _mem_attn(qm, mem2, w_kv, batch, seq, tq):
    n = qm.shape[0]
    nq = seq // tq
    m = mem2.shape[0] // batch
    hw = MEM_HEADS * MEM_DH
    return pl.pallas_call(
        _mem_attn_kernel, grid=(batch, nq),
        in_specs=[pl.BlockSpec((tq, hw), lambda b, i: (b * nq + i, 0)),
                  pl.BlockSpec((m, D_MODEL), lambda b, i: (b, 0)),
                  pl.BlockSpec((D_MODEL, 2 * hw), lambda b, i: (0, 0))],
        out_specs=pl.BlockSpec((tq, hw), lambda b, i: (b * nq + i, 0)),
        out_shape=jax.ShapeDtypeStruct((n, hw), BF),
        scratch_shapes=[pltpu.VMEM((m, 2 * hw), BF)],
        compiler_params=_cparams(("arbitrary", "arbitrary")),
    )(qm, mem2, w_kv)


def _merge_kernel(on_ref, ol_ref, om_ref, gm_ref, h_ref, wb_ref, wo_ref, g_ref, b_ref, h1_ref):
    merged = None
    for c, o_ref in enumerate((on_ref, ol_ref, om_ref)):
        gate = jax.nn.sigmoid(gm_ref[:, c * D_MODEL:(c + 1) * D_MODEL].astype(F32))
        term = gate * _dot(o_ref[...], wb_ref[c])
        merged = term if merged is None else merged + term
    y = ALPHA * h_ref[...] + _dot(merged.astype(BF), wo_ref[...])
    h1_ref[...] = _layer_norm(y, g_ref[...], b_ref[...])


def _merge(o_nsa, o_mla, o_mem, gm, h, wb, wo, g, b, tm):
    n = h.shape[0]
    row = lambda i: (i, 0)
    const = lambda i: (0, 0)
    return pl.pallas_call(
        _merge_kernel, grid=(n // tm,),
        in_specs=[pl.BlockSpec((tm, BRANCH_W), row)] * 3 + [
            pl.BlockSpec((tm, N_BRANCH * D_MODEL), row), pl.BlockSpec((tm, D_MODEL), row),
            pl.BlockSpec(wb.shape, lambda i: (0, 0, 0)), pl.BlockSpec(wo.shape, const),
            pl.BlockSpec((1, D_MODEL), const), pl.BlockSpec((1, D_MODEL), const)],
        out_specs=pl.BlockSpec((tm, D_MODEL), row),
        out_shape=jax.ShapeDtypeStruct((n, D_MODEL), F32),
        compiler_params=_cparams(("parallel",)),
    )(o_nsa, o_mla, o_mem, gm, h, wb, wo, g, b)


def _first_max(vals, idx, limit):
    mx = jnp.max(vals, 0, keepdims=True)
    first = jnp.min(jnp.where(vals == mx, idx, limit), 0, keepdims=True)
    return mx, first


def _router_kernel(h_ref, whi_ref, wlo_ref, b_ref, tri_ref, idx_ref, w_ref, rank_ref, cnt_ref):
    @pl.when(pl.program_id(0) == 0)
    def _():
        cnt_ref[...] = jnp.zeros(cnt_ref.shape, F32)

    h = h_ref[...]
    hhi = h.astype(BF)
    hlo = (h - hhi.astype(F32)).astype(BF)
    whi = whi_ref[...]
    logits = _dot_nt(whi, hhi) + _dot_nt(whi, hlo) + _dot_nt(wlo_ref[...], hhi)
    s = jax.nn.sigmoid(logits)
    sb = s + b_ref[...]
    tm = s.shape[1]
    gsz = N_EXPERTS // N_EXPERT_GROUPS
    e_io = lax.broadcasted_iota(I32, (gsz, tm), 0)
    scores = []
    for g in range(N_EXPERT_GROUPS):
        vals = sb[g * gsz:(g + 1) * gsz]
        m1, first = _first_max(vals, e_io, gsz)
        m2 = jnp.max(jnp.where(e_io == first, -jnp.inf, vals), 0, keepdims=True)
        scores.append(m1 + m2)
    gs = jnp.concatenate(scores, axis=0)
    g_io = lax.broadcasted_iota(I32, (N_EXPERT_GROUPS, tm), 0)
    x_io = lax.broadcasted_iota(I32, (N_EXPERTS, tm), 0)
    allowed = jnp.zeros((N_EXPERTS, tm), jnp.bool_)
    for _ in range(TOPK_GROUPS):
        _, first = _first_max(gs, g_io, N_EXPERT_GROUPS)
        gs = jnp.where(g_io == first, -jnp.inf, gs)
        allowed = allowed | (x_io // gsz == first)
    work = jnp.where(allowed, sb, NEG)
    base = cnt_ref[:, :1]
    tri = tri_ref[...]
    idxs, ws, ranks = [], [], []
    for _ in range(TOP_K):
        _, first = _first_max(work, x_io, N_EXPERTS)
        hit = x_io == first
        idxs.append(first)
        ws.append(jnp.sum(jnp.where(hit, s, 0.0), 0, keepdims=True))
        work = jnp.where(hit, -jnp.inf, work)
        onehot = jnp.where(hit, 1.0, 0.0)
        before = _dot(onehot.astype(BF), tri)
        ranks.append(jnp.sum(jnp.where(hit, base + before, 0.0), 0, keepdims=True))
        base = base + jnp.sum(onehot, 1, keepdims=True)
    wsel = jnp.concatenate(ws, axis=0)
    idx_ref[...] = jnp.concatenate(idxs, axis=0)
    w_ref[...] = wsel / jnp.sum(wsel, 0, keepdims=True) * ROUTE_SCALE
    rank_ref[...] = jnp.concatenate(ranks, axis=0).astype(I32)
    cnt_ref[...] = jnp.broadcast_to(base, cnt_ref.shape)


def _router(h1, whi, wlo, rb, tm):
    n = h1.shape[0]
    tri = jnp.asarray(np.triu(np.ones((tm, tm), np.float32), 1), BF)
    slot = pl.BlockSpec((TOP_K, tm), lambda i: (0, i))
    const = lambda i: (0, 0)
    return pl.pallas_call(
        _router_kernel, grid=(n // tm,),
        in_specs=[pl.BlockSpec((tm, D_MODEL), lambda i: (i, 0)),
                  pl.BlockSpec((N_EXPERTS, D_MODEL), const), pl.BlockSpec((N_EXPERTS, D_MODEL), const),
                  pl.BlockSpec((N_EXPERTS, 1), const), pl.BlockSpec((tm, tm), const)],
        out_specs=[slot, slot, slot, pl.BlockSpec((N_EXPERTS, LANE), const)],
        out_shape=[jax.ShapeDtypeStruct((TOP_K, n), I32), jax.ShapeDtypeStruct((TOP_K, n), F32),
                   jax.ShapeDtypeStruct((TOP_K, n), I32), jax.ShapeDtypeStruct((N_EXPERTS, LANE), F32)],
        compiler_params=_cparams(("arbitrary",)),
    )(h1, whi, wlo, rb, tri)


def _pos_kernel(idx_ref, rank_ref, start_ref, pos_ref):
    tm = idx_ref.shape[1]
    x_io = lax.broadcasted_iota(I32, (N_EXPERTS, tm), 0)
    start = start_ref[...]
    rows = [jnp.sum(jnp.where(x_io == idx_ref[k:k + 1, :], start, 0.0), 0, keepdims=True) for k in range(TOP_K)]
    pos_ref[...] = jnp.concatenate(rows, axis=0).astype(I32) + rank_ref[...]


def _positions(eidx_t, rank_t, pad_start, tm):
    n = eidx_t.shape[1]
    slot = pl.BlockSpec((TOP_K, tm), lambda i: (0, i))
    return pl.pallas_call(
        _pos_kernel, grid=(n // tm,),
        in_specs=[slot, slot, pl.BlockSpec((N_EXPERTS, 1), lambda i: (0, 0))],
        out_specs=slot, out_shape=jax.ShapeDtypeStruct((TOP_K, n), I32),
        compiler_params=_cparams(("parallel",)),
    )(eidx_t, rank_t, pad_start)


ROW_UNROLL = 4


def _fill_groups():
    p = EXPERT_BLOCK // 2
    while p >= 1:
        yield p
        p //= 2


def _permute_kernel(fill_start_ref, fill_n_ref, pos_ref, h_ref, xs_hbm, zbuf, row_sem, fill_sem, *, fills_per_step):
    i = pl.program_id(0)
    tm = h_ref.shape[0]
    n_fills = fill_n_ref.shape[0]
    zbuf[...] = jnp.zeros(zbuf.shape, zbuf.dtype)

    def fill_copies(e):
        n = fill_n_ref[e]
        start = fill_start_ref[e]
        for p in _fill_groups():
            @pl.when((n & p) != 0)
            def _(p=p):
                off = start + (n & (p - 1))
                if p < SUBLANES:
                    for r in range(p):
                        pltpu.make_async_copy(zbuf.at[pl.ds(r, 1)], xs_hbm.at[pl.ds(off + r, 1)], fill_sem).start()
                else:
                    pltpu.make_async_copy(zbuf.at[pl.ds(0, p)], xs_hbm.at[pl.ds(pl.multiple_of(off, SUBLANES), p)],
                                          fill_sem).start()

    def fill_waits(e):
        n = fill_n_ref[e]
        for p in _fill_groups():
            @pl.when((n & p) != 0)
            def _(p=p):
                pltpu.make_async_copy(zbuf.at[pl.ds(0, p)], zbuf.at[pl.ds(0, p)], fill_sem).wait()

    for q in range(fills_per_step):
        e = i * fills_per_step + q

        @pl.when(e < n_fills)
        def _(e=e):
            fill_copies(e)

    def body(t, c):
        for k in range(TOP_K):
            pltpu.make_async_copy(h_ref.at[pl.ds(t, 1)], xs_hbm.at[pl.ds(pos_ref[k, t], 1)], row_sem).start()
        return c

    lax.fori_loop(0, tm, body, 0, unroll=ROW_UNROLL)

    for q in range(fills_per_step):
        e = i * fills_per_step + q

        @pl.when(e < n_fills)
        def _(e=e):
            fill_waits(e)

    pltpu.make_async_copy(xs_hbm.at[pl.ds(0, TOP_K * tm)], xs_hbm.at[pl.ds(0, TOP_K * tm)], row_sem).wait()


def _permute(fill_start, fill_n, pos, h1, n_rows, tm):
    n = h1.shape[0]
    nt = n // tm
    grid_spec = pltpu.PrefetchScalarGridSpec(
        num_scalar_prefetch=2, grid=(nt,),
        in_specs=[pl.BlockSpec((TOP_K, tm), lambda i, fs, fn: (0, i), memory_space=pltpu.SMEM),
                  pl.BlockSpec((tm, D_MODEL), lambda i, fs, fn: (i, 0))],
        out_specs=pl.BlockSpec(memory_space=pl.ANY),
        scratch_shapes=[pltpu.VMEM((EXPERT_BLOCK // 2, D_MODEL), F32), pltpu.SemaphoreType.DMA(()),
                        pltpu.SemaphoreType.DMA(())])
    return pl.pallas_call(
        functools.partial(_permute_kernel, fills_per_step=-(-fill_n.shape[0] // nt)), grid_spec=grid_spec,
        out_shape=jax.ShapeDtypeStruct((n_rows, D_MODEL), F32),
        compiler_params=_cparams(("arbitrary",)),
    )(fill_start, fill_n, pos, h1)


def _experts_kernel(blk_e_ref, nused_ref, x_ref, w1_ref, w3_ref, w2_ref, y_ref, w1b, w3b, w2b):
    j = pl.program_id(0)

    @pl.when(j < nused_ref[0])
    def _():
        first_of_expert = (j == 0) | (blk_e_ref[j] != blk_e_ref[jnp.maximum(j - 1, 0)])

        @pl.when(first_of_expert)
        def _():
            w1b[...] = w1_ref[...].astype(BF)
            w3b[...] = w3_ref[...].astype(BF)
            w2b[...] = w2_ref[...].astype(BF)

        x = x_ref[...].astype(BF)
        a = _dot(x, w1b[...])
        y_ref[...] = _dot((a * jax.nn.sigmoid(a) * _dot(x, w3b[...])).astype(BF), w2b[...])

    @pl.when(j >= nused_ref[0])
    def _():
        y_ref[...] = jnp.zeros(y_ref.shape, y_ref.dtype)


def _experts(blk_e, nused, xs, w1, w3, w2):
    n_blocks = blk_e.shape[0]
    rb = EXPERT_BLOCK
    sq = pl.Squeezed()
    wmap = lambda j, be, nu: (be[j], 0, 0)
    grid_spec = pltpu.PrefetchScalarGridSpec(
        num_scalar_prefetch=2, grid=(n_blocks,),
        in_specs=[pl.BlockSpec((rb, D_MODEL), lambda j, be, nu: (jnp.minimum(j, nu[0] - 1), 0)),
                  pl.BlockSpec((sq, D_MODEL, D_EXPERT), wmap), pl.BlockSpec((sq, D_MODEL, D_EXPERT), wmap),
                  pl.BlockSpec((sq, D_EXPERT, D_MODEL), wmap)],
        out_specs=pl.BlockSpec((rb, D_MODEL), lambda j, be, nu: (j, 0)),
        scratch_shapes=[pltpu.VMEM((D_MODEL, D_EXPERT), BF), pltpu.VMEM((D_MODEL, D_EXPERT), BF),
                        pltpu.VMEM((D_EXPERT, D_MODEL), BF)])
    return pl.pallas_call(
        _experts_kernel, grid_spec=grid_spec,
        out_shape=jax.ShapeDtypeStruct((n_blocks * rb, D_MODEL), F32),
        compiler_params=_cparams(("arbitrary",)),
    )(blk_e, nused, xs, w1, w3, w2)


def _combine_kernel(pos_ref, pos_next_ref, ys_hbm, w_ref, h_ref, s1_ref, s3_ref, s2_ref, g_ref, b_ref, o_ref, ybuf, sem):
    i = pl.program_id(0)
    nt = pl.num_programs(0)
    tm = h_ref.shape[0]

    def start_gather(p_ref, slot):
        def body(t, c):
            for k in range(TOP_K):
                pltpu.make_async_copy(ys_hbm.at[pl.ds(p_ref[k, t], 1)], ybuf.at[slot, k, pl.ds(t, 1)], sem.at[slot]).start()
            return c

        lax.fori_loop(0, tm, body, 0, unroll=ROW_UNROLL)

    @pl.when(i == 0)
    def _():
        start_gather(pos_ref, 0)

    slot = i % 2

    @pl.when(i + 1 < nt)
    def _():
        start_gather(pos_next_ref, 1 - slot)

    h = h_ref[...]
    hb = h.astype(BF)
    a = _dot(hb, s1_ref[...])
    y = ALPHA * h + _dot((a * jax.nn.sigmoid(a) * _dot(hb, s3_ref[...])).astype(BF), s2_ref[...])
    pltpu.make_async_copy(ybuf.at[slot], ybuf.at[slot], sem.at[slot]).wait()
    w = w_ref[...]
    for k in range(TOP_K):
        y = y + w[:, k:k + 1] * ybuf[slot, k]
    o_ref[...] = _layer_norm(y, g_ref[...], b_ref[...])


def _combine(pos, ys, w, h1, s1, s3, s2, g, b, tm):
    n = h1.shape[0]
    nt = n // tm
    row = lambda i: (i, 0)
    const = lambda i: (0, 0)
    return pl.pallas_call(
        _combine_kernel, grid=(nt,),
        in_specs=[pl.BlockSpec((TOP_K, tm), lambda i: (0, i), memory_space=pltpu.SMEM),
                  pl.BlockSpec((TOP_K, tm), lambda i: (0, jnp.minimum(i + 1, nt - 1)), memory_space=pltpu.SMEM),
                  pl.BlockSpec(memory_space=pl.ANY),
                  pl.BlockSpec((tm, TOP_K), row), pl.BlockSpec((tm, D_MODEL), row),
                  pl.BlockSpec(s1.shape, const), pl.BlockSpec(s3.shape, const), pl.BlockSpec(s2.shape, const),
                  pl.BlockSpec((1, D_MODEL), const), pl.BlockSpec((1, D_MODEL), const)],
        out_specs=pl.BlockSpec((tm, D_MODEL), row),
        out_shape=jax.ShapeDtypeStruct((n, D_MODEL), F32),
        scratch_shapes=[pltpu.VMEM((2, TOP_K, tm, D_MODEL), F32), pltpu.SemaphoreType.DMA((2,))],
        compiler_params=_cparams(("arbitrary",)),
    )(pos, pos, ys, w, h1, s1, s3, s2, g, b)


def _overlap_matrix(nc, n_slc):
    cs = np.arange(nc) * CMP_STRIDE
    ce = cs + CMP_LEN - 1
    js = np.arange(n_slc) * SLC_LEN
    je = js + SLC_LEN - 1
    ov = ((cs[:, None] <= je[None, :]) & (ce[:, None] >= js[None, :])).astype(np.float32)
    ov[nc - 1] = 0.0
    return ov


def _block_layout(counts, n):
    rb = EXPERT_BLOCK
    counts = counts.astype(I32)
    padded = (counts + rb - 1) // rb * rb
    pad_end = jnp.cumsum(padded)
    pad_start = pad_end - padded
    n_blocks = -(-TOP_K * n // rb) + N_EXPERTS
    blk_start = jnp.arange(n_blocks, dtype=I32) * rb
    blk_e = jnp.minimum(jnp.searchsorted(pad_end, blk_start, side='right'), N_EXPERTS - 1).astype(I32)
    nused = (pad_end[-1] // rb).astype(I32).reshape(1)
    half = rb // 2
    tail_start = pad_end[-1] + half * jnp.arange(2 * N_EXPERTS, dtype=I32)
    tail_n = jnp.where(tail_start < n_blocks * rb, half, 0).astype(I32)
    fill_start = jnp.concatenate([pad_start + counts, jnp.minimum(tail_start, n_blocks * rb - half)])
    fill_n = jnp.concatenate([padded - counts, tail_n])
    return pad_start, fill_start, fill_n, blk_e, nused


def kernel(x, mem, ln0_g, ln0_b, rel_bias, w_in, cmp_pos_k, cmp_pos_v, cmp_k_w1, cmp_k_w2, cmp_v_w1, cmp_v_w2, mla_q_norm, mla_w_uq, mla_kv_norm, mla_w_ukv, mem_w_kv, w_branch, w_out, ln1_g, ln1_b, router_w, router_b, exp_w1, exp_w3, exp_w2, sh_w1, sh_w3, sh_w2, ln2_g, ln2_b):
    batch, seq, d = x.shape
    n = batch * seq
    l = 0
    row2 = lambda v: v.reshape(1, -1)
    tm = min(256, seq)

    pts = np.cumsum((0,) + IN_SPLITS)
    wcol = lambda k: w_in[l][:, pts[k]:pts[k + 1]]
    pad_cols = lambda w, lo, tot: jnp.pad(w, ((0, 0), (lo, tot - lo - w.shape[1])))
    ws = [wcol(0) * (NSA_DH ** -0.5), wcol(1), pad_cols(wcol(2), 0, LANE), wcol(3), wcol(4),
          pad_cols(wcol(5), ROPE_LANE0, LANE), wcol(6), wcol(7)]
    ws = [w.astype(BF) for w in ws]
    hd = MLA_NOPE + MLA_ROPE
    wuq = jnp.pad(mla_w_uq[l].reshape(MLA_Q_LORA, MLA_HEADS, hd), ((0, 0), (0, 0), (0, HEAD_PAD - hd)))
    wuq = wuq.reshape(MLA_Q_LORA, MLA_HEADS * HEAD_PAD).astype(BF)
    wukv = mla_w_ukv[l].reshape(MLA_KV_LORA, MLA_HEADS, MLA_NOPE + MLA_DV)
    wuk = jnp.pad(wukv[:, :, :MLA_NOPE], ((0, 0), (0, 0), (0, HEAD_PAD - MLA_NOPE)))
    wuk = wuk.reshape(MLA_KV_LORA, MLA_HEADS * HEAD_PAD).astype(BF)
    wuv = wukv[:, :, MLA_NOPE:].reshape(MLA_KV_LORA, MLA_HEADS * MLA_DV).astype(BF)

    h, qn, kvn, gn, cq, ckv, kr, qm, gm = _ln_inproj(x.reshape(n, d), row2(ln0_g), row2(ln0_b), ws, tm)

    q_mla, k_mla, v_mla = _mla_prep(cq, ckv, kr, row2(mla_q_norm[l]), row2(mla_kv_norm[l]), wuq, wuk, wuv, seq, tm)
    o_mla = _mla_attn(q_mla, k_mla, v_mla, batch, seq, min(512, seq))

    nc = seq // CMP_STRIDE
    n_slc = seq // SLC_LEN
    gw = NSA_GROUPS * NSA_DH
    cc = kvn[:, :2 * gw].reshape(batch, nc, CMP_STRIDE, 2 * NSA_GROUPS, NSA_DH)
    cc = cc.transpose(0, 3, 1, 2, 4).reshape(batch, 2 * NSA_GROUPS, nc, CMP_STRIDE * NSA_DH)
    pos = jnp.stack([cmp_pos_k[l], cmp_pos_v[l]]).reshape(2, 1, CMP_LEN * NSA_DH)
    w1c = jnp.stack([cmp_k_w1[l], cmp_v_w1[l]]).astype(BF)
    w2c = jnp.stack([cmp_k_w2[l], cmp_v_w2[l]]).astype(BF)
    kvc = _nsa_compress(cc, pos, w1c, w2c)
    overlap = jnp.asarray(_overlap_matrix(nc, n_slc), BF)
    o_cmp, selm = _nsa_cmp(rel_bias, qn, kvc, overlap, batch, seq)
    o_nsa = _nsa_attn(rel_bias, qn, kvn, selm, gn, o_cmp, batch, seq)

    o_mem = _mem_attn(qm, mem.reshape(-1, d), mem_w_kv[l].astype(BF), batch, seq, min(512, seq))

    h1 = _merge(o_nsa, o_mla, o_mem, gm, h, w_branch[l].astype(BF), w_out[l].astype(BF),
                row2(ln1_g[l]), row2(ln1_b[l]), tm)

    rw_t = router_w[l].T
    rw_hi = rw_t.astype(BF)
    rw_lo = (rw_t - rw_hi.astype(F32)).astype(BF)
    eidx_t, w_t, rank_t, counts = _router(h1, rw_hi, rw_lo, router_b[l].reshape(N_EXPERTS, 1), tm)
    pad_start, fill_start, fill_n, blk_e, nused = _block_layout(counts[:, 0], n)
    pos = _positions(eidx_t, rank_t, pad_start.astype(F32).reshape(N_EXPERTS, 1), tm)
    xs = _permute(fill_start, fill_n, pos, h1, blk_e.shape[0] * EXPERT_BLOCK, tm)
    ys = _experts(blk_e, nused, xs, exp_w1[l], exp_w3[l], exp_w2[l])
    out = _combine(pos, ys, w_t.T, h1, sh_w1[l].astype(BF), sh_w3[l].astype(BF), sh_w2[l].astype(BF),
                   row2(ln2_g[l]), row2(ln2_b[l]), min(128, seq))
    return out.reshape(batch, seq, d)
```

```python
import functools
import math

import numpy as np
import jax
import jax.numpy as jnp
from jax import lax
from jax.experimental import pallas as pl
from jax.experimental.pallas import tpu as pltpu

BF = jnp.bfloat16
F32 = jnp.float32
I32 = jnp.int32

D_MODEL = 1024
DEPTH = 1
NSA_HEADS = 8
NSA_GROUPS = 2
NSA_HPG = NSA_HEADS // NSA_GROUPS
NSA_DH = 64
CMP_LEN = 32
CMP_STRIDE = 16
CMP_HID = 256
SLC_LEN = 64
SLC_TOPN = 16
WIN = 512
MLA_HEADS = 8
MLA_NOPE = 64
MLA_ROPE = 32
MLA_DV = 64
MLA_Q_LORA = 768
MLA_KV_LORA = 256
ROPE_THETA = 10000.0
MEM_HEADS = 4
MEM_DH = 128
N_BRANCH = 3
BRANCH_W = NSA_HEADS * NSA_DH
REL_BUCKETS = 32
REL_MAX_DIST = 128
N_EXPERTS = 256
TOP_K = 8
N_EXPERT_GROUPS = 8
TOPK_GROUPS = 4
D_EXPERT = 256
ROUTE_SCALE = 2.5
EXPERT_BLOCK = 256
LN_EPS = 1e-5
RMS_EPS = 1e-6
NEG = -1e30
BIG = 1e30
ALPHA = (2 * DEPTH) ** 0.25
IN_SPLITS = (NSA_HEADS * NSA_DH, 6 * NSA_GROUPS * NSA_DH, 3 * NSA_HEADS, MLA_Q_LORA, MLA_KV_LORA,
             MLA_ROPE, MEM_HEADS * MEM_DH, N_BRANCH * D_MODEL)

LANE = 128
SUBLANES = 8
HEAD_PAD = 128
ROPE_LANE0 = MLA_NOPE
ROPE_HALF = MLA_ROPE // 2
VMEM_LIMIT = 56 * 1024 * 1024
NSA_TQ = 128
FAR_CHUNK = 512


def _cparams(sem):
    return pltpu.CompilerParams(dimension_semantics=sem, vmem_limit_bytes=VMEM_LIMIT)


def _bucket_starts():
    max_exact = REL_BUCKETS // 2
    d = np.arange(0, 4 * REL_MAX_DIST)
    nf = np.maximum(d, 1).astype(np.float32)
    large = max_exact + (np.log(nf / np.float32(max_exact)) / np.float32(math.log(REL_MAX_DIST / max_exact))
                         * np.float32(REL_BUCKETS - max_exact)).astype(np.int32)
    large = np.minimum(large, REL_BUCKETS - 1)
    bucket = np.where(d < max_exact, d, large)
    return [int(np.argmax(bucket >= b)) for b in range(REL_BUCKETS)]


BUCKET_START = _bucket_starts()
FAR_DIST = BUCKET_START[REL_BUCKETS - 1]


def _rel_bias(dist, tab_ref, head):
    val = jnp.full(dist.shape, tab_ref[REL_BUCKETS - 1, head], F32)
    for b in range(REL_BUCKETS - 2, -1, -1):
        val = jnp.where(dist < BUCKET_START[b + 1], tab_ref[b, head], val)
    return val


def _layer_norm(x, g, b):
    mu = jnp.mean(x, -1, keepdims=True)
    xc = x - mu
    var = jnp.mean(xc * xc, -1, keepdims=True)
    return xc * lax.rsqrt(var + LN_EPS) * g + b


def _dot(a, b):
    return jnp.dot(a, b, preferred_element_type=F32)


def _dot_nt(a, b):
    return lax.dot_general(a, b, (((1,), (1,)), ((), ())), preferred_element_type=F32)


def _ln_inproj_kernel(*refs, scales, n_rowmajor):
    x_ref, g_ref, b_ref = refs[:3]
    n_w = (len(refs) - 4) // 2
    w_refs = refs[3:3 + n_w]
    h_ref = refs[3 + n_w]
    o_refs = refs[4 + n_w:]
    h = _layer_norm(x_ref[...], g_ref[...], b_ref[...])
    h_ref[...] = h
    hb = h.astype(BF)
    for j, (w, o) in enumerate(zip(w_refs, o_refs)):
        if j < n_rowmajor:
            y = _dot(hb, w[...])
            o[...] = (y if scales[j] == 1.0 else y * scales[j]).astype(o.dtype)
        else:
            o[...] = _dot_nt(w[...], hb).astype(o.dtype)


def _ln_inproj(x2, g, b, ws, scales, ws_t, tm):
    n = x2.shape[0]
    row = lambda i: (i, 0)
    col = lambda i: (0, i)
    const = lambda i: (0, 0)
    in_specs = [pl.BlockSpec((tm, D_MODEL), row), pl.BlockSpec((1, D_MODEL), const), pl.BlockSpec((1, D_MODEL), const)]
    in_specs += [pl.BlockSpec(w.shape, const) for w in ws + ws_t]
    out_shape = [jax.ShapeDtypeStruct((n, D_MODEL), F32)]
    out_shape += [jax.ShapeDtypeStruct((n, w.shape[1]), BF) for w in ws]
    out_shape += [jax.ShapeDtypeStruct((w.shape[0], n), BF) for w in ws_t]
    out_specs = [pl.BlockSpec((tm, D_MODEL), row)] + [pl.BlockSpec((tm, w.shape[1]), row) for w in ws]
    out_specs += [pl.BlockSpec((w.shape[0], tm), col) for w in ws_t]
    return pl.pallas_call(
        functools.partial(_ln_inproj_kernel, scales=tuple(scales), n_rowmajor=len(ws)),
        grid=(n // tm,), in_specs=in_specs, out_specs=out_specs, out_shape=out_shape,
        compiler_params=_cparams(("parallel",)),
    )(x2, g, b, *ws, *ws_t)


def _rope_lanes(x, c, s1, s2):
    return x * c + pltpu.roll(x, LANE - ROPE_HALF, 1) * s1 + pltpu.roll(x, ROPE_HALF, 1) * s2


def _mla_prep_kernel(cq_ref, ckv_ref, kr_ref, qn_ref, kvn_ref, wuq, wuk, wuv,
                     cq_t, s1q_t, s2q_t, ck_t, s1k_t, s2k_t, q_out, k_out, v_out):
    cq = cq_ref[...].astype(F32)
    rq = cq * lax.rsqrt(jnp.mean(cq * cq, -1, keepdims=True) + RMS_EPS) * qn_ref[...]
    q = _dot(rq.astype(BF), wuq[...])
    ckv = ckv_ref[...].astype(F32)
    rkv = (ckv * lax.rsqrt(jnp.mean(ckv * ckv, -1, keepdims=True) + RMS_EPS) * kvn_ref[...]).astype(BF)
    kn = _dot(rkv, wuk[...])
    v_out[...] = _dot(rkv, wuv[...]).astype(v_out.dtype)
    kr = _rope_lanes(kr_ref[...].astype(F32), ck_t[...], s1k_t[...], s2k_t[...])
    cq_c, s1q, s2q = cq_t[...], s1q_t[...], s2q_t[...]
    for h in range(MLA_HEADS):
        sl = slice(h * HEAD_PAD, (h + 1) * HEAD_PAD)
        q_out[:, sl] = _rope_lanes(q[:, sl], cq_c, s1q, s2q).astype(q_out.dtype)
        k_out[:, sl] = (kn[:, sl] + kr).astype(k_out.dtype)


def _rope_tables(seq, scale):
    freq = ROPE_THETA ** (-jnp.arange(ROPE_HALF, dtype=F32) / ROPE_HALF)
    ang = jnp.arange(seq, dtype=F32)[:, None] * freq[None, :]
    cos, sin = jnp.cos(ang) * scale, jnp.sin(ang) * scale
    z = lambda w: jnp.zeros((seq, w), F32)
    tail = HEAD_PAD - ROPE_LANE0 - MLA_ROPE
    c = jnp.concatenate([jnp.full((seq, ROPE_LANE0), scale, F32), cos, cos, z(tail)], 1)
    s1 = jnp.concatenate([z(ROPE_LANE0), -sin, z(ROPE_HALF + tail)], 1)
    s2 = jnp.concatenate([z(ROPE_LANE0 + ROPE_HALF), sin, z(tail)], 1)
    return c, s1, s2


def _mla_prep(cq, ckv, kr, q_norm, kv_norm, wuq, wuk, wuv, seq, tm):
    n = cq.shape[0]
    nt = seq // tm
    row = lambda i: (i, 0)
    const = lambda i: (0, 0)
    pos = lambda i: (i % nt, 0)
    tabs = _rope_tables(seq, (MLA_NOPE + MLA_ROPE) ** -0.5 * math.log2(math.e)) + _rope_tables(seq, 1.0)
    in_specs = [pl.BlockSpec((tm, MLA_Q_LORA), row), pl.BlockSpec((tm, MLA_KV_LORA), row), pl.BlockSpec((tm, LANE), row),
                pl.BlockSpec((1, MLA_Q_LORA), const), pl.BlockSpec((1, MLA_KV_LORA), const),
                pl.BlockSpec(wuq.shape, const), pl.BlockSpec(wuk.shape, const), pl.BlockSpec(wuv.shape, const)]
    in_specs += [pl.BlockSpec((tm, LANE), pos)] * 6
    hq = MLA_HEADS * HEAD_PAD
    hv = MLA_HEADS * MLA_DV
    return pl.pallas_call(
        _mla_prep_kernel, grid=(n // tm,), in_specs=in_specs,
        out_specs=[pl.BlockSpec((tm, hq), row), pl.BlockSpec((tm, hq), row), pl.BlockSpec((tm, hv), row)],
        out_shape=[jax.ShapeDtypeStruct((n, hq), BF), jax.ShapeDtypeStruct((n, hq), BF), jax.ShapeDtypeStruct((n, hv), BF)],
        compiler_params=_cparams(("parallel",)),
    )(cq, ckv, kr, q_norm, kv_norm, wuq, wuk, wuv, *tabs)


def _mla_attn_kernel(q_ref, k_ref, v_ref, o_ref, m_scr, l_scr, acc_scr):
    i = pl.program_id(2)
    t = q_ref.shape[0]
    reps = t // LANE
    m_scr[...] = jnp.full(m_scr.shape, NEG, F32)
    l_scr[...] = jnp.zeros(l_scr.shape, F32)
    acc_scr[...] = jnp.zeros(acc_scr.shape, F32)

    def tile(kstart, diagonal):
        v = v_ref[pl.ds(kstart, t), :]
        for hh in range(2):
            sl = slice(hh * HEAD_PAD, (hh + 1) * HEAD_PAD)
            s = _dot_nt(q_ref[:, sl], k_ref[pl.ds(kstart, t), sl])
            if diagonal:
                row = lax.broadcasted_iota(I32, (t, t), 0)
                col = lax.broadcasted_iota(I32, (t, t), 1)
                s = jnp.where(col <= row, s, NEG)
            m_prev = m_scr[hh]
            m_new = jnp.maximum(m_prev, jnp.max(s, 1, keepdims=True))
            a = jnp.exp2(m_prev - m_new)
            e = jnp.exp2(s - jnp.tile(m_new, (1, reps)))
            l_scr[hh] = a * l_scr[hh] + jnp.sum(e, 1, keepdims=True)
            acc_scr[hh] = a * acc_scr[hh] + _dot(e.astype(BF), v)
            m_scr[hh] = m_new

    def body(c, carry):
        tile(pl.multiple_of(c * t, t), False)
        return carry

    lax.fori_loop(0, i, body, 0)
    tile(pl.multiple_of(i * t, t), True)
    lane = lax.broadcasted_iota(I32, (t, 2 * MLA_DV), 1)
    o = jnp.where(lane < MLA_DV, acc_scr[0] / l_scr[0], acc_scr[1] / l_scr[1])
    o_ref[...] = o.astype(o_ref.dtype)


def _mla_attn(q, k, v, batch, seq, t):
    n = q.shape[0]
    nt = seq // t
    qmap = lambda b, hp, i: (b * nt + i, hp)
    kmap = lambda b, hp, i: (b, hp)
    return pl.pallas_call(
        _mla_attn_kernel, grid=(batch, MLA_HEADS // 2, nt),
        in_specs=[pl.BlockSpec((t, 2 * HEAD_PAD), qmap), pl.BlockSpec((seq, 2 * HEAD_PAD), kmap),
                  pl.BlockSpec((seq, 2 * MLA_DV), kmap)],
        out_specs=pl.BlockSpec((t, 2 * MLA_DV), qmap),
        out_shape=jax.ShapeDtypeStruct((n, MLA_HEADS * MLA_DV), BF),
        scratch_shapes=[pltpu.VMEM((2, t, LANE), F32), pltpu.VMEM((2, t, LANE), F32), pltpu.VMEM((2, t, 2 * MLA_DV), F32)],
        compiler_params=_cparams(("parallel", "parallel", "arbitrary")),
    )(q, k, v)


def _nsa_compress_kernel(c_ref, pos_ref, w1_ref, w2_ref, o_ref):
    nc = c_ref.shape[0]
    half = CMP_STRIDE * NSA_DH
    c = c_ref[...]
    top = _dot(c, w1_ref[:half, :])
    bot = _dot(c, w1_ref[half:, :])
    posb = _dot(jnp.broadcast_to(pos_ref[...], (8, 2 * half)).astype(BF), w1_ref[...])[:1]
    hid = top + pltpu.roll(bot, nc - 1, 0) + posb
    o_ref[...] = _dot(jax.nn.gelu(hid).astype(BF), w2_ref[...]).astype(o_ref.dtype)


def _nsa_compress(cc, pos, w1, w2):
    b, _, nc, half = cc.shape
    sq = pl.Squeezed()
    return pl.pallas_call(
        _nsa_compress_kernel, grid=(b, 2 * NSA_GROUPS),
        in_specs=[pl.BlockSpec((sq, sq, nc, half), lambda i, c: (i, c, 0, 0)),
                  pl.BlockSpec((sq, 1, 2 * half), lambda i, c: (c // NSA_GROUPS, 0, 0)),
                  pl.BlockSpec((sq, 2 * half, CMP_HID), lambda i, c: (c // NSA_GROUPS, 0, 0)),
                  pl.BlockSpec((sq, CMP_HID, NSA_DH), lambda i, c: (c // NSA_GROUPS, 0, 0))],
        out_specs=pl.BlockSpec((sq, sq, nc, NSA_DH), lambda i, c: (i, c, 0, 0)),
        out_shape=jax.ShapeDtypeStruct((b, 2 * NSA_GROUPS, nc, NSA_DH), BF),
        compiler_params=_cparams(("parallel", "parallel")),
    )(cc, pos, w1, w2)


CMP_TQ = 512
CMP_BIAS_COLS = LANE // 2
CMP_BIAS_BACK = -(-(FAR_DIST + CMP_LEN - 1) // CMP_STRIDE)
assert (CMP_TQ - CMP_LEN) // CMP_STRIDE + CMP_BIAS_BACK < CMP_BIAS_COLS


def _nsa_cmp_kernel(tab_ref, q_ref, kc_ref, vc_ref, ov_ref, oc_ref, sel_ref, e_scr, *, n_top):
    b, g, i = pl.program_id(0), pl.program_id(1), pl.program_id(2)
    tq = q_ref.shape[0]
    nc = kc_ref.shape[0]
    n_slc = ov_ref.shape[1]
    qs = i * tq

    @pl.when((b == 0) & (g == 0) & (i == 0))
    def _():
        q_io = lax.broadcasted_iota(I32, (tq, LANE), 0)
        lane = lax.broadcasted_iota(I32, (tq, LANE), 1)
        jj = lane % CMP_BIAS_COLS
        dist = q_io - CMP_STRIDE * (jj - CMP_BIAS_BACK) - (CMP_LEN - 1)
        live = (dist >= 0) & (lane < 2 * CMP_BIAS_COLS)
        for h in range(NSA_HEADS):
            e = jnp.where(live, _rel_bias(dist, tab_ref, h) - tab_ref[REL_BUCKETS - 1, h], 0.0)
            hi = e.astype(BF)
            lo = (e - hi.astype(F32)).astype(BF)
            e_scr[h] = jnp.where(lane < CMP_BIAS_COLS, hi, lo)

    n0 = qs // CMP_STRIDE
    jrow = lax.broadcasted_iota(I32, (LANE, nc), 0)
    ncol = lax.broadcasted_iota(I32, (LANE, nc), 1)
    ft = jnp.where((ncol == n0 + (jrow % CMP_BIAS_COLS) - CMP_BIAS_BACK) & (jrow < 2 * CMP_BIAS_COLS), 1.0, 0.0).astype(BF)

    t = qs + lax.broadcasted_iota(I32, (tq, nc), 0)
    n_io = lax.broadcasted_iota(I32, (tq, nc), 1)
    mask = (t >= n_io * CMP_STRIDE + (CMP_LEN - 1)) & (n_io < nc - 1)
    kc = kc_ref[...]
    vc = vc_ref[...]
    psum = jnp.zeros((tq, nc), F32)
    for h in range(NSA_HPG):
        head = g * NSA_HPG + h
        qh = q_ref[:, h * NSA_DH:(h + 1) * NSA_DH]
        s = _dot_nt(qh, kc) + _dot(e_scr[head], ft)
        s = jnp.where(mask, s, NEG)
        m = jnp.max(s, 1, keepdims=True)
        e = jnp.where(mask, jnp.exp(s - m), 0.0)
        l = jnp.sum(e, 1, keepdims=True)
        p = e * jnp.where(l > 0.0, 1.0 / l, 0.0)
        oc_ref[:, h * NSA_DH:(h + 1) * NSA_DH] = _dot(p.astype(BF), vc).astype(oc_ref.dtype)
        psum = psum + p

    ov = ov_ref[...]
    p0 = psum.astype(BF)
    r1 = psum - p0.astype(F32)
    p1 = r1.astype(BF)
    p2 = (r1 - p1.astype(F32)).astype(BF)
    imp = _dot(p0, ov) + _dot(p1, ov) + _dot(p2, ov)

    tj = qs + lax.broadcasted_iota(I32, (tq, n_slc), 0)
    j = lax.broadcasted_iota(I32, (tq, n_slc), 1)
    cur = tj // SLC_LEN
    forced = (j == 0) | (j == cur) | (j == cur - 1)
    work = jnp.where(j * SLC_LEN > tj, NEG, jnp.where(forced, BIG, imp))
    sel = jnp.full((tq, n_slc), NEG, F32)
    for _ in range(n_top):
        mx = jnp.max(work, 1, keepdims=True)
        first = jnp.min(jnp.where(work == mx, j, n_slc), 1, keepdims=True)
        hit = j == first
        sel = jnp.where(hit & (mx > 0.5 * NEG), 0.0, sel)
        work = jnp.where(hit, -jnp.inf, work)
    sel_ref[...] = sel.T


def _nsa_cmp(tab, qn, kvc, overlap, batch, seq):
    n = qn.shape[0]
    tq = min(CMP_TQ, seq)
    nq = seq // tq
    nc = kvc.shape[2]
    n_slc = overlap.shape[1]
    sq = pl.Squeezed()
    gw = NSA_HPG * NSA_DH
    return pl.pallas_call(
        functools.partial(_nsa_cmp_kernel, n_top=min(SLC_TOPN, n_slc)), grid=(batch, NSA_GROUPS, nq),
        in_specs=[pl.BlockSpec(memory_space=pltpu.SMEM),
                  pl.BlockSpec((tq, gw), lambda b, g, i: (b * nq + i, g)),
                  pl.BlockSpec((sq, sq, nc, NSA_DH), lambda b, g, i: (b, g, 0, 0)),
                  pl.BlockSpec((sq, sq, nc, NSA_DH), lambda b, g, i: (b, NSA_GROUPS + g, 0, 0)),
                  pl.BlockSpec((nc, n_slc), lambda b, g, i: (0, 0))],
        out_specs=[pl.BlockSpec((tq, gw), lambda b, g, i: (b * nq + i, g)),
                   pl.BlockSpec((sq, n_slc, tq), lambda b, g, i: (g, 0, b * nq + i))],
        out_shape=[jax.ShapeDtypeStruct((n, NSA_HEADS * NSA_DH), BF),
                   jax.ShapeDtypeStruct((NSA_GROUPS, n_slc, n), F32)],
        scratch_shapes=[pltpu.VMEM((NSA_HEADS, tq, LANE), BF)],
        compiler_params=_cparams(("arbitrary", "arbitrary", "arbitrary")),
    )(tab, qn, kvc, kvc, overlap)


LOG2E = math.log2(math.e)


def _nsa_attn_kernel(tab_ref, qt_ref, ks_ref, kw_ref, vst_ref, vwt_ref, sel_ref, gate_ref, gate_t_ref, oc_ref, o_ref,
                     d_scr, q_scr, m_scr, l_scr, acc_scr):
    b, i = pl.program_id(0), pl.program_id(1)
    tq = NSA_TQ
    lanes = NSA_HPG * tq
    qs = i * tq
    near_w = 2 * tq
    far_w = FAR_CHUNK
    win_far_w = WIN - tq

    @pl.when((b == 0) & (i == 0))
    def _():
        kk = lax.broadcasted_iota(I32, (near_w, tq), 0)
        q_io = lax.broadcasted_iota(I32, (near_w, tq), 1)
        dist = jnp.maximum(q_io + tq - kk, 0)
        for g in range(NSA_GROUPS):
            d_scr[g] = jnp.concatenate(
                [(_rel_bias(dist, tab_ref, g * NSA_HPG + h) - tab_ref[REL_BUCKETS - 1, g * NSA_HPG + h]) * LOG2E
                 for h in range(NSA_HPG)], axis=1)

    qt = qt_ref[...]
    for g in range(NSA_GROUPS):
        qg = jnp.concatenate([qt[(g * NSA_HPG + h) * NSA_DH:(g * NSA_HPG + h + 1) * NSA_DH, :]
                              for h in range(NSA_HPG)], axis=1)
        parts = [jnp.zeros((NSA_DH, lanes), BF)] * NSA_GROUPS
        parts[g] = qg
        q_scr[g] = jnp.concatenate(parts, axis=0)

    def update(slot, s, vt):
        m_prev = m_scr[slot]
        m_new = jnp.maximum(m_prev, jnp.max(s, 0, keepdims=True))
        a = jnp.exp2(m_prev - m_new)
        e = jnp.exp2(s - m_new)
        l_scr[slot] = a * l_scr[slot] + jnp.sum(e, 0, keepdims=True)
        acc_scr[slot] = a * acc_scr[slot] + _dot(vt, e.astype(BF))
        m_scr[slot] = m_new

    def all_heads(x):
        return jnp.tile(x, (1, NSA_HPG))

    def sel_rows(g, first_block, n_blk):
        return jnp.concatenate(
            [jnp.broadcast_to(sel_ref[g, pl.ds(jnp.maximum(first_block + r, 0), 1), :], (SLC_LEN, tq))
             for r in range(n_blk)], axis=0)

    m_scr[...] = jnp.full(m_scr.shape, NEG, F32)
    l_scr[...] = jnp.zeros(l_scr.shape, F32)
    acc_scr[...] = jnp.zeros(acc_scr.shape, F32)

    kk = lax.broadcasted_iota(I32, (near_w, tq), 0)
    q_io = lax.broadcasted_iota(I32, (near_w, tq), 1)
    kpos_near = qs - tq + kk
    causal_add = jnp.where((kpos_near >= 0) & (kpos_near <= qs + q_io), 0.0, NEG)
    start_a = pl.multiple_of(qs, tq)
    start_b = pl.multiple_of(jnp.maximum(qs - tq, 0), tq)
    for g in range(NSA_GROUPS):
        gs = slice(g * NSA_DH, (g + 1) * NSA_DH)
        for br, (k_ref, vt_ref) in enumerate(((ks_ref, vst_ref), (kw_ref, vwt_ref))):
            k = jnp.concatenate([k_ref[pl.ds(start_b, tq), :], k_ref[pl.ds(start_a, tq), :]], axis=0)
            vt = jnp.concatenate([vt_ref[gs, pl.ds(start_b, tq)], vt_ref[gs, pl.ds(start_a, tq)]], axis=1)
            add = causal_add + sel_rows(g, 2 * i - 2, near_w // SLC_LEN) if br == 0 else causal_add
            update(2 * g + br, _dot(k, q_scr[g]) + d_scr[g] + all_heads(add), vt)

    ws = pl.multiple_of(jnp.maximum(qs - WIN, 0), tq)
    kpos_w = ws + lax.broadcasted_iota(I32, (win_far_w, tq), 0)
    t_w = qs + lax.broadcasted_iota(I32, (win_far_w, tq), 1)
    add_w = all_heads(jnp.where((kpos_w < qs - tq) & (kpos_w > t_w - WIN), 0.0, NEG))
    k_w = kw_ref[pl.ds(ws, win_far_w), :]
    for g in range(NSA_GROUPS):
        gs = slice(g * NSA_DH, (g + 1) * NSA_DH)
        update(2 * g + 1, _dot(k_w, q_scr[g]) + add_w, vwt_ref[gs, pl.ds(ws, win_far_w)])

    n_far = (jnp.maximum(i - 1, 0) + (far_w // tq - 1)) // (far_w // tq)
    krow = lax.broadcasted_iota(I32, (far_w, tq), 0)

    def far_body(c, carry):
        base = pl.multiple_of(c * far_w, far_w)
        in_range = jnp.where(base + krow < qs - tq, 0.0, NEG)
        k = ks_ref[pl.ds(base, far_w), :]
        logits = []
        for g in range(NSA_GROUPS):
            add = all_heads(sel_rows(g, c * (far_w // SLC_LEN), far_w // SLC_LEN) + in_range)
            logits.append(_dot(k, q_scr[g]) + add)
        for g in range(NSA_GROUPS):
            update(2 * g, logits[g], vst_ref[g * NSA_DH:(g + 1) * NSA_DH, pl.ds(base, far_w)])
        return carry

    lax.fori_loop(0, n_far, far_body, 0)

    gates = jax.nn.sigmoid(gate_ref[...].astype(F32))
    gates_t = jax.nn.sigmoid(gate_t_ref[...].astype(F32))
    for g in range(NSA_GROUPS):
        o_s = acc_scr[2 * g] / l_scr[2 * g]
        o_w = acc_scr[2 * g + 1] / l_scr[2 * g + 1]
        for h in range(NSA_HPG):
            head = g * NSA_HPG + h
            hl = slice(head * NSA_DH, (head + 1) * NSA_DH)
            cl = slice(h * tq, (h + 1) * tq)
            sw_t = (gates_t[3 * head + 1:3 * head + 2, :] * o_s[:, cl]
                    + gates_t[3 * head + 2:3 * head + 3, :] * o_w[:, cl])
            o = gates[:, 3 * head:3 * head + 1] * oc_ref[:, hl].astype(F32) + sw_t.T
            o_ref[:, hl] = o.astype(o_ref.dtype)


def _nsa_attn(tab, qn_t, kvn, vs_t, vw_t, sel_t, gn, gn_t, oc, batch, seq):
    n = kvn.shape[0]
    tq = NSA_TQ
    nq = seq // tq
    n_slc = sel_t.shape[1]
    hw = NSA_HEADS * NSA_DH
    gw = NSA_GROUPS * NSA_DH
    lanes = NSA_HPG * tq
    row = lambda b, i: (b * nq + i, 0)
    col = lambda b, i: (0, b * nq + i)
    k_spec = lambda kind: pl.BlockSpec((seq, gw), lambda b, i: (b, kind))
    vt_spec = pl.BlockSpec((gw, seq), lambda b, i: (0, b))
    n_state = 2 * NSA_GROUPS
    return pl.pallas_call(
        _nsa_attn_kernel, grid=(batch, nq),
        in_specs=[pl.BlockSpec(memory_space=pltpu.SMEM), pl.BlockSpec((hw, tq), col),
                  k_spec(2), k_spec(4), vt_spec, vt_spec,
                  pl.BlockSpec((NSA_GROUPS, n_slc, tq), lambda b, i: (0, 0, b * nq + i)),
                  pl.BlockSpec((tq, LANE), row), pl.BlockSpec((LANE, tq), col), pl.BlockSpec((tq, hw), row)],
        out_specs=pl.BlockSpec((tq, hw), row),
        out_shape=jax.ShapeDtypeStruct((n, hw), BF),
        scratch_shapes=[pltpu.VMEM((NSA_GROUPS, 2 * tq, lanes), F32), pltpu.VMEM((NSA_GROUPS, gw, lanes), BF),
                        pltpu.VMEM((n_state, 1, lanes), F32), pltpu.VMEM((n_state, 1, lanes), F32),
                        pltpu.VMEM((n_state, NSA_DH, lanes), F32)],
        compiler_params=_cparams(("arbitrary", "arbitrary")),
    )(tab, qn_t, kvn, kvn, vs_t, vw_t, sel_t, gn, gn_t, oc)


def _mem_attn_kernel(q_ref, mem_ref, w_ref, o_ref, kv_scr):
    @pl.when(pl.program_id(1) == 0)
    def _():
        kv_scr[...] = _dot(mem_ref[...].astype(BF), w_ref[...]).astype(BF)

    hw = MEM_HEADS * MEM_DH
    for h in range(MEM_HEADS):
        sl = slice(h * MEM_DH, (h + 1) * MEM_DH)
        s = _dot_nt(q_ref[:, sl], kv_scr[:, sl])
        e = jnp.exp(s - jnp.max(s, 1, keepdims=True))
        p = e / jnp.sum(e, 1, keepdims=True)
        o_ref[:, sl] = _dot(p.astype(BF), kv_scr[:, hw + h * MEM_DH:hw + (h + 1) * MEM_DH]).astype(o_ref.dtype)


def _mem_attn(qm, mem2, w_kv, batch, seq, tq):
    n = qm.shape[0]
    nq = seq // tq
    m = mem2.shape[0] // batch
    hw = MEM_HEADS * MEM_DH
    return pl.pallas_call(
        _mem_attn_kernel, grid=(batch, nq),
        in_specs=[pl.BlockSpec((tq, hw), lambda b, i: (b * nq + i, 0)),
                  pl.BlockSpec((m, D_MODEL), lambda b, i: (b, 0)),
                  pl.BlockSpec((D_MODEL, 2 * hw), lambda b, i: (0, 0))],
        out_specs=pl.BlockSpec((tq, hw), lambda b, i: (b * nq + i, 0)),
        out_shape=jax.ShapeDtypeStruct((n, hw), BF),
        scratch_shapes=[pltpu.VMEM((m, 2 * hw), BF)],
        compiler_params=_cparams(("arbitrary", "arbitrary")),
    )(qm, mem2, w_kv)


def _merge_kernel(on_ref, ol_ref, om_ref, gm_ref, h_ref, wb_ref, wo_ref, g_ref, b_ref, h1_ref):
    merged = None
    for c, o_ref in enumerate((on_ref, ol_ref, om_ref)):
        gate = jax.nn.sigmoid(gm_ref[:, c * D_MODEL:(c + 1) * D_MODEL].astype(F32))
        term = gate * _dot(o_ref[...], wb_ref[c])
        merged = term if merged is None else merged + term
    y = ALPHA * h_ref[...] + _dot(merged.astype(BF), wo_ref[...])
    h1_ref[...] = _layer_norm(y, g_ref[...], b_ref[...])


def _merge(o_nsa, o_mla, o_mem, gm, h, wb, wo, g, b, tm):
    n = h.shape[0]
    row = lambda i: (i, 0)
    const = lambda i: (0, 0)
    return pl.pallas_call(
        _merge_kernel, grid=(n // tm,),
        in_specs=[pl.BlockSpec((tm, BRANCH_W), row)] * 3 + [
            pl.BlockSpec((tm, N_BRANCH * D_MODEL), row), pl.BlockSpec((tm, D_MODEL), row),
            pl.BlockSpec(wb.shape, lambda i: (0, 0, 0)), pl.BlockSpec(wo.shape, const),
            pl.BlockSpec((1, D_MODEL), const), pl.BlockSpec((1, D_MODEL), const)],
        out_specs=pl.BlockSpec((tm, D_MODEL), row),
        out_shape=jax.ShapeDtypeStruct((n, D_MODEL), F32),
        compiler_params=_cparams(("parallel",)),
    )(o_nsa, o_mla, o_mem, gm, h, wb, wo, g, b)


def _first_max(vals, idx, limit):
    mx = jnp.max(vals, 0, keepdims=True)
    first = jnp.min(jnp.where(vals == mx, idx, limit), 0, keepdims=True)
    return mx, first


def _router_kernel(h_ref, whi_ref, wlo_ref, b_ref, tri_ref, idx_ref, w_ref, rank_ref, cnt_ref):
    @pl.when(pl.program_id(0) == 0)
    def _():
        cnt_ref[...] = jnp.zeros(cnt_ref.shape, F32)

    h = h_ref[...]
    hhi = h.astype(BF)
    hlo = (h - hhi.astype(F32)).astype(BF)
    whi = whi_ref[...]
    logits = _dot_nt(whi, hhi) + _dot_nt(whi, hlo) + _dot_nt(wlo_ref[...], hhi)
    s = jax.nn.sigmoid(logits)
    sb = s + b_ref[...]
    tm = s.shape[1]
    gsz = N_EXPERTS // N_EXPERT_GROUPS
    e_io = lax.broadcasted_iota(I32, (gsz, tm), 0)
    scores = []
    for g in range(N_EXPERT_GROUPS):
        vals = sb[g * gsz:(g + 1) * gsz]
        m1, first = _first_max(vals, e_io, gsz)
        m2 = jnp.max(jnp.where(e_io == first, -jnp.inf, vals), 0, keepdims=True)
        scores.append(m1 + m2)
    gs = jnp.concatenate(scores, axis=0)
    g_io = lax.broadcasted_iota(I32, (N_EXPERT_GROUPS, tm), 0)
    x_io = lax.broadcasted_iota(I32, (N_EXPERTS, tm), 0)
    allowed = jnp.zeros((N_EXPERTS, tm), jnp.bool_)
    for _ in range(TOPK_GROUPS):
        _, first = _first_max(gs, g_io, N_EXPERT_GROUPS)
        gs = jnp.where(g_io == first, -jnp.inf, gs)
        allowed = allowed | (x_io // gsz == first)
    work = jnp.where(allowed, sb, NEG)
    base = cnt_ref[:, :1]
    tri = tri_ref[...]
    idxs, ws, ranks = [], [], []
    for _ in range(TOP_K):
        _, first = _first_max(work, x_io, N_EXPERTS)
        hit = x_io == first
        idxs.append(first)
        ws.append(jnp.sum(jnp.where(hit, s, 0.0), 0, keepdims=True))
        work = jnp.where(hit, -jnp.inf, work)
        onehot = jnp.where(hit, 1.0, 0.0)
        before = _dot(onehot.astype(BF), tri)
        ranks.append(jnp.sum(jnp.where(hit, base + before, 0.0), 0, keepdims=True))
        base = base + jnp.sum(onehot, 1, keepdims=True)
    wsel = jnp.concatenate(ws, axis=0)
    idx_ref[...] = jnp.concatenate(idxs, axis=0)
    w_ref[...] = wsel / jnp.sum(wsel, 0, keepdims=True) * ROUTE_SCALE
    rank_ref[...] = jnp.concatenate(ranks, axis=0).astype(I32)
    cnt_ref[...] = jnp.broadcast_to(base, cnt_ref.shape)


def _router(h1, whi, wlo, rb, tm):
    n = h1.shape[0]
    tri = jnp.asarray(np.triu(np.ones((tm, tm), np.float32), 1), BF)
    slot = pl.BlockSpec((TOP_K, tm), lambda i: (0, i))
    const = lambda i: (0, 0)
    return pl.pallas_call(
        _router_kernel, grid=(n // tm,),
        in_specs=[pl.BlockSpec((tm, D_MODEL), lambda i: (i, 0)),
                  pl.BlockSpec((N_EXPERTS, D_MODEL), const), pl.BlockSpec((N_EXPERTS, D_MODEL), const),
                  pl.BlockSpec((N_EXPERTS, 1), const), pl.BlockSpec((tm, tm), const)],
        out_specs=[slot, slot, slot, pl.BlockSpec((N_EXPERTS, LANE), const)],
        out_shape=[jax.ShapeDtypeStruct((TOP_K, n), I32), jax.ShapeDtypeStruct((TOP_K, n), F32),
                   jax.ShapeDtypeStruct((TOP_K, n), I32), jax.ShapeDtypeStruct((N_EXPERTS, LANE), F32)],
        compiler_params=_cparams(("arbitrary",)),
    )(h1, whi, wlo, rb, tri)


def _pos_kernel(idx_ref, rank_ref, start_ref, pos_ref):
    tm = idx_ref.shape[1]
    x_io = lax.broadcasted_iota(I32, (N_EXPERTS, tm), 0)
    start = start_ref[...]
    rows = [jnp.sum(jnp.where(x_io == idx_ref[k:k + 1, :], start, 0.0), 0, keepdims=True) for k in range(TOP_K)]
    pos_ref[...] = jnp.concatenate(rows, axis=0).astype(I32) + rank_ref[...]


def _positions(eidx_t, rank_t, pad_start, tm):
    n = eidx_t.shape[1]
    slot = pl.BlockSpec((TOP_K, tm), lambda i: (0, i))
    return pl.pallas_call(
        _pos_kernel, grid=(n // tm,),
        in_specs=[slot, slot, pl.BlockSpec((N_EXPERTS, 1), lambda i: (0, 0))],
        out_specs=slot, out_shape=jax.ShapeDtypeStruct((TOP_K, n), I32),
        compiler_params=_cparams(("parallel",)),
    )(eidx_t, rank_t, pad_start)


ROW_UNROLL = 4


def _fill_groups():
    p = EXPERT_BLOCK // 2
    while p >= 1:
        yield p
        p //= 2


def _permute_kernel(fill_start_ref, fill_n_ref, pos_ref, h_ref, xs_hbm, zbuf, row_sem, fill_sem, *, fills_per_step):
    i = pl.program_id(0)
    tm = h_ref.shape[0]
    n_fills = fill_n_ref.shape[0]
    zbuf[...] = jnp.zeros(zbuf.shape, zbuf.dtype)

    def fill_copies(e):
        n = fill_n_ref[e]
        start = fill_start_ref[e]
        for p in _fill_groups():
            @pl.when((n & p) != 0)
            def _(p=p):
                off = start + (n & (p - 1))
                if p < SUBLANES:
                    for r in range(p):
                        pltpu.make_async_copy(zbuf.at[pl.ds(r, 1)], xs_hbm.at[pl.ds(off + r, 1)], fill_sem).start()
                else:
                    pltpu.make_async_copy(zbuf.at[pl.ds(0, p)], xs_hbm.at[pl.ds(pl.multiple_of(off, SUBLANES), p)],
                                          fill_sem).start()

    def fill_waits(e):
        n = fill_n_ref[e]
        for p in _fill_groups():
            @pl.when((n & p) != 0)
            def _(p=p):
                pltpu.make_async_copy(zbuf.at[pl.ds(0, p)], zbuf.at[pl.ds(0, p)], fill_sem).wait()

    for q in range(fills_per_step):
        e = i * fills_per_step + q

        @pl.when(e < n_fills)
        def _(e=e):
            fill_copies(e)

    def body(t, c):
        for k in range(TOP_K):
            pltpu.make_async_copy(h_ref.at[pl.ds(t, 1)], xs_hbm.at[pl.ds(pos_ref[k, t], 1)], row_sem).start()
        return c

    lax.fori_loop(0, tm, body, 0, unroll=ROW_UNROLL)

    for q in range(fills_per_step):
        e = i * fills_per_step + q

        @pl.when(e < n_fills)
        def _(e=e):
            fill_waits(e)

    pltpu.make_async_copy(xs_hbm.at[pl.ds(0, TOP_K * tm)], xs_hbm.at[pl.ds(0, TOP_K * tm)], row_sem).wait()


def _permute(fill_start, fill_n, pos, h1, n_rows, tm):
    n = h1.shape[0]
    nt = n // tm
    grid_spec = pltpu.PrefetchScalarGridSpec(
        num_scalar_prefetch=2, grid=(nt,),
        in_specs=[pl.BlockSpec((TOP_K, tm), lambda i, fs, fn: (0, i), memory_space=pltpu.SMEM),
                  pl.BlockSpec((tm, D_MODEL), lambda i, fs, fn: (i, 0))],
        out_specs=pl.BlockSpec(memory_space=pl.ANY),
        scratch_shapes=[pltpu.VMEM((EXPERT_BLOCK // 2, D_MODEL), F32), pltpu.SemaphoreType.DMA(()),
                        pltpu.SemaphoreType.DMA(())])
    return pl.pallas_call(
        functools.partial(_permute_kernel, fills_per_step=-(-fill_n.shape[0] // nt)), grid_spec=grid_spec,
        out_shape=jax.ShapeDtypeStruct((n_rows, D_MODEL), F32),
        compiler_params=_cparams(("arbitrary",)),
    )(fill_start, fill_n, pos, h1)


def _experts_kernel(blk_e_ref, nused_ref, x_ref, w1_ref, w3_ref, w2_ref, y_ref, w1b, w3b, w2b):
    j = pl.program_id(0)

    @pl.when(j < nused_ref[0])
    def _():
        first_of_expert = (j == 0) | (blk_e_ref[j] != blk_e_ref[jnp.maximum(j - 1, 0)])

        @pl.when(first_of_expert)
        def _():
            w1b[...] = w1_ref[...].astype(BF)
            w3b[...] = w3_ref[...].astype(BF)
            w2b[...] = w2_ref[...].astype(BF)

        x = x_ref[...].astype(BF)
        a = _dot(x, w1b[...])
        y_ref[...] = _dot((a * jax.nn.sigmoid(a) * _dot(x, w3b[...])).astype(BF), w2b[...])

    @pl.when(j >= nused_ref[0])
    def _():
        y_ref[...] = jnp.zeros(y_ref.shape, y_ref.dtype)


def _experts(blk_e, nused, xs, w1, w3, w2):
    n_blocks = blk_e.shape[0]
    rb = EXPERT_BLOCK
    sq = pl.Squeezed()
    wmap = lambda j, be, nu: (be[j], 0, 0)
    grid_spec = pltpu.PrefetchScalarGridSpec(
        num_scalar_prefetch=2, grid=(n_blocks,),
        in_specs=[pl.BlockSpec((rb, D_MODEL), lambda j, be, nu: (jnp.minimum(j, nu[0] - 1), 0)),
                  pl.BlockSpec((sq, D_MODEL, D_EXPERT), wmap), pl.BlockSpec((sq, D_MODEL, D_EXPERT), wmap),
                  pl.BlockSpec((sq, D_EXPERT, D_MODEL), wmap)],
        out_specs=pl.BlockSpec((rb, D_MODEL), lambda j, be, nu: (j, 0)),
        scratch_shapes=[pltpu.VMEM((D_MODEL, D_EXPERT), BF), pltpu.VMEM((D_MODEL, D_EXPERT), BF),
                        pltpu.VMEM((D_EXPERT, D_MODEL), BF)])
    return pl.pallas_call(
        _experts_kernel, grid_spec=grid_spec,
        out_shape=jax.ShapeDtypeStruct((n_blocks * rb, D_MODEL), F32),
        compiler_params=_cparams(("arbitrary",)),
    )(blk_e, nused, xs, w1, w3, w2)


def _combine_kernel(pos_ref, pos_next_ref, ys_hbm, w_ref, h_ref, s1_ref, s3_ref, s2_ref, g_ref, b_ref, o_ref, ybuf, sem):
    i = pl.program_id(0)
    nt = pl.num_programs(0)
    tm = h_ref.shape[0]

    def start_gather(p_ref, slot):
        def body(t, c):
            for k in range(TOP_K):
                pltpu.make_async_copy(ys_hbm.at[pl.ds(p_ref[k, t], 1)], ybuf.at[slot, k, pl.ds(t, 1)], sem.at[slot]).start()
            return c

        lax.fori_loop(0, tm, body, 0, unroll=ROW_UNROLL)

    @pl.when(i == 0)
    def _():
        start_gather(pos_ref, 0)

    slot = i % 2

    @pl.when(i + 1 < nt)
    def _():
        start_gather(pos_next_ref, 1 - slot)

    h = h_ref[...]
    hb = h.astype(BF)
    a = _dot(hb, s1_ref[...])
    y = ALPHA * h + _dot((a * jax.nn.sigmoid(a) * _dot(hb, s3_ref[...])).astype(BF), s2_ref[...])
    pltpu.make_async_copy(ybuf.at[slot], ybuf.at[slot], sem.at[slot]).wait()
    w = w_ref[...]
    for k in range(TOP_K):
        y = y + w[:, k:k + 1] * ybuf[slot, k]
    o_ref[...] = _layer_norm(y, g_ref[...], b_ref[...])


def _combine(pos, ys, w, h1, s1, s3, s2, g, b, tm):
    n = h1.shape[0]
    nt = n // tm
    row = lambda i: (i, 0)
    const = lambda i: (0, 0)
    return pl.pallas_call(
        _combine_kernel, grid=(nt,),
        in_specs=[pl.BlockSpec((TOP_K, tm), lambda i: (0, i), memory_space=pltpu.SMEM),
                  pl.BlockSpec((TOP_K, tm), lambda i: (0, jnp.minimum(i + 1, nt - 1)), memory_space=pltpu.SMEM),
                  pl.BlockSpec(memory_space=pl.ANY),
                  pl.BlockSpec((tm, TOP_K), row), pl.BlockSpec((tm, D_MODEL), row),
                  pl.BlockSpec(s1.shape, const), pl.BlockSpec(s3.shape, const), pl.BlockSpec(s2.shape, const),
                  pl.BlockSpec((1, D_MODEL), const), pl.BlockSpec((1, D_MODEL), const)],
        out_specs=pl.BlockSpec((tm, D_MODEL), row),
        out_shape=jax.ShapeDtypeStruct((n, D_MODEL), F32),
        scratch_shapes=[pltpu.VMEM((2, TOP_K, tm, D_MODEL), F32), pltpu.SemaphoreType.DMA((2,))],
        compiler_params=_cparams(("arbitrary",)),
    )(pos, pos, ys, w, h1, s1, s3, s2, g, b)


def _overlap_matrix(nc, n_slc):
    cs = np.arange(nc) * CMP_STRIDE
    ce = cs + CMP_LEN - 1
    js = np.arange(n_slc) * SLC_LEN
    je = js + SLC_LEN - 1
    ov = ((cs[:, None] <= je[None, :]) & (ce[:, None] >= js[None, :])).astype(np.float32)
    ov[nc - 1] = 0.0
    return ov


def _block_layout(counts, n):
    rb = EXPERT_BLOCK
    counts = counts.astype(I32)
    padded = (counts + rb - 1) // rb * rb
    pad_end = jnp.cumsum(padded)
    pad_start = pad_end - padded
    n_blocks = -(-TOP_K * n // rb) + N_EXPERTS
    blk_start = jnp.arange(n_blocks, dtype=I32) * rb
    blk_e = jnp.minimum(jnp.sum(pad_end[None, :] <= blk_start[:, None], axis=1), N_EXPERTS - 1).astype(I32)
    nused = (pad_end[-1] // rb).astype(I32).reshape(1)
    half = rb // 2
    tail_start = pad_end[-1] + half * jnp.arange(2 * N_EXPERTS, dtype=I32)
    tail_n = jnp.where(tail_start < n_blocks * rb, half, 0).astype(I32)
    fill_start = jnp.concatenate([pad_start + counts, jnp.minimum(tail_start, n_blocks * rb - half)])
    fill_n = jnp.concatenate([padded - counts, tail_n])
    return pad_start, fill_start, fill_n, blk_e, nused


def kernel(x, mem, ln0_g, ln0_b, rel_bias, w_in, cmp_pos_k, cmp_pos_v, cmp_k_w1, cmp_k_w2, cmp_v_w1, cmp_v_w2, mla_q_norm, mla_w_uq, mla_kv_norm, mla_w_ukv, mem_w_kv, w_branch, w_out, ln1_g, ln1_b, router_w, router_b, exp_w1, exp_w3, exp_w2, sh_w1, sh_w3, sh_w2, ln2_g, ln2_b):
    batch, seq, d = x.shape
    n = batch * seq
    l = 0
    row2 = lambda v: v.reshape(1, -1)
    tm = min(256, seq)

    pts = np.cumsum((0,) + IN_SPLITS)
    wcol = lambda k: w_in[l][:, pts[k]:pts[k + 1]]
    pad_cols = lambda w, lo, tot: jnp.pad(w, ((0, 0), (lo, tot - lo - w.shape[1])))
    ws = [wcol(0) * (NSA_DH ** -0.5), wcol(1), pad_cols(wcol(2), 0, LANE), wcol(3), wcol(4),
          pad_cols(wcol(5), ROPE_LANE0, LANE), wcol(6), wcol(7)]
    ws = [w.astype(BF) for w in ws]
    scales = [1.0] * 6 + [MEM_DH ** -0.5, 1.0]
    gw = NSA_GROUPS * NSA_DH
    ws_t = [wcol(0) * (NSA_DH ** -0.5 * LOG2E), wcol(1)[:, 3 * gw:4 * gw], wcol(1)[:, 5 * gw:6 * gw],
            pad_cols(wcol(2), 0, LANE)]
    ws_t = [w.T.astype(BF) for w in ws_t]
    hd = MLA_NOPE + MLA_ROPE
    wuq = jnp.pad(mla_w_uq[l].reshape(MLA_Q_LORA, MLA_HEADS, hd), ((0, 0), (0, 0), (0, HEAD_PAD - hd)))
    wuq = wuq.reshape(MLA_Q_LORA, MLA_HEADS * HEAD_PAD).astype(BF)
    wukv = mla_w_ukv[l].reshape(MLA_KV_LORA, MLA_HEADS, MLA_NOPE + MLA_DV)
    wuk = jnp.pad(wukv[:, :, :MLA_NOPE], ((0, 0), (0, 0), (0, HEAD_PAD - MLA_NOPE)))
    wuk = wuk.reshape(MLA_KV_LORA, MLA_HEADS * HEAD_PAD).astype(BF)
    wuv = wukv[:, :, MLA_NOPE:].reshape(MLA_KV_LORA, MLA_HEADS * MLA_DV).astype(BF)

    h, qn, kvn, gn, cq, ckv, kr, qm, gm, qn_t, vs_t, vw_t, gn_t = _ln_inproj(
        x.reshape(n, d), row2(ln0_g), row2(ln0_b), ws, scales, ws_t, tm)

    q_mla, k_mla, v_mla = _mla_prep(cq, ckv, kr, row2(mla_q_norm[l]), row2(mla_kv_norm[l]), wuq, wuk, wuv, seq, tm)
    o_mla = _mla_attn(q_mla, k_mla, v_mla, batch, seq, min(512, seq))

    nc = seq // CMP_STRIDE
    n_slc = seq // SLC_LEN
    cc = kvn[:, :2 * gw].reshape(batch, nc, CMP_STRIDE, 2 * NSA_GROUPS, NSA_DH)
    cc = cc.transpose(0, 3, 1, 2, 4).reshape(batch, 2 * NSA_GROUPS, nc, CMP_STRIDE * NSA_DH)
    pos = jnp.stack([cmp_pos_k[l], cmp_pos_v[l]]).reshape(2, 1, CMP_LEN * NSA_DH)
    w1c = jnp.stack([cmp_k_w1[l], cmp_v_w1[l]]).astype(BF)
    w2c = jnp.stack([cmp_k_w2[l], cmp_v_w2[l]]).astype(BF)
    kvc = _nsa_compress(cc, pos, w1c, w2c)
    overlap = jnp.asarray(_overlap_matrix(nc, n_slc), BF)
    o_cmp, sel_t = _nsa_cmp(rel_bias, qn, kvc, overlap, batch, seq)
    o_nsa = _nsa_attn(rel_bias, qn_t, kvn, vs_t, vw_t, sel_t, gn, gn_t, o_cmp, batch, seq)

    o_mem = _mem_attn(qm, mem.reshape(-1, d), mem_w_kv[l].astype(BF), batch, seq, min(512, seq))

    h1 = _merge(o_nsa, o_mla, o_mem, gm, h, w_branch[l].astype(BF), w_out[l].astype(BF),
                row2(ln1_g[l]), row2(ln1_b[l]), tm)

    rw_t = router_w[l].T
    rw_hi = rw_t.astype(BF)
    rw_lo = (rw_t - rw_hi.astype(F32)).astype(BF)
    eidx_t, w_t, rank_t, counts = _router(h1, rw_hi, rw_lo, router_b[l].reshape(N_EXPERTS, 1), tm)
    pad_start, fill_start, fill_n, blk_e, nused = _block_layout(counts[:, 0], n)
    pos = _positions(eidx_t, rank_t, pad_start.astype(F32).reshape(N_EXPERTS, 1), tm)
    xs = _permute(fill_start, fill_n, pos, h1, blk_e.shape[0] * EXPERT_BLOCK, tm)
    ys = _experts(blk_e, nused, xs, exp_w1[l], exp_w3[l], exp_w2[l])
    out = _combine(pos, ys, w_t.T, h1, sh_w1[l].astype(BF), sh_w3[l].astype(BF), sh_w2[l].astype(BF),
                   row2(ln2_g[l]), row2(ln2_b[l]), min(128, seq))
    return out.reshape(batch, seq, d)
```

```python
import functools
import math

import numpy as np
import jax
import jax.numpy as jnp
from jax import lax
from jax.experimental import pallas as pl
from jax.experimental.pallas import tpu as pltpu

BF = jnp.bfloat16
F32 = jnp.float32
I32 = jnp.int32

D_MODEL = 1024
DEPTH = 1
NSA_HEADS = 8
NSA_GROUPS = 2
NSA_HPG = NSA_HEADS // NSA_GROUPS
NSA_DH = 64
CMP_LEN = 32
CMP_STRIDE = 16
CMP_HID = 256
SLC_LEN = 64
SLC_TOPN = 16
WIN = 512
MLA_HEADS = 8
MLA_NOPE = 64
MLA_ROPE = 32
MLA_DV = 64
MLA_Q_LORA = 768
MLA_KV_LORA = 256
ROPE_THETA = 10000.0
MEM_HEADS = 4
MEM_DH = 128
N_BRANCH = 3
BRANCH_W = NSA_HEADS * NSA_DH
REL_BUCKETS = 32
REL_MAX_DIST = 128
N_EXPERTS = 256
TOP_K = 8
N_EXPERT_GROUPS = 8
TOPK_GROUPS = 4
D_EXPERT = 256
ROUTE_SCALE = 2.5
EXPERT_BLOCK = 256
LN_EPS = 1e-5
RMS_EPS = 1e-6
NEG = -1e30
BIG = 1e30
ALPHA = (2 * DEPTH) ** 0.25
IN_SPLITS = (NSA_HEADS * NSA_DH, 6 * NSA_GROUPS * NSA_DH, 3 * NSA_HEADS, MLA_Q_LORA, MLA_KV_LORA,
             MLA_ROPE, MEM_HEADS * MEM_DH, N_BRANCH * D_MODEL)

LANE = 128
SUBLANES = 8
HEAD_PAD = 128
ROPE_LANE0 = MLA_NOPE
ROPE_HALF = MLA_ROPE // 2
VMEM_LIMIT = 56 * 1024 * 1024
NSA_TQ = 128
FAR_CHUNK = 512


def _cparams(sem):
    return pltpu.CompilerParams(dimension_semantics=sem, vmem_limit_bytes=VMEM_LIMIT)


def _bucket_starts():
    max_exact = REL_BUCKETS // 2
    d = np.arange(0, 4 * REL_MAX_DIST)
    nf = np.maximum(d, 1).astype(np.float32)
    large = max_exact + (np.log(nf / np.float32(max_exact)) / np.float32(math.log(REL_MAX_DIST / max_exact))
                         * np.float32(REL_BUCKETS - max_exact)).astype(np.int32)
    large = np.minimum(large, REL_BUCKETS - 1)
    bucket = np.where(d < max_exact, d, large)
    return [int(np.argmax(bucket >= b)) for b in range(REL_BUCKETS)]


BUCKET_START = _bucket_starts()
FAR_DIST = BUCKET_START[REL_BUCKETS - 1]


def _rel_bias(dist, tab_ref, head):
    val = jnp.full(dist.shape, tab_ref[REL_BUCKETS - 1, head], F32)
    for b in range(REL_BUCKETS - 2, -1, -1):
        val = jnp.where(dist < BUCKET_START[b + 1], tab_ref[b, head], val)
    return val


def _layer_norm(x, g, b):
    mu = jnp.mean(x, -1, keepdims=True)
    xc = x - mu
    var = jnp.mean(xc * xc, -1, keepdims=True)
    return xc * lax.rsqrt(var + LN_EPS) * g + b


def _dot(a, b):
    return jnp.dot(a, b, preferred_element_type=F32)


def _dot_nt(a, b):
    return lax.dot_general(a, b, (((1,), (1,)), ((), ())), preferred_element_type=F32)


def _ln_inproj_kernel(*refs, scales, n_rowmajor):
    x_ref, g_ref, b_ref = refs[:3]
    n_w = (len(refs) - 4) // 2
    w_refs = refs[3:3 + n_w]
    h_ref = refs[3 + n_w]
    o_refs = refs[4 + n_w:]
    h = _layer_norm(x_ref[...], g_ref[...], b_ref[...])
    h_ref[...] = h
    hb = h.astype(BF)
    for j, (w, o) in enumerate(zip(w_refs, o_refs)):
        if j < n_rowmajor:
            y = _dot(hb, w[...])
            o[...] = (y if scales[j] == 1.0 else y * scales[j]).astype(o.dtype)
        else:
            o[...] = _dot_nt(w[...], hb).astype(o.dtype)


def _ln_inproj(x2, g, b, ws, scales, ws_t, tm):
    n = x2.shape[0]
    row = lambda i: (i, 0)
    col = lambda i: (0, i)
    const = lambda i: (0, 0)
    in_specs = [pl.BlockSpec((tm, D_MODEL), row), pl.BlockSpec((1, D_MODEL), const), pl.BlockSpec((1, D_MODEL), const)]
    in_specs += [pl.BlockSpec(w.shape, const) for w in ws + ws_t]
    out_shape = [jax.ShapeDtypeStruct((n, D_MODEL), F32)]
    out_shape += [jax.ShapeDtypeStruct((n, w.shape[1]), BF) for w in ws]
    out_shape += [jax.ShapeDtypeStruct((w.shape[0], n), BF) for w in ws_t]
    out_specs = [pl.BlockSpec((tm, D_MODEL), row)] + [pl.BlockSpec((tm, w.shape[1]), row) for w in ws]
    out_specs += [pl.BlockSpec((w.shape[0], tm), col) for w in ws_t]
    return pl.pallas_call(
        functools.partial(_ln_inproj_kernel, scales=tuple(scales), n_rowmajor=len(ws)),
        grid=(n // tm,), in_specs=in_specs, out_specs=out_specs, out_shape=out_shape,
        compiler_params=_cparams(("parallel",)),
    )(x2, g, b, *ws, *ws_t)


def _rope_lanes(x, c, s1, s2):
    return x * c + pltpu.roll(x, LANE - ROPE_HALF, 1) * s1 + pltpu.roll(x, ROPE_HALF, 1) * s2


def _mla_prep_kernel(cq_ref, ckv_ref, kr_ref, qn_ref, kvn_ref, wuq, wuk, wuv,
                     cq_t, s1q_t, s2q_t, ck_t, s1k_t, s2k_t, q_out, k_out, v_out):
    cq = cq_ref[...].astype(F32)
    rq = cq * lax.rsqrt(jnp.mean(cq * cq, -1, keepdims=True) + RMS_EPS) * qn_ref[...]
    q = _dot(rq.astype(BF), wuq[...])
    ckv = ckv_ref[...].astype(F32)
    rkv = (ckv * lax.rsqrt(jnp.mean(ckv * ckv, -1, keepdims=True) + RMS_EPS) * kvn_ref[...]).astype(BF)
    kn = _dot(rkv, wuk[...])
    v = _dot(rkv, wuv[...])
    pair_w = 2 * MLA_DV
    ones = jnp.ones((v.shape[0], pair_w), v_out.dtype)
    for p in range(MLA_HEADS // 2):
        v_out[:, 2 * p * pair_w:(2 * p + 1) * pair_w] = v[:, p * pair_w:(p + 1) * pair_w].astype(v_out.dtype)
        v_out[:, (2 * p + 1) * pair_w:(2 * p + 2) * pair_w] = ones
    kr = _rope_lanes(kr_ref[...].astype(F32), ck_t[...], s1k_t[...], s2k_t[...])
    cq_c, s1q, s2q = cq_t[...], s1q_t[...], s2q_t[...]
    for h in range(MLA_HEADS):
        sl = slice(h * HEAD_PAD, (h + 1) * HEAD_PAD)
        q_out[:, sl] = _rope_lanes(q[:, sl], cq_c, s1q, s2q).astype(q_out.dtype)
        k_out[:, sl] = (kn[:, sl] + kr).astype(k_out.dtype)


def _rope_tables(seq, scale):
    freq = ROPE_THETA ** (-jnp.arange(ROPE_HALF, dtype=F32) / ROPE_HALF)
    ang = jnp.arange(seq, dtype=F32)[:, None] * freq[None, :]
    cos, sin = jnp.cos(ang) * scale, jnp.sin(ang) * scale
    z = lambda w: jnp.zeros((seq, w), F32)
    tail = HEAD_PAD - ROPE_LANE0 - MLA_ROPE
    c = jnp.concatenate([jnp.full((seq, ROPE_LANE0), scale, F32), cos, cos, z(tail)], 1)
    s1 = jnp.concatenate([z(ROPE_LANE0), -sin, z(ROPE_HALF + tail)], 1)
    s2 = jnp.concatenate([z(ROPE_LANE0 + ROPE_HALF), sin, z(tail)], 1)
    return c, s1, s2


def _mla_prep(cq, ckv, kr, q_norm, kv_norm, wuq, wuk, wuv, seq, tm):
    n = cq.shape[0]
    nt = seq // tm
    row = lambda i: (i, 0)
    const = lambda i: (0, 0)
    pos = lambda i: (i % nt, 0)
    tabs = _rope_tables(seq, (MLA_NOPE + MLA_ROPE) ** -0.5 * math.log2(math.e)) + _rope_tables(seq, 1.0)
    in_specs = [pl.BlockSpec((tm, MLA_Q_LORA), row), pl.BlockSpec((tm, MLA_KV_LORA), row), pl.BlockSpec((tm, LANE), row),
                pl.BlockSpec((1, MLA_Q_LORA), const), pl.BlockSpec((1, MLA_KV_LORA), const),
                pl.BlockSpec(wuq.shape, const), pl.BlockSpec(wuk.shape, const), pl.BlockSpec(wuv.shape, const)]
    in_specs += [pl.BlockSpec((tm, LANE), pos)] * 6
    hq = MLA_HEADS * HEAD_PAD
    hv = 2 * MLA_HEADS * MLA_DV
    return pl.pallas_call(
        _mla_prep_kernel, grid=(n // tm,), in_specs=in_specs,
        out_specs=[pl.BlockSpec((tm, hq), row), pl.BlockSpec((tm, hq), row), pl.BlockSpec((tm, hv), row)],
        out_shape=[jax.ShapeDtypeStruct((n, hq), BF), jax.ShapeDtypeStruct((n, hq), BF), jax.ShapeDtypeStruct((n, hv), BF)],
        compiler_params=_cparams(("parallel",)),
    )(cq, ckv, kr, q_norm, kv_norm, wuq, wuk, wuv, *tabs)


def _mla_attn_kernel(q_ref, k_ref, v_ref, o_ref, sa_scr, sb_scr, m_scr, acc_scr):
    i = pl.program_id(2)
    t = q_ref.shape[0]
    reps = t // LANE
    pair_w = 2 * MLA_DV
    m_scr[...] = jnp.full(m_scr.shape, NEG, F32)
    acc_scr[...] = jnp.zeros(acc_scr.shape, F32)

    def logits(tile, s_scr):
        kstart = pl.multiple_of(tile * t, t)
        for hh in range(2):
            sl = slice(hh * HEAD_PAD, (hh + 1) * HEAD_PAD)
            s_scr[hh] = _dot_nt(q_ref[:, sl], k_ref[pl.ds(kstart, t), sl])

    def consume(tile, s_scr, diagonal):
        v = v_ref[pl.ds(pl.multiple_of(tile * t, t), t), :]
        for hh in range(2):
            s = s_scr[hh]
            if diagonal:
                row = lax.broadcasted_iota(I32, (t, t), 0)
                col = lax.broadcasted_iota(I32, (t, t), 1)
                s = jnp.where(col <= row, s, NEG)
            m_prev = m_scr[hh]
            m_new = jnp.maximum(m_prev, jnp.max(s, 1, keepdims=True))
            a = jnp.exp2(m_prev - m_new)
            e = jnp.exp2(s - jnp.tile(m_new, (1, reps)))
            acc_scr[hh] = jnp.tile(a, (1, 2)) * acc_scr[hh] + _dot(e.astype(BF), v)
            m_scr[hh] = m_new

    logits(0, sa_scr)

    def body(p, carry):
        logits(2 * p + 1, sb_scr)
        consume(2 * p, sa_scr, False)
        logits(2 * p + 2, sa_scr)
        consume(2 * p + 1, sb_scr, False)
        return carry

    lax.fori_loop(0, i // 2, body, 0)

    @pl.when(i % 2 == 0)
    def _():
        consume(i, sa_scr, True)

    @pl.when(i % 2 == 1)
    def _():
        logits(i, sb_scr)
        consume(i - 1, sa_scr, False)
        consume(i, sb_scr, True)

    lane = lax.broadcasted_iota(I32, (t, pair_w), 1)
    o = jnp.where(lane < MLA_DV, acc_scr[0, :, :pair_w] / acc_scr[0, :, pair_w:],
                  acc_scr[1, :, :pair_w] / acc_scr[1, :, pair_w:])
    o_ref[...] = o.astype(o_ref.dtype)


def _mla_attn(q, k, v, batch, seq, t):
    n = q.shape[0]
    nt = seq // t
    qmap = lambda b, hp, i: (b * nt + i, hp)
    kmap = lambda b, hp, i: (b, hp)
    return pl.pallas_call(
        _mla_attn_kernel, grid=(batch, MLA_HEADS // 2, nt),
        in_specs=[pl.BlockSpec((t, 2 * HEAD_PAD), qmap), pl.BlockSpec((seq, 2 * HEAD_PAD), kmap),
                  pl.BlockSpec((seq, 4 * MLA_DV), kmap)],
        out_specs=pl.BlockSpec((t, 2 * MLA_DV), qmap),
        out_shape=jax.ShapeDtypeStruct((n, MLA_HEADS * MLA_DV), BF),
        scratch_shapes=[pltpu.VMEM((2, t, t), F32), pltpu.VMEM((2, t, t), F32),
                        pltpu.VMEM((2, t, LANE), F32), pltpu.VMEM((2, t, 4 * MLA_DV), F32)],
        compiler_params=_cparams(("parallel", "parallel", "arbitrary")),
    )(q, k, v)


def _nsa_compress_kernel(c_ref, pos_ref, w1_ref, w2_ref, o_ref):
    nc = c_ref.shape[0]
    half = CMP_STRIDE * NSA_DH
    c = c_ref[...]
    top = _dot(c, w1_ref[:half, :])
    bot = _dot(c, w1_ref[half:, :])
    posb = _dot(jnp.broadcast_to(pos_ref[...], (8, 2 * half)).astype(BF), w1_ref[...])[:1]
    hid = top + pltpu.roll(bot, nc - 1, 0) + posb
    o_ref[...] = _dot(jax.nn.gelu(hid).astype(BF), w2_ref[...]).astype(o_ref.dtype)


def _nsa_compress(cc, pos, w1, w2):
    b, _, nc, half = cc.shape
    sq = pl.Squeezed()
    return pl.pallas_call(
        _nsa_compress_kernel, grid=(b, 2 * NSA_GROUPS),
        in_specs=[pl.BlockSpec((sq, sq, nc, half), lambda i, c: (i, c, 0, 0)),
                  pl.BlockSpec((sq, 1, 2 * half), lambda i, c: (c // NSA_GROUPS, 0, 0)),
                  pl.BlockSpec((sq, 2 * half, CMP_HID), lambda i, c: (c // NSA_GROUPS, 0, 0)),
                  pl.BlockSpec((sq, CMP_HID, NSA_DH), lambda i, c: (c // NSA_GROUPS, 0, 0))],
        out_specs=pl.BlockSpec((sq, sq, nc, NSA_DH), lambda i, c: (i, c, 0, 0)),
        out_shape=jax.ShapeDtypeStruct((b, 2 * NSA_GROUPS, nc, NSA_DH), BF),
        compiler_params=_cparams(("parallel", "parallel")),
    )(cc, pos, w1, w2)


CMP_TQ = 512
CMP_BIAS_COLS = LANE // 2
CMP_BIAS_BACK = -(-(FAR_DIST + CMP_LEN - 1) // CMP_STRIDE)
assert (CMP_TQ - CMP_LEN) // CMP_STRIDE + CMP_BIAS_BACK < CMP_BIAS_COLS


def _nsa_cmp_kernel(tab_ref, q_ref, kc_ref, vc_ref, ov_ref, oc_ref, sel_ref, e_scr, *, n_top):
    b, g, i = pl.program_id(0), pl.program_id(1), pl.program_id(2)
    tq = q_ref.shape[0]
    nc = kc_ref.shape[0]
    n_slc = ov_ref.shape[1]
    qs = i * tq

    @pl.when((b == 0) & (g == 0) & (i == 0))
    def _():
        q_io = lax.broadcasted_iota(I32, (tq, LANE), 0)
        lane = lax.broadcasted_iota(I32, (tq, LANE), 1)
        jj = lane % CMP_BIAS_COLS
        dist = q_io - CMP_STRIDE * (jj - CMP_BIAS_BACK) - (CMP_LEN - 1)
        live = (dist >= 0) & (lane < 2 * CMP_BIAS_COLS)
        for h in range(NSA_HEADS):
            e = jnp.where(live, _rel_bias(dist, tab_ref, h) - tab_ref[REL_BUCKETS - 1, h], 0.0)
            hi = e.astype(BF)
            lo = (e - hi.astype(F32)).astype(BF)
            e_scr[h] = jnp.where(lane < CMP_BIAS_COLS, hi, lo)

    n0 = qs // CMP_STRIDE
    jrow = lax.broadcasted_iota(I32, (LANE, nc), 0)
    ncol = lax.broadcasted_iota(I32, (LANE, nc), 1)
    ft = jnp.where((ncol == n0 + (jrow % CMP_BIAS_COLS) - CMP_BIAS_BACK) & (jrow < 2 * CMP_BIAS_COLS), 1.0, 0.0).astype(BF)

    t = qs + lax.broadcasted_iota(I32, (tq, nc), 0)
    n_io = lax.broadcasted_iota(I32, (tq, nc), 1)
    mask = (t >= n_io * CMP_STRIDE + (CMP_LEN - 1)) & (n_io < nc - 1)
    kc = kc_ref[...]
    vc = vc_ref[...]
    psum = jnp.zeros((tq, nc), F32)
    for h in range(NSA_HPG):
        head = g * NSA_HPG + h
        qh = q_ref[:, h * NSA_DH:(h + 1) * NSA_DH]
        s = _dot_nt(qh, kc) + _dot(e_scr[head], ft)
        s = jnp.where(mask, s, NEG)
        m = jnp.max(s, 1, keepdims=True)
        e = jnp.where(mask, jnp.exp(s - m), 0.0)
        l = jnp.sum(e, 1, keepdims=True)
        p = e * jnp.where(l > 0.0, 1.0 / l, 0.0)
        oc_ref[:, h * NSA_DH:(h + 1) * NSA_DH] = _dot(p.astype(BF), vc).astype(oc_ref.dtype)
        psum = psum + p

    ov = ov_ref[...]
    p0 = psum.astype(BF)
    r1 = psum - p0.astype(F32)
    p1 = r1.astype(BF)
    p2 = (r1 - p1.astype(F32)).astype(BF)
    imp = (_dot(p0, ov) + _dot(p1, ov) + _dot(p2, ov)).T

    tj = qs + lax.broadcasted_iota(I32, (n_slc, tq), 1)
    j = lax.broadcasted_iota(I32, (n_slc, tq), 0)
    cur = tj // SLC_LEN
    forced = (j == 0) | (j == cur) | (j == cur - 1)
    work = jnp.where(j * SLC_LEN > tj, NEG, jnp.where(forced, BIG, imp))
    sel = jnp.full((n_slc, tq), NEG, F32)
    for _ in range(n_top):
        mx, first = _first_max(work, j, n_slc)
        hit = j == first
        sel = jnp.where(hit & (mx > 0.5 * NEG), 0.0, sel)
        work = jnp.where(hit, -jnp.inf, work)
    sel_ref[...] = sel


def _nsa_cmp(tab, qn, kvc, overlap, batch, seq):
    n = qn.shape[0]
    tq = min(CMP_TQ, seq)
    nq = seq // tq
    nc = kvc.shape[2]
    n_slc = overlap.shape[1]
    sq = pl.Squeezed()
    gw = NSA_HPG * NSA_DH
    return pl.pallas_call(
        functools.partial(_nsa_cmp_kernel, n_top=min(SLC_TOPN, n_slc)), grid=(batch, NSA_GROUPS, nq),
        in_specs=[pl.BlockSpec(memory_space=pltpu.SMEM),
                  pl.BlockSpec((tq, gw), lambda b, g, i: (b * nq + i, g)),
                  pl.BlockSpec((sq, sq, nc, NSA_DH), lambda b, g, i: (b, g, 0, 0)),
                  pl.BlockSpec((sq, sq, nc, NSA_DH), lambda b, g, i: (b, NSA_GROUPS + g, 0, 0)),
                  pl.BlockSpec((nc, n_slc), lambda b, g, i: (0, 0))],
        out_specs=[pl.BlockSpec((tq, gw), lambda b, g, i: (b * nq + i, g)),
                   pl.BlockSpec((sq, n_slc, tq), lambda b, g, i: (g, 0, b * nq + i))],
        out_shape=[jax.ShapeDtypeStruct((n, NSA_HEADS * NSA_DH), BF),
                   jax.ShapeDtypeStruct((NSA_GROUPS, n_slc, n), F32)],
        scratch_shapes=[pltpu.VMEM((NSA_HEADS, tq, LANE), BF)],
        compiler_params=_cparams(("arbitrary", "arbitrary", "arbitrary")),
    )(tab, qn, kvc, kvc, overlap)


LOG2E = math.log2(math.e)
DEN_ROWS = 16


def _nsa_attn_kernel(tab_ref, qt_ref, ks_ref, kw_ref, vst_ref, vwt_ref, sel_ref, gate_ref, gate_t_ref, oc_ref, o_ref,
                     d_scr, q_scr, sa_scr, sb_scr, m_scr, acc_scr):
    b, i = pl.program_id(0), pl.program_id(1)
    tq = NSA_TQ
    lanes = NSA_HPG * tq
    qs = i * tq
    near_w = 2 * tq
    far_w = FAR_CHUNK
    win_far_w = WIN - tq

    @pl.when((b == 0) & (i == 0))
    def _():
        kk = lax.broadcasted_iota(I32, (near_w, tq), 0)
        q_io = lax.broadcasted_iota(I32, (near_w, tq), 1)
        dist = jnp.maximum(q_io + tq - kk, 0)
        for g in range(NSA_GROUPS):
            d_scr[g] = jnp.concatenate(
                [(_rel_bias(dist, tab_ref, g * NSA_HPG + h) - tab_ref[REL_BUCKETS - 1, g * NSA_HPG + h]) * LOG2E
                 for h in range(NSA_HPG)], axis=1)

    qt = qt_ref[...]
    for g in range(NSA_GROUPS):
        qg = jnp.concatenate([qt[(g * NSA_HPG + h) * NSA_DH:(g * NSA_HPG + h + 1) * NSA_DH, :]
                              for h in range(NSA_HPG)], axis=1)
        parts = [jnp.zeros((NSA_DH, lanes), BF)] * NSA_GROUPS
        parts[g] = qg
        q_scr[g] = jnp.concatenate(parts, axis=0)

    def update(slot, s, vt):
        m_prev = m_scr[slot]
        m_new = jnp.maximum(m_prev, jnp.max(s, 0, keepdims=True))
        a = jnp.exp2(m_prev - m_new)
        e = jnp.exp2(s - m_new)
        vt_den = jnp.concatenate([vt, jnp.ones((DEN_ROWS, vt.shape[1]), BF)], axis=0)
        acc_scr[slot] = a * acc_scr[slot] + _dot(vt_den, e.astype(BF))
        m_scr[slot] = m_new

    def all_heads(x):
        return jnp.tile(x, (1, NSA_HPG))

    def sel_rows(g, first_block, n_blk):
        return jnp.concatenate(
            [jnp.broadcast_to(sel_ref[g, pl.ds(jnp.maximum(first_block + r, 0), 1), :], (SLC_LEN, tq))
             for r in range(n_blk)], axis=0)

    m_scr[...] = jnp.full(m_scr.shape, NEG, F32)
    acc_scr[...] = jnp.zeros(acc_scr.shape, F32)

    kk = lax.broadcasted_iota(I32, (near_w, tq), 0)
    q_io = lax.broadcasted_iota(I32, (near_w, tq), 1)
    kpos_near = qs - tq + kk
    causal_add = jnp.where((kpos_near >= 0) & (kpos_near <= qs + q_io), 0.0, NEG)
    start_a = pl.multiple_of(qs, tq)
    start_b = pl.multiple_of(jnp.maximum(qs - tq, 0), tq)
    for g in range(NSA_GROUPS):
        gs = slice(g * NSA_DH, (g + 1) * NSA_DH)
        for br, (k_ref, vt_ref) in enumerate(((ks_ref, vst_ref), (kw_ref, vwt_ref))):
            k = jnp.concatenate([k_ref[pl.ds(start_b, tq), :], k_ref[pl.ds(start_a, tq), :]], axis=0)
            vt = jnp.concatenate([vt_ref[gs, pl.ds(start_b, tq)], vt_ref[gs, pl.ds(start_a, tq)]], axis=1)
            add = causal_add + sel_rows(g, 2 * i - 2, near_w // SLC_LEN) if br == 0 else causal_add
            update(2 * g + br, _dot(k, q_scr[g]) + d_scr[g] + all_heads(add), vt)

    ws = pl.multiple_of(jnp.maximum(qs - WIN, 0), tq)
    kpos_w = ws + lax.broadcasted_iota(I32, (win_far_w, tq), 0)
    t_w = qs + lax.broadcasted_iota(I32, (win_far_w, tq), 1)
    add_w = all_heads(jnp.where((kpos_w < qs - tq) & (kpos_w > t_w - WIN), 0.0, NEG))
    k_w = kw_ref[pl.ds(ws, win_far_w), :]
    for g in range(NSA_GROUPS):
        gs = slice(g * NSA_DH, (g + 1) * NSA_DH)
        update(2 * g + 1, _dot(k_w, q_scr[g]) + add_w, vwt_ref[gs, pl.ds(ws, win_far_w)])

    n_far = (jnp.maximum(i - 1, 0) + (far_w // tq - 1)) // (far_w // tq)
    last_chunk = ks_ref.shape[0] // far_w - 1
    krow = lax.broadcasted_iota(I32, (far_w, tq), 0)

    def far_logits(c, s_scr):
        c = jnp.minimum(c, last_chunk)
        base = pl.multiple_of(c * far_w, far_w)
        in_range = jnp.where(base + krow < qs - tq, 0.0, NEG)
        k = ks_ref[pl.ds(base, far_w), :]
        for g in range(NSA_GROUPS):
            add = all_heads(sel_rows(g, c * (far_w // SLC_LEN), far_w // SLC_LEN) + in_range)
            s_scr[g] = _dot(k, q_scr[g]) + add

    def far_consume(c, s_scr):
        base = pl.multiple_of(c * far_w, far_w)
        for g in range(NSA_GROUPS):
            update(2 * g, s_scr[g], vst_ref[g * NSA_DH:(g + 1) * NSA_DH, pl.ds(base, far_w)])

    far_logits(0, sa_scr)

    def far_body(p, carry):
        far_logits(2 * p + 1, sb_scr)
        far_consume(2 * p, sa_scr)
        far_logits(2 * p + 2, sa_scr)
        far_consume(2 * p + 1, sb_scr)
        return carry

    lax.fori_loop(0, n_far // 2, far_body, 0)

    @pl.when(n_far % 2 == 1)
    def _():
        far_consume(n_far - 1, sa_scr)

    gates = jax.nn.sigmoid(gate_ref[...].astype(F32))
    gates_t = jax.nn.sigmoid(gate_t_ref[...].astype(F32))
    for g in range(NSA_GROUPS):
        o_s = acc_scr[2 * g, :NSA_DH] / acc_scr[2 * g, NSA_DH:NSA_DH + 1]
        o_w = acc_scr[2 * g + 1, :NSA_DH] / acc_scr[2 * g + 1, NSA_DH:NSA_DH + 1]
        for h in range(NSA_HPG):
            head = g * NSA_HPG + h
            hl = slice(head * NSA_DH, (head + 1) * NSA_DH)
            cl = slice(h * tq, (h + 1) * tq)
            sw_t = (gates_t[3 * head + 1:3 * head + 2, :] * o_s[:, cl]
                    + gates_t[3 * head + 2:3 * head + 3, :] * o_w[:, cl])
            o = gates[:, 3 * head:3 * head + 1] * oc_ref[:, hl].astype(F32) + sw_t.T
            o_ref[:, hl] = o.astype(o_ref.dtype)


def _nsa_attn(tab, qn_t, kvn, vs_t, vw_t, sel_t, gn, gn_t, oc, batch, seq):
    n = kvn.shape[0]
    tq = NSA_TQ
    nq = seq // tq
    n_slc = sel_t.shape[1]
    hw = NSA_HEADS * NSA_DH
    gw = NSA_GROUPS * NSA_DH
    lanes = NSA_HPG * tq
    row = lambda b, i: (b * nq + i, 0)
    col = lambda b, i: (0, b * nq + i)
    k_spec = lambda kind: pl.BlockSpec((seq, gw), lambda b, i: (b, kind))
    vt_spec = pl.BlockSpec((gw, seq), lambda b, i: (0, b))
    n_state = 2 * NSA_GROUPS
    return pl.pallas_call(
        _nsa_attn_kernel, grid=(batch, nq),
        in_specs=[pl.BlockSpec(memory_space=pltpu.SMEM), pl.BlockSpec((hw, tq), col),
                  k_spec(2), k_spec(4), vt_spec, vt_spec,
                  pl.BlockSpec((NSA_GROUPS, n_slc, tq), lambda b, i: (0, 0, b * nq + i)),
                  pl.BlockSpec((tq, LANE), row), pl.BlockSpec((LANE, tq), col), pl.BlockSpec((tq, hw), row)],
        out_specs=pl.BlockSpec((tq, hw), row),
        out_shape=jax.ShapeDtypeStruct((n, hw), BF),
        scratch_shapes=[pltpu.VMEM((NSA_GROUPS, 2 * tq, lanes), F32), pltpu.VMEM((NSA_GROUPS, gw, lanes), BF),
                        pltpu.VMEM((NSA_GROUPS, FAR_CHUNK, lanes), F32), pltpu.VMEM((NSA_GROUPS, FAR_CHUNK, lanes), F32),
                        pltpu.VMEM((n_state, 1, lanes), F32), pltpu.VMEM((n_state, NSA_DH + DEN_ROWS, lanes), F32)],
        compiler_params=_cparams(("arbitrary", "arbitrary")),
    )(tab, qn_t, kvn, kvn, vs_t, vw_t, sel_t, gn, gn_t, oc)


def _mem_attn_kernel(q_ref, mem_ref, w_ref, o_ref, kv_scr):
    @pl.when(pl.program_id(1) == 0)
    def _():
        kv_scr[...] = _dot(mem_ref[...].astype(BF), w_ref[...]).astype(BF)

    hw = MEM_HEADS * MEM_DH
    for h in range(MEM_HEADS):
        sl = slice(h * MEM_DH, (h + 1) * MEM_DH)
        s = _dot_nt(q_ref[:, sl], kv_scr[:, sl])
        e = jnp.exp(s - jnp.max(s, 1, keepdims=True))
        p = e / jnp.sum(e, 1, keepdims=True)
        o_ref[:, sl] = _dot(p.astype(BF), kv_scr[:, hw + h * MEM_DH:hw + (h + 1) * MEM_DH]).astype(o_ref.dtype)


def _mem_attn(qm, mem2, w_kv, batch, seq, tq):
    n = qm.shape[0]
    nq = seq // tq
    m = mem2.shape[0] // batch
    hw = MEM_HEADS * MEM_DH
    return pl.pallas_call(
        _mem_attn_kernel, grid=(batch, nq),
        in_specs=[pl.BlockSpec((tq, hw), lambda b, i: (b * nq + i, 0)),
                  pl.BlockSpec((m, D_MODEL), lambda b, i: (b, 0)),
                  pl.BlockSpec((D_MODEL, 2 * hw), lambda b, i: (0, 0))],
        out_specs=pl.BlockSpec((tq, hw), lambda b, i: (b * nq + i, 0)),
        out_shape=jax.ShapeDtypeStruct((n, hw), BF),
        scratch_shapes=[pltpu.VMEM((m, 2 * hw), BF)],
        compiler_params=_cparams(("arbitrary", "arbitrary")),
    )(qm, mem2, w_kv)


def _merge_kernel(on_ref, ol_ref, om_ref, gm_ref, h_ref, wb_ref, wo_ref, g_ref, b_ref, h1_ref):
    merged = None
    for c, o_ref in enumerate((on_ref, ol_ref, om_ref)):
        gate = jax.nn.sigmoid(gm_ref[:, c * D_MODEL:(c + 1) * D_MODEL].astype(F32))
        term = gate * _dot(o_ref[...], wb_ref[c])
        merged = term if merged is None else merged + term
    y = ALPHA * h_ref[...] + _dot(merged.astype(BF), wo_ref[...])
    h1_ref[...] = _layer_norm(y, g_ref[...], b_ref[...])


def _merge(o_nsa, o_mla, o_mem, gm, h, wb, wo, g, b, tm):
    n = h.shape[0]
    row = lambda i: (i, 0)
    const = lambda i: (0, 0)
    return pl.pallas_call(
        _merge_kernel, grid=(n // tm,),
        in_specs=[pl.BlockSpec((tm, BRANCH_W), row)] * 3 + [
            pl.BlockSpec((tm, N_BRANCH * D_MODEL), row), pl.BlockSpec((tm, D_MODEL), row),
            pl.BlockSpec(wb.shape, lambda i: (0, 0, 0)), pl.BlockSpec(wo.shape, const),
            pl.BlockSpec((1, D_MODEL), const), pl.BlockSpec((1, D_MODEL), const)],
        out_specs=pl.BlockSpec((tm, D_MODEL), row),
        out_shape=jax.ShapeDtypeStruct((n, D_MODEL), F32),
        compiler_params=_cparams(("parallel",)),
    )(o_nsa, o_mla, o_mem, gm, h, wb, wo, g, b)


def _first_max(vals, idx, limit):
    mx = jnp.max(vals, 0, keepdims=True)
    first = jnp.min(jnp.where(vals == mx, idx, limit), 0, keepdims=True)
    return mx, first


def _router_kernel(h_ref, whi_ref, wlo_ref, b_ref, tri_ref, idx_ref, w_ref, rank_ref, cnt_ref):
    @pl.when(pl.program_id(0) == 0)
    def _():
        cnt_ref[...] = jnp.zeros(cnt_ref.shape, F32)

    h = h_ref[...]
    hhi = h.astype(BF)
    hlo = (h - hhi.astype(F32)).astype(BF)
    whi = whi_ref[...]
    logits = _dot_nt(whi, hhi) + _dot_nt(whi, hlo) + _dot_nt(wlo_ref[...], hhi)
    s = jax.nn.sigmoid(logits)
    sb = s + b_ref[...]
    tm = s.shape[1]
    gsz = N_EXPERTS // N_EXPERT_GROUPS
    e_io = lax.broadcasted_iota(I32, (gsz, tm), 0)
    scores = []
    for g in range(N_EXPERT_GROUPS):
        vals = sb[g * gsz:(g + 1) * gsz]
        m1, first = _first_max(vals, e_io, gsz)
        m2 = jnp.max(jnp.where(e_io == first, -jnp.inf, vals), 0, keepdims=True)
        scores.append(m1 + m2)
    gs = jnp.concatenate(scores, axis=0)
    g_io = lax.broadcasted_iota(I32, (N_EXPERT_GROUPS, tm), 0)
    x_io = lax.broadcasted_iota(I32, (N_EXPERTS, tm), 0)
    allowed = jnp.zeros((N_EXPERTS, tm), jnp.bool_)
    for _ in range(TOPK_GROUPS):
        _, first = _first_max(gs, g_io, N_EXPERT_GROUPS)
        gs = jnp.where(g_io == first, -jnp.inf, gs)
        allowed = allowed | (x_io // gsz == first)
    work = jnp.where(allowed, sb, NEG)
    base = cnt_ref[:, :1]
    tri = tri_ref[...]
    idxs, ws, ranks = [], [], []
    for _ in range(TOP_K):
        _, first = _first_max(work, x_io, N_EXPERTS)
        hit = x_io == first
        idxs.append(first)
        ws.append(jnp.sum(jnp.where(hit, s, 0.0), 0, keepdims=True))
        work = jnp.where(hit, -jnp.inf, work)
        onehot = jnp.where(hit, 1.0, 0.0)
        before = _dot(onehot.astype(BF), tri)
        ranks.append(jnp.sum(jnp.where(hit, base + before, 0.0), 0, keepdims=True))
        base = base + jnp.sum(onehot, 1, keepdims=True)
    wsel = jnp.concatenate(ws, axis=0)
    idx_ref[...] = jnp.concatenate(idxs, axis=0)
    w_ref[...] = wsel / jnp.sum(wsel, 0, keepdims=True) * ROUTE_SCALE
    rank_ref[...] = jnp.concatenate(ranks, axis=0).astype(I32)
    cnt_ref[...] = jnp.broadcast_to(base, cnt_ref.shape)


def _router(h1, whi, wlo, rb, tm):
    n = h1.shape[0]
    tri = jnp.asarray(np.triu(np.ones((tm, tm), np.float32), 1), BF)
    slot = pl.BlockSpec((TOP_K, tm), lambda i: (0, i))
    const = lambda i: (0, 0)
    return pl.pallas_call(
        _router_kernel, grid=(n // tm,),
        in_specs=[pl.BlockSpec((tm, D_MODEL), lambda i: (i, 0)),
                  pl.BlockSpec((N_EXPERTS, D_MODEL), const), pl.BlockSpec((N_EXPERTS, D_MODEL), const),
                  pl.BlockSpec((N_EXPERTS, 1), const), pl.BlockSpec((tm, tm), const)],
        out_specs=[slot, slot, slot, pl.BlockSpec((N_EXPERTS, LANE), const)],
        out_shape=[jax.ShapeDtypeStruct((TOP_K, n), I32), jax.ShapeDtypeStruct((TOP_K, n), F32),
                   jax.ShapeDtypeStruct((TOP_K, n), I32), jax.ShapeDtypeStruct((N_EXPERTS, LANE), F32)],
        compiler_params=_cparams(("arbitrary",)),
    )(h1, whi, wlo, rb, tri)


def _pos_kernel(idx_ref, rank_ref, start_ref, pos_ref):
    tm = idx_ref.shape[1]
    x_io = lax.broadcasted_iota(I32, (N_EXPERTS, tm), 0)
    start = start_ref[...]
    rows = [jnp.sum(jnp.where(x_io == idx_ref[k:k + 1, :], start, 0.0), 0, keepdims=True) for k in range(TOP_K)]
    pos_ref[...] = jnp.concatenate(rows, axis=0).astype(I32) + rank_ref[...]


def _positions(eidx_t, rank_t, pad_start, tm):
    n = eidx_t.shape[1]
    slot = pl.BlockSpec((TOP_K, tm), lambda i: (0, i))
    return pl.pallas_call(
        _pos_kernel, grid=(n // tm,),
        in_specs=[slot, slot, pl.BlockSpec((N_EXPERTS, 1), lambda i: (0, 0))],
        out_specs=slot, out_shape=jax.ShapeDtypeStruct((TOP_K, n), I32),
        compiler_params=_cparams(("parallel",)),
    )(eidx_t, rank_t, pad_start)


ROW_UNROLL = 4


def _fill_groups():
    p = EXPERT_BLOCK // 2
    while p >= 1:
        yield p
        p //= 2


def _permute_kernel(fill_start_ref, fill_n_ref, pos_ref, h_ref, xs_hbm, zbuf, row_sem, fill_sem, *, fills_per_step):
    i = pl.program_id(0)
    tm = h_ref.shape[0]
    n_fills = fill_n_ref.shape[0]
    zbuf[...] = jnp.zeros(zbuf.shape, zbuf.dtype)

    def fill_copies(e):
        n = fill_n_ref[e]
        start = fill_start_ref[e]
        for p in _fill_groups():
            @pl.when((n & p) != 0)
            def _(p=p):
                off = start + (n & (p - 1))
                if p < SUBLANES:
                    for r in range(p):
                        pltpu.make_async_copy(zbuf.at[pl.ds(r, 1)], xs_hbm.at[pl.ds(off + r, 1)], fill_sem).start()
                else:
                    pltpu.make_async_copy(zbuf.at[pl.ds(0, p)], xs_hbm.at[pl.ds(pl.multiple_of(off, SUBLANES), p)],
                                          fill_sem).start()

    def fill_waits(e):
        n = fill_n_ref[e]
        for p in _fill_groups():
            @pl.when((n & p) != 0)
            def _(p=p):
                pltpu.make_async_copy(zbuf.at[pl.ds(0, p)], zbuf.at[pl.ds(0, p)], fill_sem).wait()

    for q in range(fills_per_step):
        e = i * fills_per_step + q

        @pl.when(e < n_fills)
        def _(e=e):
            fill_copies(e)

    def body(t, c):
        for k in range(TOP_K):
            pltpu.make_async_copy(h_ref.at[pl.ds(t, 1)], xs_hbm.at[pl.ds(pos_ref[k, t], 1)], row_sem).start()
        return c

    lax.fori_loop(0, tm, body, 0, unroll=ROW_UNROLL)

    for q in range(fills_per_step):
        e = i * fills_per_step + q

        @pl.when(e < n_fills)
        def _(e=e):
            fill_waits(e)

    pltpu.make_async_copy(xs_hbm.at[pl.ds(0, TOP_K * tm)], xs_hbm.at[pl.ds(0, TOP_K * tm)], row_sem).wait()


def _permute(fill_start, fill_n, pos, h1, n_rows, tm):
    n = h1.shape[0]
    nt = n // tm
    grid_spec = pltpu.PrefetchScalarGridSpec(
        num_scalar_prefetch=2, grid=(nt,),
        in_specs=[pl.BlockSpec((TOP_K, tm), lambda i, fs, fn: (0, i), memory_space=pltpu.SMEM),
                  pl.BlockSpec((tm, D_MODEL), lambda i, fs, fn: (i, 0))],
        out_specs=pl.BlockSpec(memory_space=pl.ANY),
        scratch_shapes=[pltpu.VMEM((EXPERT_BLOCK // 2, D_MODEL), F32), pltpu.SemaphoreType.DMA(()),
                        pltpu.SemaphoreType.DMA(())])
    return pl.pallas_call(
        functools.partial(_permute_kernel, fills_per_step=-(-fill_n.shape[0] // nt)), grid_spec=grid_spec,
        out_shape=jax.ShapeDtypeStruct((n_rows, D_MODEL), F32),
        compiler_params=_cparams(("arbitrary",)),
    )(fill_start, fill_n, pos, h1)


def _experts_kernel(blk_e_ref, nused_ref, x_ref, w1_ref, w3_ref, w2_ref, y_ref, w1b, w3b, w2b):
    j = pl.program_id(0)

    @pl.when(j < nused_ref[0])
    def _():
        first_of_expert = (j == 0) | (blk_e_ref[j] != blk_e_ref[jnp.maximum(j - 1, 0)])

        @pl.when(first_of_expert)
        def _():
            w1b[...] = w1_ref[...].astype(BF)
            w3b[...] = w3_ref[...].astype(BF)
            w2b[...] = w2_ref[...].astype(BF)

        x = x_ref[...].astype(BF)
        a = _dot(x, w1b[...])
        y_ref[...] = _dot((a * jax.nn.sigmoid(a) * _dot(x, w3b[...])).astype(BF), w2b[...])

    @pl.when(j >= nused_ref[0])
    def _():
        y_ref[...] = jnp.zeros(y_ref.shape, y_ref.dtype)


def _experts(blk_e, nused, xs, w1, w3, w2):
    n_blocks = blk_e.shape[0]
    rb = EXPERT_BLOCK
    sq = pl.Squeezed()
    wmap = lambda j, be, nu: (be[j], 0, 0)
    grid_spec = pltpu.PrefetchScalarGridSpec(
        num_scalar_prefetch=2, grid=(n_blocks,),
        in_specs=[pl.BlockSpec((rb, D_MODEL), lambda j, be, nu: (jnp.minimum(j, nu[0] - 1), 0)),
                  pl.BlockSpec((sq, D_MODEL, D_EXPERT), wmap), pl.BlockSpec((sq, D_MODEL, D_EXPERT), wmap),
                  pl.BlockSpec((sq, D_EXPERT, D_MODEL), wmap)],
        out_specs=pl.BlockSpec((rb, D_MODEL), lambda j, be, nu: (j, 0)),
        scratch_shapes=[pltpu.VMEM((D_MODEL, D_EXPERT), BF), pltpu.VMEM((D_MODEL, D_EXPERT), BF),
                        pltpu.VMEM((D_EXPERT, D_MODEL), BF)])
    return pl.pallas_call(
        _experts_kernel, grid_spec=grid_spec,
        out_shape=jax.ShapeDtypeStruct((n_blocks * rb, D_MODEL), F32),
        compiler_params=_cparams(("arbitrary",)),
    )(blk_e, nused, xs, w1, w3, w2)


def _combine_kernel(pos_ref, pos_next_ref, ys_hbm, w_ref, h_ref, s1_ref, s3_ref, s2_ref, g_ref, b_ref, o_ref, ybuf, sem):
    i = pl.program_id(0)
    nt = pl.num_programs(0)
    tm = h_ref.shape[0]

    def start_gather(p_ref, slot):
        def body(t, c):
            for k in range(TOP_K):
                pltpu.make_async_copy(ys_hbm.at[pl.ds(p_ref[k, t], 1)], ybuf.at[slot, k, pl.ds(t, 1)], sem.at[slot]).start()
            return c

        lax.fori_loop(0, tm, body, 0, unroll=ROW_UNROLL)

    @pl.when(i == 0)
    def _():
        start_gather(pos_ref, 0)

    slot = i % 2

    @pl.when(i + 1 < nt)
    def _():
        start_gather(pos_next_ref, 1 - slot)

    h = h_ref[...]
    hb = h.astype(BF)
    a = _dot(hb, s1_ref[...])
    y = ALPHA * h + _dot((a * jax.nn.sigmoid(a) * _dot(hb, s3_ref[...])).astype(BF), s2_ref[...])
    pltpu.make_async_copy(ybuf.at[slot], ybuf.at[slot], sem.at[slot]).wait()
    w = w_ref[...]
    for k in range(TOP_K):
        y = y + w[:, k:k + 1] * ybuf[slot, k]
    o_ref[...] = _layer_norm(y, g_ref[...], b_ref[...])


def _combine(pos, ys, w, h1, s1, s3, s2, g, b, tm):
    n = h1.shape[0]
    nt = n // tm
    row = lambda i: (i, 0)
    const = lambda i: (0, 0)
    return pl.pallas_call(
        _combine_kernel, grid=(nt,),
        in_specs=[pl.BlockSpec((TOP_K, tm), lambda i: (0, i), memory_space=pltpu.SMEM),
                  pl.BlockSpec((TOP_K, tm), lambda i: (0, jnp.minimum(i + 1, nt - 1)), memory_space=pltpu.SMEM),
                  pl.BlockSpec(memory_space=pl.ANY),
                  pl.BlockSpec((tm, TOP_K), row), pl.BlockSpec((tm, D_MODEL), row),
                  pl.BlockSpec(s1.shape, const), pl.BlockSpec(s3.shape, const), pl.BlockSpec(s2.shape, const),
                  pl.BlockSpec((1, D_MODEL), const), pl.BlockSpec((1, D_MODEL), const)],
        out_specs=pl.BlockSpec((tm, D_MODEL), row),
        out_shape=jax.ShapeDtypeStruct((n, D_MODEL), F32),
        scratch_shapes=[pltpu.VMEM((2, TOP_K, tm, D_MODEL), F32), pltpu.SemaphoreType.DMA((2,))],
        compiler_params=_cparams(("arbitrary",)),
    )(pos, pos, ys, w, h1, s1, s3, s2, g, b)


def _overlap_matrix(nc, n_slc):
    cs = np.arange(nc) * CMP_STRIDE
    ce = cs + CMP_LEN - 1
    js = np.arange(n_slc) * SLC_LEN
    je = js + SLC_LEN - 1
    ov = ((cs[:, None] <= je[None, :]) & (ce[:, None] >= js[None, :])).astype(np.float32)
    ov[nc - 1] = 0.0
    return ov


def _block_layout(counts, n):
    rb = EXPERT_BLOCK
    counts = counts.astype(I32)
    padded = (counts + rb - 1) // rb * rb
    pad_end = jnp.cumsum(padded)
    pad_start = pad_end - padded
    n_blocks = -(-TOP_K * n // rb) + N_EXPERTS
    blk_start = jnp.arange(n_blocks, dtype=I32) * rb
    blk_e = jnp.minimum(jnp.sum(pad_end[None, :] <= blk_start[:, None], axis=1), N_EXPERTS - 1).astype(I32)
    nused = (pad_end[-1] // rb).astype(I32).reshape(1)
    half = rb // 2
    tail_start = pad_end[-1] + half * jnp.arange(2 * N_EXPERTS, dtype=I32)
    tail_n = jnp.where(tail_start < n_blocks * rb, half, 0).astype(I32)
    fill_start = jnp.concatenate([pad_start + counts, jnp.minimum(tail_start, n_blocks * rb - half)])
    fill_n = jnp.concatenate([padded - counts, tail_n])
    return pad_start, fill_start, fill_n, blk_e, nused


def kernel(x, mem, ln0_g, ln0_b, rel_bias, w_in, cmp_pos_k, cmp_pos_v, cmp_k_w1, cmp_k_w2, cmp_v_w1, cmp_v_w2, mla_q_norm, mla_w_uq, mla_kv_norm, mla_w_ukv, mem_w_kv, w_branch, w_out, ln1_g, ln1_b, router_w, router_b, exp_w1, exp_w3, exp_w2, sh_w1, sh_w3, sh_w2, ln2_g, ln2_b):
    batch, seq, d = x.shape
    n = batch * seq
    l = 0
    row2 = lambda v: v.reshape(1, -1)
    tm = min(256, seq)

    pts = np.cumsum((0,) + IN_SPLITS)
    wcol = lambda k: w_in[l][:, pts[k]:pts[k + 1]]
    pad_cols = lambda w, lo, tot: jnp.pad(w, ((0, 0), (lo, tot - lo - w.shape[1])))
    ws = [wcol(0) * (NSA_DH ** -0.5), wcol(1), pad_cols(wcol(2), 0, LANE), wcol(3), wcol(4),
          pad_cols(wcol(5), ROPE_LANE0, LANE), wcol(6), wcol(7)]
    ws = [w.astype(BF) for w in ws]
    scales = [1.0] * 6 + [MEM_DH ** -0.5, 1.0]
    gw = NSA_GROUPS * NSA_DH
    ws_t = [wcol(0) * (NSA_DH ** -0.5 * LOG2E), wcol(1)[:, 3 * gw:4 * gw], wcol(1)[:, 5 * gw:6 * gw],
            pad_cols(wcol(2), 0, LANE)]
    ws_t = [w.T.astype(BF) for w in ws_t]
    hd = MLA_NOPE + MLA_ROPE
    wuq = jnp.pad(mla_w_uq[l].reshape(MLA_Q_LORA, MLA_HEADS, hd), ((0, 0), (0, 0), (0, HEAD_PAD - hd)))
    wuq = wuq.reshape(MLA_Q_LORA, MLA_HEADS * HEAD_PAD).astype(BF)
    wukv = mla_w_ukv[l].reshape(MLA_KV_LORA, MLA_HEADS, MLA_NOPE + MLA_DV)
    wuk = jnp.pad(wukv[:, :, :MLA_NOPE], ((0, 0), (0, 0), (0, HEAD_PAD - MLA_NOPE)))
    wuk = wuk.reshape(MLA_KV_LORA, MLA_HEADS * HEAD_PAD).astype(BF)
    wuv = wukv[:, :, MLA_NOPE:].reshape(MLA_KV_LORA, MLA_HEADS * MLA_DV).astype(BF)

    h, qn, kvn, gn, cq, ckv, kr, qm, gm, qn_t, vs_t, vw_t, gn_t = _ln_inproj(
        x.reshape(n, d), row2(ln0_g), row2(ln0_b), ws, scales, ws_t, tm)

    q_mla, k_mla, v_mla = _mla_prep(cq, ckv, kr, row2(mla_q_norm[l]), row2(mla_kv_norm[l]), wuq, wuk, wuv, seq, tm)
    o_mla = _mla_attn(q_mla, k_mla, v_mla, batch, seq, min(512, seq))

    nc = seq // CMP_STRIDE
    n_slc = seq // SLC_LEN
    cc = kvn[:, :2 * gw].reshape(batch, nc, CMP_STRIDE, 2 * NSA_GROUPS, NSA_DH)
    cc = cc.transpose(0, 3, 1, 2, 4).reshape(batch, 2 * NSA_GROUPS, nc, CMP_STRIDE * NSA_DH)
    pos = jnp.stack([cmp_pos_k[l], cmp_pos_v[l]]).reshape(2, 1, CMP_LEN * NSA_DH)
    w1c = jnp.stack([cmp_k_w1[l], cmp_v_w1[l]]).astype(BF)
    w2c = jnp.stack([cmp_k_w2[l], cmp_v_w2[l]]).astype(BF)
    kvc = _nsa_compress(cc, pos, w1c, w2c)
    overlap = jnp.asarray(_overlap_matrix(nc, n_slc), BF)
    o_cmp, sel_t = _nsa_cmp(rel_bias, qn, kvc, overlap, batch, seq)
    o_nsa = _nsa_attn(rel_bias, qn_t, kvn, vs_t, vw_t, sel_t, gn, gn_t, o_cmp, batch, seq)

    o_mem = _mem_attn(qm, mem.reshape(-1, d), mem_w_kv[l].astype(BF), batch, seq, min(512, seq))

    h1 = _merge(o_nsa, o_mla, o_mem, gm, h, w_branch[l].astype(BF), w_out[l].astype(BF),
                row2(ln1_g[l]), row2(ln1_b[l]), tm)

    rw_t = router_w[l].T
    rw_hi = rw_t.astype(BF)
    rw_lo = (rw_t - rw_hi.astype(F32)).astype(BF)
    eidx_t, w_t, rank_t, counts = _router(h1, rw_hi, rw_lo, router_b[l].reshape(N_EXPERTS, 1), tm)
    pad_start, fill_start, fill_n, blk_e, nused = _block_layout(counts[:, 0], n)
    pos = _positions(eidx_t, rank_t, pad_start.astype(F32).reshape(N_EXPERTS, 1), tm)
    xs = _permute(fill_start, fill_n, pos, h1, blk_e.shape[0] * EXPERT_BLOCK, tm)
    ys = _experts(blk_e, nused, xs, exp_w1[l], exp_w3[l], exp_w2[l])
    out = _combine(pos, ys, w_t.T, h1, sh_w1[l].astype(BF), sh_w3[l].astype(BF), sh_w2[l].astype(BF),
                   row2(ln2_g[l]), row2(ln2_b[l]), min(128, seq))
    return out.reshape(batch, seq, d)
```

```python
import functools
import math

import numpy as np
import jax
import jax.numpy as jnp
from jax import lax
from jax.experimental import pallas as pl
from jax.experimental.pallas import tpu as pltpu

BF = jnp.bfloat16
F32 = jnp.float32
I32 = jnp.int32

D_MODEL = 1024
DEPTH = 1
NSA_HEADS = 8
NSA_GROUPS = 2
NSA_HPG = NSA_HEADS // NSA_GROUPS
NSA_DH = 64
CMP_LEN = 32
CMP_STRIDE = 16
CMP_HID = 256
SLC_LEN = 64
SLC_TOPN = 16
WIN = 512
MLA_HEADS = 8
MLA_NOPE = 64
MLA_ROPE = 32
MLA_DV = 64
MLA_Q_LORA = 768
MLA_KV_LORA = 256
ROPE_THETA = 10000.0
MEM_HEADS = 4
MEM_DH = 128
N_BRANCH = 3
BRANCH_W = NSA_HEADS * NSA_DH
REL_BUCKETS = 32
REL_MAX_DIST = 128
N_EXPERTS = 256
TOP_K = 8
N_EXPERT_GROUPS = 8
TOPK_GROUPS = 4
D_EXPERT = 256
ROUTE_SCALE = 2.5
EXPERT_BLOCK = 512
LN_EPS = 1e-5
RMS_EPS = 1e-6
NEG = -1e30
BIG = 1e30
ALPHA = (2 * DEPTH) ** 0.25
IN_SPLITS = (NSA_HEADS * NSA_DH, 6 * NSA_GROUPS * NSA_DH, 3 * NSA_HEADS, MLA_Q_LORA, MLA_KV_LORA,
             MLA_ROPE, MEM_HEADS * MEM_DH, N_BRANCH * D_MODEL)

LANE = 128
SUBLANES = 8
HEAD_PAD = 128
ROPE_LANE0 = MLA_NOPE
ROPE_HALF = MLA_ROPE // 2
VMEM_LIMIT = 56 * 1024 * 1024
NSA_TQ = 128
FAR_CHUNK = 512


def _cparams(sem):
    return pltpu.CompilerParams(dimension_semantics=sem, vmem_limit_bytes=VMEM_LIMIT)


def _bucket_starts():
    max_exact = REL_BUCKETS // 2
    d = np.arange(0, 4 * REL_MAX_DIST)
    nf = np.maximum(d, 1).astype(np.float32)
    large = max_exact + (np.log(nf / np.float32(max_exact)) / np.float32(math.log(REL_MAX_DIST / max_exact))
                         * np.float32(REL_BUCKETS - max_exact)).astype(np.int32)
    large = np.minimum(large, REL_BUCKETS - 1)
    bucket = np.where(d < max_exact, d, large)
    return [int(np.argmax(bucket >= b)) for b in range(REL_BUCKETS)]


BUCKET_START = _bucket_starts()
FAR_DIST = BUCKET_START[REL_BUCKETS - 1]


def _rel_bias(dist, tab_ref, head):
    val = jnp.full(dist.shape, tab_ref[REL_BUCKETS - 1, head], F32)
    for b in range(REL_BUCKETS - 2, -1, -1):
        val = jnp.where(dist < BUCKET_START[b + 1], tab_ref[b, head], val)
    return val


def _layer_norm(x, g, b):
    mu = jnp.mean(x, -1, keepdims=True)
    xc = x - mu
    var = jnp.mean(xc * xc, -1, keepdims=True)
    return xc * lax.rsqrt(var + LN_EPS) * g + b


def _dot(a, b):
    return jnp.dot(a, b, preferred_element_type=F32)


def _dot_nt(a, b):
    return lax.dot_general(a, b, (((1,), (1,)), ((), ())), preferred_element_type=F32)


def _ln_inproj_kernel(*refs, scales, n_rowmajor):
    x_ref, g_ref, b_ref = refs[:3]
    n_w = (len(refs) - 4) // 2
    w_refs = refs[3:3 + n_w]
    h_ref = refs[3 + n_w]
    o_refs = refs[4 + n_w:]
    h = _layer_norm(x_ref[...], g_ref[...], b_ref[...])
    h_ref[...] = h
    hb = h.astype(BF)
    for j, (w, o) in enumerate(zip(w_refs, o_refs)):
        if j < n_rowmajor:
            y = _dot(hb, w[...])
            o[...] = (y if scales[j] == 1.0 else y * scales[j]).astype(o.dtype)
        else:
            o[...] = _dot_nt(w[...], hb).astype(o.dtype)


def _ln_inproj(x2, g, b, ws, scales, ws_t, tm):
    n = x2.shape[0]
    row = lambda i: (i, 0)
    col = lambda i: (0, i)
    const = lambda i: (0, 0)
    in_specs = [pl.BlockSpec((tm, D_MODEL), row), pl.BlockSpec((1, D_MODEL), const), pl.BlockSpec((1, D_MODEL), const)]
    in_specs += [pl.BlockSpec(w.shape, const) for w in ws + ws_t]
    out_shape = [jax.ShapeDtypeStruct((n, D_MODEL), F32)]
    out_shape += [jax.ShapeDtypeStruct((n, w.shape[1]), BF) for w in ws]
    out_shape += [jax.ShapeDtypeStruct((w.shape[0], n), BF) for w in ws_t]
    out_specs = [pl.BlockSpec((tm, D_MODEL), row)] + [pl.BlockSpec((tm, w.shape[1]), row) for w in ws]
    out_specs += [pl.BlockSpec((w.shape[0], tm), col) for w in ws_t]
    return pl.pallas_call(
        functools.partial(_ln_inproj_kernel, scales=tuple(scales), n_rowmajor=len(ws)),
        grid=(n // tm,), in_specs=in_specs, out_specs=out_specs, out_shape=out_shape,
        compiler_params=_cparams(("parallel",)),
    )(x2, g, b, *ws, *ws_t)


def _rope_lanes(x, c, s1, s2):
    return x * c + pltpu.roll(x, LANE - ROPE_HALF, 1) * s1 + pltpu.roll(x, ROPE_HALF, 1) * s2


def _mla_prep_kernel(cq_ref, ckv_ref, kr_ref, qn_ref, kvn_ref, wuq, wuk, wuv,
                     cq_t, s1q_t, s2q_t, ck_t, s1k_t, s2k_t, q_out, k_out, v_out):
    cq = cq_ref[...].astype(F32)
    rq = cq * lax.rsqrt(jnp.mean(cq * cq, -1, keepdims=True) + RMS_EPS) * qn_ref[...]
    q = _dot(rq.astype(BF), wuq[...])
    ckv = ckv_ref[...].astype(F32)
    rkv = (ckv * lax.rsqrt(jnp.mean(ckv * ckv, -1, keepdims=True) + RMS_EPS) * kvn_ref[...]).astype(BF)
    kn = _dot(rkv, wuk[...])
    v = _dot(rkv, wuv[...])
    pair_w = 2 * MLA_DV
    ones = jnp.ones((v.shape[0], pair_w), v_out.dtype)
    for p in range(MLA_HEADS // 2):
        v_out[:, 2 * p * pair_w:(2 * p + 1) * pair_w] = v[:, p * pair_w:(p + 1) * pair_w].astype(v_out.dtype)
        v_out[:, (2 * p + 1) * pair_w:(2 * p + 2) * pair_w] = ones
    kr = _rope_lanes(kr_ref[...].astype(F32), ck_t[...], s1k_t[...], s2k_t[...])
    cq_c, s1q, s2q = cq_t[...], s1q_t[...], s2q_t[...]
    for h in range(MLA_HEADS):
        sl = slice(h * HEAD_PAD, (h + 1) * HEAD_PAD)
        q_out[:, sl] = _rope_lanes(q[:, sl], cq_c, s1q, s2q).astype(q_out.dtype)
        k_out[:, sl] = (kn[:, sl] + kr).astype(k_out.dtype)


def _rope_tables(seq, scale):
    freq = ROPE_THETA ** (-jnp.arange(ROPE_HALF, dtype=F32) / ROPE_HALF)
    ang = jnp.arange(seq, dtype=F32)[:, None] * freq[None, :]
    cos, sin = jnp.cos(ang) * scale, jnp.sin(ang) * scale
    z = lambda w: jnp.zeros((seq, w), F32)
    tail = HEAD_PAD - ROPE_LANE0 - MLA_ROPE
    c = jnp.concatenate([jnp.full((seq, ROPE_LANE0), scale, F32), cos, cos, z(tail)], 1)
    s1 = jnp.concatenate([z(ROPE_LANE0), -sin, z(ROPE_HALF + tail)], 1)
    s2 = jnp.concatenate([z(ROPE_LANE0 + ROPE_HALF), sin, z(tail)], 1)
    return c, s1, s2


def _mla_prep(cq, ckv, kr, q_norm, kv_norm, wuq, wuk, wuv, seq, tm):
    n = cq.shape[0]
    nt = seq // tm
    row = lambda i: (i, 0)
    const = lambda i: (0, 0)
    pos = lambda i: (i % nt, 0)
    tabs = _rope_tables(seq, (MLA_NOPE + MLA_ROPE) ** -0.5 * math.log2(math.e)) + _rope_tables(seq, 1.0)
    in_specs = [pl.BlockSpec((tm, MLA_Q_LORA), row), pl.BlockSpec((tm, MLA_KV_LORA), row), pl.BlockSpec((tm, LANE), row),
                pl.BlockSpec((1, MLA_Q_LORA), const), pl.BlockSpec((1, MLA_KV_LORA), const),
                pl.BlockSpec(wuq.shape, const), pl.BlockSpec(wuk.shape, const), pl.BlockSpec(wuv.shape, const)]
    in_specs += [pl.BlockSpec((tm, LANE), pos)] * 6
    hq = MLA_HEADS * HEAD_PAD
    hv = 2 * MLA_HEADS * MLA_DV
    return pl.pallas_call(
        _mla_prep_kernel, grid=(n // tm,), in_specs=in_specs,
        out_specs=[pl.BlockSpec((tm, hq), row), pl.BlockSpec((tm, hq), row), pl.BlockSpec((tm, hv), row)],
        out_shape=[jax.ShapeDtypeStruct((n, hq), BF), jax.ShapeDtypeStruct((n, hq), BF), jax.ShapeDtypeStruct((n, hv), BF)],
        compiler_params=_cparams(("parallel",)),
    )(cq, ckv, kr, q_norm, kv_norm, wuq, wuk, wuv, *tabs)


def _mla_attn_kernel(q_ref, k_ref, v_ref, o_ref, sa_scr, sb_scr, m_scr, acc_scr):
    i = pl.program_id(2)
    t = q_ref.shape[0]
    reps = t // LANE
    pair_w = 2 * MLA_DV
    m_scr[...] = jnp.full(m_scr.shape, NEG, F32)
    acc_scr[...] = jnp.zeros(acc_scr.shape, F32)

    def logits(tile, s_scr):
        kstart = pl.multiple_of(tile * t, t)
        for hh in range(2):
            sl = slice(hh * HEAD_PAD, (hh + 1) * HEAD_PAD)
            s_scr[hh] = _dot_nt(q_ref[:, sl], k_ref[pl.ds(kstart, t), sl])

    def consume(tile, s_scr, diagonal):
        v = v_ref[pl.ds(pl.multiple_of(tile * t, t), t), :]
        for hh in range(2):
            s = s_scr[hh]
            if diagonal:
                row = lax.broadcasted_iota(I32, (t, t), 0)
                col = lax.broadcasted_iota(I32, (t, t), 1)
                s = jnp.where(col <= row, s, NEG)
            m_prev = m_scr[hh]
            m_new = jnp.maximum(m_prev, jnp.max(s, 1, keepdims=True))
            a = jnp.exp2(m_prev - m_new)
            e = jnp.exp2(s - jnp.tile(m_new, (1, reps)))
            acc_scr[hh] = jnp.tile(a, (1, 2)) * acc_scr[hh] + _dot(e.astype(BF), v)
            m_scr[hh] = m_new

    logits(0, sa_scr)

    def body(p, carry):
        logits(2 * p + 1, sb_scr)
        consume(2 * p, sa_scr, False)
        logits(2 * p + 2, sa_scr)
        consume(2 * p + 1, sb_scr, False)
        return carry

    lax.fori_loop(0, i // 2, body, 0)

    @pl.when(i % 2 == 0)
    def _():
        consume(i, sa_scr, True)

    @pl.when(i % 2 == 1)
    def _():
        logits(i, sb_scr)
        consume(i - 1, sa_scr, False)
        consume(i, sb_scr, True)

    lane = lax.broadcasted_iota(I32, (t, pair_w), 1)
    o = jnp.where(lane < MLA_DV, acc_scr[0, :, :pair_w] / acc_scr[0, :, pair_w:],
                  acc_scr[1, :, :pair_w] / acc_scr[1, :, pair_w:])
    o_ref[...] = o.astype(o_ref.dtype)


def _mla_attn(q, k, v, batch, seq, t):
    n = q.shape[0]
    nt = seq // t
    qmap = lambda b, hp, i: (b * nt + i, hp)
    kmap = lambda b, hp, i: (b, hp)
    return pl.pallas_call(
        _mla_attn_kernel, grid=(batch, MLA_HEADS // 2, nt),
        in_specs=[pl.BlockSpec((t, 2 * HEAD_PAD), qmap), pl.BlockSpec((seq, 2 * HEAD_PAD), kmap),
                  pl.BlockSpec((seq, 4 * MLA_DV), kmap)],
        out_specs=pl.BlockSpec((t, 2 * MLA_DV), qmap),
        out_shape=jax.ShapeDtypeStruct((n, MLA_HEADS * MLA_DV), BF),
        scratch_shapes=[pltpu.VMEM((2, t, t), F32), pltpu.VMEM((2, t, t), F32),
                        pltpu.VMEM((2, t, LANE), F32), pltpu.VMEM((2, t, 4 * MLA_DV), F32)],
        compiler_params=_cparams(("parallel", "parallel", "arbitrary")),
    )(q, k, v)


def _nsa_compress_kernel(c_ref, pos_ref, w1_ref, w2_ref, o_ref):
    nc = c_ref.shape[0]
    half = CMP_STRIDE * NSA_DH
    c = c_ref[...]
    top = _dot(c, w1_ref[:half, :])
    bot = _dot(c, w1_ref[half:, :])
    posb = _dot(jnp.broadcast_to(pos_ref[...], (8, 2 * half)).astype(BF), w1_ref[...])[:1]
    hid = top + pltpu.roll(bot, nc - 1, 0) + posb
    o_ref[...] = _dot(jax.nn.gelu(hid).astype(BF), w2_ref[...]).astype(o_ref.dtype)


def _nsa_compress(cc, pos, w1, w2):
    b, _, nc, half = cc.shape
    sq = pl.Squeezed()
    return pl.pallas_call(
        _nsa_compress_kernel, grid=(b, 2 * NSA_GROUPS),
        in_specs=[pl.BlockSpec((sq, sq, nc, half), lambda i, c: (i, c, 0, 0)),
                  pl.BlockSpec((sq, 1, 2 * half), lambda i, c: (c // NSA_GROUPS, 0, 0)),
                  pl.BlockSpec((sq, 2 * half, CMP_HID), lambda i, c: (c // NSA_GROUPS, 0, 0)),
                  pl.BlockSpec((sq, CMP_HID, NSA_DH), lambda i, c: (c // NSA_GROUPS, 0, 0))],
        out_specs=pl.BlockSpec((sq, sq, nc, NSA_DH), lambda i, c: (i, c, 0, 0)),
        out_shape=jax.ShapeDtypeStruct((b, 2 * NSA_GROUPS, nc, NSA_DH), BF),
        compiler_params=_cparams(("parallel", "parallel")),
    )(cc, pos, w1, w2)


CMP_TQ = 512
CMP_BIAS_COLS = LANE // 2
CMP_BIAS_BACK = -(-(FAR_DIST + CMP_LEN - 1) // CMP_STRIDE)
assert (CMP_TQ - CMP_LEN) // CMP_STRIDE + CMP_BIAS_BACK < CMP_BIAS_COLS


def _nsa_cmp_kernel(tab_ref, q_ref, kc_ref, vc_ref, ov_ref, oc_ref, sel_ref, e_scr, *, n_top):
    b, g, i = pl.program_id(0), pl.program_id(1), pl.program_id(2)
    tq = q_ref.shape[0]
    nc = kc_ref.shape[0]
    n_slc = ov_ref.shape[1]
    qs = i * tq

    @pl.when((b == 0) & (g == 0) & (i == 0))
    def _():
        q_io = lax.broadcasted_iota(I32, (tq, LANE), 0)
        lane = lax.broadcasted_iota(I32, (tq, LANE), 1)
        jj = lane % CMP_BIAS_COLS
        dist = q_io - CMP_STRIDE * (jj - CMP_BIAS_BACK) - (CMP_LEN - 1)
        live = (dist >= 0) & (lane < 2 * CMP_BIAS_COLS)
        for h in range(NSA_HEADS):
            e = jnp.where(live, (_rel_bias(dist, tab_ref, h) - tab_ref[REL_BUCKETS - 1, h]) * LOG2E, 0.0)
            hi = e.astype(BF)
            lo = (e - hi.astype(F32)).astype(BF)
            e_scr[h] = jnp.where(lane < CMP_BIAS_COLS, hi, lo)

    n0 = qs // CMP_STRIDE
    jrow = lax.broadcasted_iota(I32, (LANE, nc), 0)
    ncol = lax.broadcasted_iota(I32, (LANE, nc), 1)
    ft = jnp.where((ncol == n0 + (jrow % CMP_BIAS_COLS) - CMP_BIAS_BACK) & (jrow < 2 * CMP_BIAS_COLS), 1.0, 0.0).astype(BF)

    t = qs + lax.broadcasted_iota(I32, (tq, nc), 0)
    n_io = lax.broadcasted_iota(I32, (tq, nc), 1)
    mask_add = jnp.where((t >= n_io * CMP_STRIDE + (CMP_LEN - 1)) & (n_io < nc - 1), 0.0, NEG)
    has_key = qs + lax.broadcasted_iota(I32, (tq, 1), 0) >= CMP_LEN - 1
    kc = kc_ref[...]
    vc = vc_ref[...]
    psum = jnp.zeros((tq, nc), F32)
    for h in range(NSA_HPG):
        head = g * NSA_HPG + h
        qh = q_ref[:, h * NSA_DH:(h + 1) * NSA_DH]
        s = _dot_nt(qh, kc) + _dot(e_scr[head], ft) + mask_add
        e = jnp.exp2(s - jnp.max(s, 1, keepdims=True))
        p = e * jnp.where(has_key, 1.0 / jnp.sum(e, 1, keepdims=True), 0.0)
        oc_ref[:, h * NSA_DH:(h + 1) * NSA_DH] = _dot(p.astype(BF), vc).astype(oc_ref.dtype)
        psum = psum + p

    ov = ov_ref[...]
    p0 = psum.astype(BF)
    r1 = psum - p0.astype(F32)
    p1 = r1.astype(BF)
    p2 = (r1 - p1.astype(F32)).astype(BF)
    imp = (_dot(p0, ov) + _dot(p1, ov) + _dot(p2, ov)).T

    tj = qs + lax.broadcasted_iota(I32, (n_slc, tq), 1)
    j = lax.broadcasted_iota(I32, (n_slc, tq), 0)
    cur = tj // SLC_LEN
    forced = (j == 0) | (j == cur) | (j == cur - 1)
    work = jnp.where(j * SLC_LEN > tj, NEG, jnp.where(forced, BIG, imp))
    sel = jnp.full((n_slc, tq), NEG, F32)
    for _ in range(n_top):
        mx, first = _first_max(work, j, n_slc)
        hit = j == first
        sel = jnp.where(hit & (mx > 0.5 * NEG), 0.0, sel)
        work = jnp.where(hit, -jnp.inf, work)
    sel_ref[...] = sel


def _nsa_cmp(tab, qn, kvc, overlap, batch, seq):
    n = qn.shape[0]
    tq = min(CMP_TQ, seq)
    nq = seq // tq
    nc = kvc.shape[2]
    n_slc = overlap.shape[1]
    sq = pl.Squeezed()
    gw = NSA_HPG * NSA_DH
    return pl.pallas_call(
        functools.partial(_nsa_cmp_kernel, n_top=min(SLC_TOPN, n_slc)), grid=(batch, NSA_GROUPS, nq),
        in_specs=[pl.BlockSpec(memory_space=pltpu.SMEM),
                  pl.BlockSpec((tq, gw), lambda b, g, i: (b * nq + i, g)),
                  pl.BlockSpec((sq, sq, nc, NSA_DH), lambda b, g, i: (b, g, 0, 0)),
                  pl.BlockSpec((sq, sq, nc, NSA_DH), lambda b, g, i: (b, NSA_GROUPS + g, 0, 0)),
                  pl.BlockSpec((nc, n_slc), lambda b, g, i: (0, 0))],
        out_specs=[pl.BlockSpec((tq, gw), lambda b, g, i: (b * nq + i, g)),
                   pl.BlockSpec((sq, n_slc, tq), lambda b, g, i: (g, 0, b * nq + i))],
        out_shape=[jax.ShapeDtypeStruct((n, NSA_HEADS * NSA_DH), BF),
                   jax.ShapeDtypeStruct((NSA_GROUPS, n_slc, n), F32)],
        scratch_shapes=[pltpu.VMEM((NSA_HEADS, tq, LANE), BF)],
        compiler_params=_cparams(("arbitrary", "arbitrary", "arbitrary")),
    )(tab, qn, kvc, kvc, overlap)


LOG2E = math.log2(math.e)
DEN_ROWS = 16


def _nsa_attn_kernel(tab_ref, qt_ref, ks_ref, kw_ref, vst_ref, vwt_ref, sel_ref, gate_ref, gate_t_ref, oc_ref, o_ref,
                     d_scr, q_scr, sa_scr, sb_scr, m_scr, acc_scr):
    b, i = pl.program_id(0), pl.program_id(1)
    tq = NSA_TQ
    lanes = NSA_HPG * tq
    qs = i * tq
    near_w = 2 * tq
    far_w = FAR_CHUNK
    win_far_w = WIN - tq

    @pl.when((b == 0) & (i == 0))
    def _():
        kk = lax.broadcasted_iota(I32, (near_w, tq), 0)
        q_io = lax.broadcasted_iota(I32, (near_w, tq), 1)
        dist = jnp.maximum(q_io + tq - kk, 0)
        for g in range(NSA_GROUPS):
            d_scr[g] = jnp.concatenate(
                [(_rel_bias(dist, tab_ref, g * NSA_HPG + h) - tab_ref[REL_BUCKETS - 1, g * NSA_HPG + h]) * LOG2E
                 for h in range(NSA_HPG)], axis=1)

    qt = qt_ref[...]
    for g in range(NSA_GROUPS):
        qg = jnp.concatenate([qt[(g * NSA_HPG + h) * NSA_DH:(g * NSA_HPG + h + 1) * NSA_DH, :]
                              for h in range(NSA_HPG)], axis=1)
        parts = [jnp.zeros((NSA_DH, lanes), BF)] * NSA_GROUPS
        parts[g] = qg
        q_scr[g] = jnp.concatenate(parts, axis=0)

    def update(slot, s, vt):
        m_prev = m_scr[slot]
        m_new = jnp.maximum(m_prev, jnp.max(s, 0, keepdims=True))
        a = jnp.exp2(m_prev - m_new)
        e = jnp.exp2(s - m_new)
        vt_den = jnp.concatenate([vt, jnp.ones((DEN_ROWS, vt.shape[1]), BF)], axis=0)
        acc_scr[slot] = a * acc_scr[slot] + _dot(vt_den, e.astype(BF))
        m_scr[slot] = m_new

    def all_heads(x):
        return jnp.tile(x, (1, NSA_HPG))

    def sel_rows(g, first_block, n_blk):
        return jnp.concatenate(
            [jnp.broadcast_to(sel_ref[g, pl.ds(jnp.maximum(first_block + r, 0), 1), :], (SLC_LEN, tq))
             for r in range(n_blk)], axis=0)

    m_scr[...] = jnp.full(m_scr.shape, NEG, F32)
    acc_scr[...] = jnp.zeros(acc_scr.shape, F32)

    kk = lax.broadcasted_iota(I32, (near_w, tq), 0)
    q_io = lax.broadcasted_iota(I32, (near_w, tq), 1)
    kpos_near = qs - tq + kk
    causal_add = jnp.where((kpos_near >= 0) & (kpos_near <= qs + q_io), 0.0, NEG)
    start_a = pl.multiple_of(qs, tq)
    start_b = pl.multiple_of(jnp.maximum(qs - tq, 0), tq)
    for g in range(NSA_GROUPS):
        gs = slice(g * NSA_DH, (g + 1) * NSA_DH)
        for br, (k_ref, vt_ref) in enumerate(((ks_ref, vst_ref), (kw_ref, vwt_ref))):
            k = jnp.concatenate([k_ref[pl.ds(start_b, tq), :], k_ref[pl.ds(start_a, tq), :]], axis=0)
            vt = jnp.concatenate([vt_ref[gs, pl.ds(start_b, tq)], vt_ref[gs, pl.ds(start_a, tq)]], axis=1)
            add = causal_add + sel_rows(g, 2 * i - 2, near_w // SLC_LEN) if br == 0 else causal_add
            update(2 * g + br, _dot(k, q_scr[g]) + d_scr[g] + all_heads(add), vt)

    ws = pl.multiple_of(jnp.maximum(qs - WIN, 0), tq)
    kpos_w = ws + lax.broadcasted_iota(I32, (win_far_w, tq), 0)
    t_w = qs + lax.broadcasted_iota(I32, (win_far_w, tq), 1)
    add_w = all_heads(jnp.where((kpos_w < qs - tq) & (kpos_w > t_w - WIN), 0.0, NEG))
    k_w = kw_ref[pl.ds(ws, win_far_w), :]
    for g in range(NSA_GROUPS):
        gs = slice(g * NSA_DH, (g + 1) * NSA_DH)
        update(2 * g + 1, _dot(k_w, q_scr[g]) + add_w, vwt_ref[gs, pl.ds(ws, win_far_w)])

    n_far = (jnp.maximum(i - 1, 0) + (far_w // tq - 1)) // (far_w // tq)
    last_chunk = ks_ref.shape[0] // far_w - 1
    krow = lax.broadcasted_iota(I32, (far_w, tq), 0)

    def far_logits(c, s_scr):
        c = jnp.minimum(c, last_chunk)
        base = pl.multiple_of(c * far_w, far_w)
        in_range = jnp.where(base + krow < qs - tq, 0.0, NEG)
        k = ks_ref[pl.ds(base, far_w), :]
        for g in range(NSA_GROUPS):
            add = all_heads(sel_rows(g, c * (far_w // SLC_LEN), far_w // SLC_LEN) + in_range)
            s_scr[g] = _dot(k, q_scr[g]) + add

    def far_consume(c, s_scr):
        base = pl.multiple_of(c * far_w, far_w)
        for g in range(NSA_GROUPS):
            update(2 * g, s_scr[g], vst_ref[g * NSA_DH:(g + 1) * NSA_DH, pl.ds(base, far_w)])

    far_logits(0, sa_scr)

    def far_body(p, carry):
        far_logits(2 * p + 1, sb_scr)
        far_consume(2 * p, sa_scr)
        far_logits(2 * p + 2, sa_scr)
        far_consume(2 * p + 1, sb_scr)
        return carry

    lax.fori_loop(0, n_far // 2, far_body, 0)

    @pl.when(n_far % 2 == 1)
    def _():
        far_consume(n_far - 1, sa_scr)

    gates = jax.nn.sigmoid(gate_ref[...].astype(F32))
    gates_t = jax.nn.sigmoid(gate_t_ref[...].astype(F32))
    for g in range(NSA_GROUPS):
        o_s = acc_scr[2 * g, :NSA_DH] / acc_scr[2 * g, NSA_DH:NSA_DH + 1]
        o_w = acc_scr[2 * g + 1, :NSA_DH] / acc_scr[2 * g + 1, NSA_DH:NSA_DH + 1]
        for h in range(NSA_HPG):
            head = g * NSA_HPG + h
            hl = slice(head * NSA_DH, (head + 1) * NSA_DH)
            cl = slice(h * tq, (h + 1) * tq)
            sw_t = (gates_t[3 * head + 1:3 * head + 2, :] * o_s[:, cl]
                    + gates_t[3 * head + 2:3 * head + 3, :] * o_w[:, cl])
            o = gates[:, 3 * head:3 * head + 1] * oc_ref[:, hl].astype(F32) + sw_t.T
            o_ref[:, hl] = o.astype(o_ref.dtype)


def _nsa_attn(tab, qn_t, kvn, vs_t, vw_t, sel_t, gn, gn_t, oc, batch, seq):
    n = kvn.shape[0]
    tq = NSA_TQ
    nq = seq // tq
    n_slc = sel_t.shape[1]
    hw = NSA_HEADS * NSA_DH
    gw = NSA_GROUPS * NSA_DH
    lanes = NSA_HPG * tq
    row = lambda b, i: (b * nq + i, 0)
    col = lambda b, i: (0, b * nq + i)
    k_spec = lambda kind: pl.BlockSpec((seq, gw), lambda b, i: (b, kind))
    vt_spec = pl.BlockSpec((gw, seq), lambda b, i: (0, b))
    n_state = 2 * NSA_GROUPS
    return pl.pallas_call(
        _nsa_attn_kernel, grid=(batch, nq),
        in_specs=[pl.BlockSpec(memory_space=pltpu.SMEM), pl.BlockSpec((hw, tq), col),
                  k_spec(2), k_spec(4), vt_spec, vt_spec,
                  pl.BlockSpec((NSA_GROUPS, n_slc, tq), lambda b, i: (0, 0, b * nq + i)),
                  pl.BlockSpec((tq, LANE), row), pl.BlockSpec((LANE, tq), col), pl.BlockSpec((tq, hw), row)],
        out_specs=pl.BlockSpec((tq, hw), row),
        out_shape=jax.ShapeDtypeStruct((n, hw), BF),
        scratch_shapes=[pltpu.VMEM((NSA_GROUPS, 2 * tq, lanes), F32), pltpu.VMEM((NSA_GROUPS, gw, lanes), BF),
                        pltpu.VMEM((NSA_GROUPS, FAR_CHUNK, lanes), F32), pltpu.VMEM((NSA_GROUPS, FAR_CHUNK, lanes), F32),
                        pltpu.VMEM((n_state, 1, lanes), F32), pltpu.VMEM((n_state, NSA_DH + DEN_ROWS, lanes), F32)],
        compiler_params=_cparams(("arbitrary", "arbitrary")),
    )(tab, qn_t, kvn, kvn, vs_t, vw_t, sel_t, gn, gn_t, oc)


def _mem_attn_kernel(q_ref, mem_ref, w_ref, o_ref, kv_scr):
    @pl.when(pl.program_id(1) == 0)
    def _():
        kv_scr[...] = _dot(mem_ref[...].astype(BF), w_ref[...]).astype(BF)

    hw = MEM_HEADS * MEM_DH
    for h in range(MEM_HEADS):
        sl = slice(h * MEM_DH, (h + 1) * MEM_DH)
        s = _dot_nt(q_ref[:, sl], kv_scr[:, sl])
        e = jnp.exp(s - jnp.max(s, 1, keepdims=True))
        p = e / jnp.sum(e, 1, keepdims=True)
        o_ref[:, sl] = _dot(p.astype(BF), kv_scr[:, hw + h * MEM_DH:hw + (h + 1) * MEM_DH]).astype(o_ref.dtype)


def _mem_attn(qm, mem2, w_kv, batch, seq, tq):
    n = qm.shape[0]
    nq = seq // tq
    m = mem2.shape[0] // batch
    hw = MEM_HEADS * MEM_DH
    return pl.pallas_call(
        _mem_attn_kernel, grid=(batch, nq),
        in_specs=[pl.BlockSpec((tq, hw), lambda b, i: (b * nq + i, 0)),
                  pl.BlockSpec((m, D_MODEL), lambda b, i: (b, 0)),
                  pl.BlockSpec((D_MODEL, 2 * hw), lambda b, i: (0, 0))],
        out_specs=pl.BlockSpec((tq, hw), lambda b, i: (b * nq + i, 0)),
        out_shape=jax.ShapeDtypeStruct((n, hw), BF),
        scratch_shapes=[pltpu.VMEM((m, 2 * hw), BF)],
        compiler_params=_cparams(("arbitrary", "arbitrary")),
    )(qm, mem2, w_kv)


def _merge_kernel(on_ref, ol_ref, om_ref, gm_ref, h_ref, wb_ref, wo_ref, g_ref, b_ref, h1_ref):
    merged = None
    for c, o_ref in enumerate((on_ref, ol_ref, om_ref)):
        gate = jax.nn.sigmoid(gm_ref[:, c * D_MODEL:(c + 1) * D_MODEL].astype(F32))
        term = gate * _dot(o_ref[...], wb_ref[c])
        merged = term if merged is None else merged + term
    y = ALPHA * h_ref[...] + _dot(merged.astype(BF), wo_ref[...])
    h1_ref[...] = _layer_norm(y, g_ref[...], b_ref[...])


def _merge(o_nsa, o_mla, o_mem, gm, h, wb, wo, g, b, tm):
    n = h.shape[0]
    row = lambda i: (i, 0)
    const = lambda i: (0, 0)
    return pl.pallas_call(
        _merge_kernel, grid=(n // tm,),
        in_specs=[pl.BlockSpec((tm, BRANCH_W), row)] * 3 + [
            pl.BlockSpec((tm, N_BRANCH * D_MODEL), row), pl.BlockSpec((tm, D_MODEL), row),
            pl.BlockSpec(wb.shape, lambda i: (0, 0, 0)), pl.BlockSpec(wo.shape, const),
            pl.BlockSpec((1, D_MODEL), const), pl.BlockSpec((1, D_MODEL), const)],
        out_specs=pl.BlockSpec((tm, D_MODEL), row),
        out_shape=jax.ShapeDtypeStruct((n, D_MODEL), F32),
        compiler_params=_cparams(("parallel",)),
    )(o_nsa, o_mla, o_mem, gm, h, wb, wo, g, b)


def _first_max(vals, idx, limit):
    mx = jnp.max(vals, 0, keepdims=True)
    first = jnp.min(jnp.where(vals == mx, idx, limit), 0, keepdims=True)
    return mx, first


def _router_kernel(h_ref, whi_ref, wlo_ref, b_ref, tri_ref, idx_ref, w_ref, rank_ref, cnt_ref):
    @pl.when(pl.program_id(0) == 0)
    def _():
        cnt_ref[...] = jnp.zeros(cnt_ref.shape, F32)

    h = h_ref[...]
    hhi = h.astype(BF)
    hlo = (h - hhi.astype(F32)).astype(BF)
    whi = whi_ref[...]
    logits = _dot_nt(whi, hhi) + _dot_nt(whi, hlo) + _dot_nt(wlo_ref[...], hhi)
    s = jax.nn.sigmoid(logits)
    sb = s + b_ref[...]
    tm = s.shape[1]
    gsz = N_EXPERTS // N_EXPERT_GROUPS
    e_io = lax.broadcasted_iota(I32, (gsz, tm), 0)
    scores = []
    for g in range(N_EXPERT_GROUPS):
        vals = sb[g * gsz:(g + 1) * gsz]
        m1, first = _first_max(vals, e_io, gsz)
        m2 = jnp.max(jnp.where(e_io == first, -jnp.inf, vals), 0, keepdims=True)
        scores.append(m1 + m2)
    gs = jnp.concatenate(scores, axis=0)
    g_io = lax.broadcasted_iota(I32, (N_EXPERT_GROUPS, tm), 0)
    x_io = lax.broadcasted_iota(I32, (N_EXPERTS, tm), 0)
    allowed = jnp.zeros((N_EXPERTS, tm), jnp.bool_)
    for _ in range(TOPK_GROUPS):
        _, first = _first_max(gs, g_io, N_EXPERT_GROUPS)
        gs = jnp.where(g_io == first, -jnp.inf, gs)
        allowed = allowed | (x_io // gsz == first)
    work = jnp.where(allowed, sb, NEG)
    base = cnt_ref[:, :1]
    tri = tri_ref[...]
    idxs, ws, ranks = [], [], []
    for _ in range(TOP_K):
        _, first = _first_max(work, x_io, N_EXPERTS)
        hit = x_io == first
        idxs.append(first)
        ws.append(jnp.sum(jnp.where(hit, s, 0.0), 0, keepdims=True))
        work = jnp.where(hit, -jnp.inf, work)
        onehot = jnp.where(hit, 1.0, 0.0)
        before = _dot(onehot.astype(BF), tri)
        ranks.append(jnp.sum(jnp.where(hit, base + before, 0.0), 0, keepdims=True))
        base = base + jnp.sum(onehot, 1, keepdims=True)
    wsel = jnp.concatenate(ws, axis=0)
    idx_ref[...] = jnp.concatenate(idxs, axis=0)
    w_ref[...] = wsel / jnp.sum(wsel, 0, keepdims=True) * ROUTE_SCALE
    rank_ref[...] = jnp.concatenate(ranks, axis=0).astype(I32)
    cnt_ref[...] = jnp.broadcast_to(base, cnt_ref.shape)


def _router(h1, whi, wlo, rb, tm):
    n = h1.shape[0]
    tri = jnp.asarray(np.triu(np.ones((tm, tm), np.float32), 1), BF)
    slot = pl.BlockSpec((TOP_K, tm), lambda i: (0, i))
    const = lambda i: (0, 0)
    return pl.pallas_call(
        _router_kernel, grid=(n // tm,),
        in_specs=[pl.BlockSpec((tm, D_MODEL), lambda i: (i, 0)),
                  pl.BlockSpec((N_EXPERTS, D_MODEL), const), pl.BlockSpec((N_EXPERTS, D_MODEL), const),
                  pl.BlockSpec((N_EXPERTS, 1), const), pl.BlockSpec((tm, tm), const)],
        out_specs=[slot, slot, slot, pl.BlockSpec((N_EXPERTS, LANE), const)],
        out_shape=[jax.ShapeDtypeStruct((TOP_K, n), I32), jax.ShapeDtypeStruct((TOP_K, n), F32),
                   jax.ShapeDtypeStruct((TOP_K, n), I32), jax.ShapeDtypeStruct((N_EXPERTS, LANE), F32)],
        compiler_params=_cparams(("arbitrary",)),
    )(h1, whi, wlo, rb, tri)


def _pos_kernel(idx_ref, rank_ref, start_ref, pos_ref):
    tm = idx_ref.shape[1]
    x_io = lax.broadcasted_iota(I32, (N_EXPERTS, tm), 0)
    start = start_ref[...]
    rows = [jnp.sum(jnp.where(x_io == idx_ref[k:k + 1, :], start, 0.0), 0, keepdims=True) for k in range(TOP_K)]
    pos_ref[...] = jnp.concatenate(rows, axis=0).astype(I32) + rank_ref[...]


def _positions(eidx_t, rank_t, pad_start, tm):
    n = eidx_t.shape[1]
    slot = pl.BlockSpec((TOP_K, tm), lambda i: (0, i))
    return pl.pallas_call(
        _pos_kernel, grid=(n // tm,),
        in_specs=[slot, slot, pl.BlockSpec((N_EXPERTS, 1), lambda i: (0, 0))],
        out_specs=slot, out_shape=jax.ShapeDtypeStruct((TOP_K, n), I32),
        compiler_params=_cparams(("parallel",)),
    )(eidx_t, rank_t, pad_start)


ROW_UNROLL = 8
PACKED_W = D_MODEL // 2
U32 = jnp.uint32


def _fill_groups():
    p = EXPERT_BLOCK // 2
    while p >= 1:
        yield p
        p //= 2


def _pack_rows(x):
    bits = lambda v: lax.bitcast_convert_type(v.astype(BF).astype(F32), U32)
    return (bits(x[:, :PACKED_W]) >> 16) | (bits(x[:, PACKED_W:]) & U32(0xFFFF0000))


def _unpack_rows(u):
    return (lax.bitcast_convert_type(u << 16, F32), lax.bitcast_convert_type(u & U32(0xFFFF0000), F32))


def _permute_kernel(fill_start_ref, fill_n_ref, pos_ref, h_ref, xs_hbm, xp, zbuf, row_sem, fill_sem, *, fills_per_step):
    i = pl.program_id(0)
    tm = h_ref.shape[0]
    n_fills = fill_n_ref.shape[0]
    zbuf[...] = jnp.zeros(zbuf.shape, zbuf.dtype)
    xp[...] = _pack_rows(h_ref[...])

    def fill_copies(e):
        n = fill_n_ref[e]
        start = fill_start_ref[e]
        for p in _fill_groups():
            @pl.when((n & p) != 0)
            def _(p=p):
                off = start + (n & (p - 1))
                if p < SUBLANES:
                    for r in range(p):
                        pltpu.make_async_copy(zbuf.at[pl.ds(r, 1)], xs_hbm.at[pl.ds(off + r, 1)], fill_sem).start()
                else:
                    pltpu.make_async_copy(zbuf.at[pl.ds(0, p)], xs_hbm.at[pl.ds(pl.multiple_of(off, SUBLANES), p)],
                                          fill_sem).start()

    def fill_waits(e):
        n = fill_n_ref[e]
        for p in _fill_groups():
            @pl.when((n & p) != 0)
            def _(p=p):
                pltpu.make_async_copy(zbuf.at[pl.ds(0, p)], zbuf.at[pl.ds(0, p)], fill_sem).wait()

    for q in range(fills_per_step):
        e = i * fills_per_step + q

        @pl.when(e < n_fills)
        def _(e=e):
            fill_copies(e)

    def body(t, c):
        for k in range(TOP_K):
            pltpu.make_async_copy(xp.at[pl.ds(t, 1)], xs_hbm.at[pl.ds(pos_ref[k, t], 1)], row_sem).start()
        return c

    lax.fori_loop(0, tm, body, 0, unroll=ROW_UNROLL)

    for q in range(fills_per_step):
        e = i * fills_per_step + q

        @pl.when(e < n_fills)
        def _(e=e):
            fill_waits(e)

    pltpu.make_async_copy(xs_hbm.at[pl.ds(0, TOP_K * tm)], xs_hbm.at[pl.ds(0, TOP_K * tm)], row_sem).wait()


def _permute(fill_start, fill_n, pos, h1, n_rows, tm):
    n = h1.shape[0]
    nt = n // tm
    grid_spec = pltpu.PrefetchScalarGridSpec(
        num_scalar_prefetch=2, grid=(nt,),
        in_specs=[pl.BlockSpec((TOP_K, tm), lambda i, fs, fn: (0, i), memory_space=pltpu.SMEM),
                  pl.BlockSpec((tm, D_MODEL), lambda i, fs, fn: (i, 0))],
        out_specs=pl.BlockSpec(memory_space=pl.ANY),
        scratch_shapes=[pltpu.VMEM((tm, PACKED_W), U32), pltpu.VMEM((EXPERT_BLOCK // 2, PACKED_W), U32),
                        pltpu.SemaphoreType.DMA(()), pltpu.SemaphoreType.DMA(())])
    return pl.pallas_call(
        functools.partial(_permute_kernel, fills_per_step=-(-fill_n.shape[0] // nt)), grid_spec=grid_spec,
        out_shape=jax.ShapeDtypeStruct((n_rows, PACKED_W), U32),
        compiler_params=_cparams(("arbitrary",)),
    )(fill_start, fill_n, pos, h1)


def _experts_kernel(blk_e_ref, nused_ref, x_ref, w1_ref, w3_ref, w2_ref, y_ref, w1b, w3b, w2b):
    j = pl.program_id(0)

    @pl.when(j < nused_ref[0])
    def _():
        first_of_expert = (j == 0) | (blk_e_ref[j] != blk_e_ref[jnp.maximum(j - 1, 0)])

        @pl.when(first_of_expert)
        def _():
            w1b[...] = w1_ref[...].astype(BF)
            w3b[...] = w3_ref[...].astype(BF)
            w2b[...] = w2_ref[...].astype(BF)

        x = jnp.concatenate([half.astype(BF) for half in _unpack_rows(x_ref[...])], axis=1)
        a = _dot(x, w1b[...])
        y_ref[...] = _pack_rows(_dot((a * jax.nn.sigmoid(a) * _dot(x, w3b[...])).astype(BF), w2b[...]))

    @pl.when(j >= nused_ref[0])
    def _():
        y_ref[...] = jnp.zeros(y_ref.shape, y_ref.dtype)


def _experts(blk_e, nused, xs, w1, w3, w2):
    n_blocks = blk_e.shape[0]
    rb = EXPERT_BLOCK
    sq = pl.Squeezed()
    wmap = lambda j, be, nu: (be[j], 0, 0)
    grid_spec = pltpu.PrefetchScalarGridSpec(
        num_scalar_prefetch=2, grid=(n_blocks,),
        in_specs=[pl.BlockSpec((rb, PACKED_W), lambda j, be, nu: (jnp.minimum(j, nu[0] - 1), 0)),
                  pl.BlockSpec((sq, D_MODEL, D_EXPERT), wmap), pl.BlockSpec((sq, D_MODEL, D_EXPERT), wmap),
                  pl.BlockSpec((sq, D_EXPERT, D_MODEL), wmap)],
        out_specs=pl.BlockSpec((rb, PACKED_W), lambda j, be, nu: (j, 0)),
        scratch_shapes=[pltpu.VMEM((D_MODEL, D_EXPERT), BF), pltpu.VMEM((D_MODEL, D_EXPERT), BF),
                        pltpu.VMEM((D_EXPERT, D_MODEL), BF)])
    return pl.pallas_call(
        _experts_kernel, grid_spec=grid_spec,
        out_shape=jax.ShapeDtypeStruct((n_blocks * rb, PACKED_W), U32),
        compiler_params=_cparams(("arbitrary",)),
    )(blk_e, nused, xs, w1, w3, w2)


def _combine_kernel(pos_ref, pos_next_ref, ys_hbm, w_ref, h_ref, s1_ref, s3_ref, s2_ref, g_ref, b_ref, o_ref, ybuf, sem):
    i = pl.program_id(0)
    nt = pl.num_programs(0)
    tm = h_ref.shape[0]

    def start_gather(p_ref, slot):
        def body(t, c):
            for k in range(TOP_K):
                pltpu.make_async_copy(ys_hbm.at[pl.ds(p_ref[k, t], 1)], ybuf.at[slot, k, pl.ds(t, 1)], sem.at[slot]).start()
            return c

        lax.fori_loop(0, tm, body, 0, unroll=ROW_UNROLL)

    @pl.when(i == 0)
    def _():
        start_gather(pos_ref, 0)

    slot = i % 2

    @pl.when(i + 1 < nt)
    def _():
        start_gather(pos_next_ref, 1 - slot)

    h = h_ref[...]
    hb = h.astype(BF)
    a = _dot(hb, s1_ref[...])
    y = ALPHA * h + _dot((a * jax.nn.sigmoid(a) * _dot(hb, s3_ref[...])).astype(BF), s2_ref[...])
    pltpu.make_async_copy(ybuf.at[slot], ybuf.at[slot], sem.at[slot]).wait()
    w = w_ref[...]
    routed = None
    for k in range(TOP_K):
        terms = [w[:, k:k + 1] * half for half in _unpack_rows(ybuf[slot, k])]
        routed = terms if routed is None else [r + t for r, t in zip(routed, terms)]
    y = y + jnp.concatenate(routed, axis=1)
    o_ref[...] = _layer_norm(y, g_ref[...], b_ref[...])


def _combine(pos, ys, w, h1, s1, s3, s2, g, b, tm):
    n = h1.shape[0]
    nt = n // tm
    row = lambda i: (i, 0)
    const = lambda i: (0, 0)
    return pl.pallas_call(
        _combine_kernel, grid=(nt,),
        in_specs=[pl.BlockSpec((TOP_K, tm), lambda i: (0, i), memory_space=pltpu.SMEM),
                  pl.BlockSpec((TOP_K, tm), lambda i: (0, jnp.minimum(i + 1, nt - 1)), memory_space=pltpu.SMEM),
                  pl.BlockSpec(memory_space=pl.ANY),
                  pl.BlockSpec((tm, TOP_K), row), pl.BlockSpec((tm, D_MODEL), row),
                  pl.BlockSpec(s1.shape, const), pl.BlockSpec(s3.shape, const), pl.BlockSpec(s2.shape, const),
                  pl.BlockSpec((1, D_MODEL), const), pl.BlockSpec((1, D_MODEL), const)],
        out_specs=pl.BlockSpec((tm, D_MODEL), row),
        out_shape=jax.ShapeDtypeStruct((n, D_MODEL), F32),
        scratch_shapes=[pltpu.VMEM((2, TOP_K, tm, PACKED_W), U32), pltpu.SemaphoreType.DMA((2,))],
        compiler_params=_cparams(("arbitrary",)),
    )(pos, pos, ys, w, h1, s1, s3, s2, g, b)


def _overlap_matrix(nc, n_slc):
    cs = np.arange(nc) * CMP_STRIDE
    ce = cs + CMP_LEN - 1
    js = np.arange(n_slc) * SLC_LEN
    je = js + SLC_LEN - 1
    ov = ((cs[:, None] <= je[None, :]) & (ce[:, None] >= js[None, :])).astype(np.float32)
    ov[nc - 1] = 0.0
    return ov


def _block_layout(counts, n):
    rb = EXPERT_BLOCK
    counts = counts.astype(I32)
    padded = (counts + rb - 1) // rb * rb
    pad_end = jnp.cumsum(padded)
    pad_start = pad_end - padded
    n_blocks = -(-TOP_K * n // rb) + N_EXPERTS
    blk_start = jnp.arange(n_blocks, dtype=I32) * rb
    blk_e = jnp.minimum(jnp.sum(pad_end[None, :] <= blk_start[:, None], axis=1), N_EXPERTS - 1).astype(I32)
    nused = (pad_end[-1] // rb).astype(I32).reshape(1)
    half = rb // 2
    tail_start = pad_end[-1] + half * jnp.arange(2 * N_EXPERTS, dtype=I32)
    tail_n = jnp.where(tail_start < n_blocks * rb, half, 0).astype(I32)
    fill_start = jnp.concatenate([pad_start + counts, jnp.minimum(tail_start, n_blocks * rb - half)])
    fill_n = jnp.concatenate([padded - counts, tail_n])
    return pad_start, fill_start, fill_n, blk_e, nused


def kernel(x, mem, ln0_g, ln0_b, rel_bias, w_in, cmp_pos_k, cmp_pos_v, cmp_k_w1, cmp_k_w2, cmp_v_w1, cmp_v_w2, mla_q_norm, mla_w_uq, mla_kv_norm, mla_w_ukv, mem_w_kv, w_branch, w_out, ln1_g, ln1_b, router_w, router_b, exp_w1, exp_w3, exp_w2, sh_w1, sh_w3, sh_w2, ln2_g, ln2_b):
    batch, seq, d = x.shape
    n = batch * seq
    l = 0
    row2 = lambda v: v.reshape(1, -1)
    tm = min(256, seq)

    pts = np.cumsum((0,) + IN_SPLITS)
    wcol = lambda k: w_in[l][:, pts[k]:pts[k + 1]]
    pad_cols = lambda w, lo, tot: jnp.pad(w, ((0, 0), (lo, tot - lo - w.shape[1])))
    ws = [wcol(0) * (NSA_DH ** -0.5 * LOG2E), wcol(1), pad_cols(wcol(2), 0, LANE), wcol(3), wcol(4),
          pad_cols(wcol(5), ROPE_LANE0, LANE), wcol(6), wcol(7)]
    ws = [w.astype(BF) for w in ws]
    scales = [1.0] * 6 + [MEM_DH ** -0.5, 1.0]
    gw = NSA_GROUPS * NSA_DH
    ws_t = [wcol(0) * (NSA_DH ** -0.5 * LOG2E), wcol(1)[:, 3 * gw:4 * gw], wcol(1)[:, 5 * gw:6 * gw],
            pad_cols(wcol(2), 0, LANE)]
    ws_t = [w.T.astype(BF) for w in ws_t]
    hd = MLA_NOPE + MLA_ROPE
    wuq = jnp.pad(mla_w_uq[l].reshape(MLA_Q_LORA, MLA_HEADS, hd), ((0, 0), (0, 0), (0, HEAD_PAD - hd)))
    wuq = wuq.reshape(MLA_Q_LORA, MLA_HEADS * HEAD_PAD).astype(BF)
    wukv = mla_w_ukv[l].reshape(MLA_KV_LORA, MLA_HEADS, MLA_NOPE + MLA_DV)
    wuk = jnp.pad(wukv[:, :, :MLA_NOPE], ((0, 0), (0, 0), (0, HEAD_PAD - MLA_NOPE)))
    wuk = wuk.reshape(MLA_KV_LORA, MLA_HEADS * HEAD_PAD).astype(BF)
    wuv = wukv[:, :, MLA_NOPE:].reshape(MLA_KV_LORA, MLA_HEADS * MLA_DV).astype(BF)

    h, qn, kvn, gn, cq, ckv, kr, qm, gm, qn_t, vs_t, vw_t, gn_t = _ln_inproj(
        x.reshape(n, d), row2(ln0_g), row2(ln0_b), ws, scales, ws_t, tm)

    q_mla, k_mla, v_mla = _mla_prep(cq, ckv, kr, row2(mla_q_norm[l]), row2(mla_kv_norm[l]), wuq, wuk, wuv, seq, tm)
    o_mla = _mla_attn(q_mla, k_mla, v_mla, batch, seq, min(512, seq))

    nc = seq // CMP_STRIDE
    n_slc = seq // SLC_LEN
    cc = kvn[:, :2 * gw].reshape(batch, nc, CMP_STRIDE, 2 * NSA_GROUPS, NSA_DH)
    cc = cc.transpose(0, 3, 1, 2, 4).reshape(batch, 2 * NSA_GROUPS, nc, CMP_STRIDE * NSA_DH)
    pos = jnp.stack([cmp_pos_k[l], cmp_pos_v[l]]).reshape(2, 1, CMP_LEN * NSA_DH)
    w1c = jnp.stack([cmp_k_w1[l], cmp_v_w1[l]]).astype(BF)
    w2c = jnp.stack([cmp_k_w2[l], cmp_v_w2[l]]).astype(BF)
    kvc = _nsa_compress(cc, pos, w1c, w2c)
    overlap = jnp.asarray(_overlap_matrix(nc, n_slc), BF)
    o_cmp, sel_t = _nsa_cmp(rel_bias, qn, kvc, overlap, batch, seq)
    o_nsa = _nsa_attn(rel_bias, qn_t, kvn, vs_t, vw_t, sel_t, gn, gn_t, o_cmp, batch, seq)

    o_mem = _mem_attn(qm, mem.reshape(-1, d), mem_w_kv[l].astype(BF), batch, seq, min(512, seq))

    h1 = _merge(o_nsa, o_mla, o_mem, gm, h, w_branch[l].astype(BF), w_out[l].astype(BF),
                row2(ln1_g[l]), row2(ln1_b[l]), tm)

    rw_t = router_w[l].T
    rw_hi = rw_t.astype(BF)
    rw_lo = (rw_t - rw_hi.astype(F32)).astype(BF)
    eidx_t, w_t, rank_t, counts = _router(h1, rw_hi, rw_lo, router_b[l].reshape(N_EXPERTS, 1), tm)
    pad_start, fill_start, fill_n, blk_e, nused = _block_layout(counts[:, 0], n)
    pos = _positions(eidx_t, rank_t, pad_start.astype(F32).reshape(N_EXPERTS, 1), tm)
    xs = _permute(fill_start, fill_n, pos, h1, blk_e.shape[0] * EXPERT_BLOCK, tm)
    ys = _experts(blk_e, nused, xs, exp_w1[l], exp_w3[l], exp_w2[l])
    out = _combine(pos, ys, w_t.T, h1, sh_w1[l].astype(BF), sh_w3[l].astype(BF), sh_w2[l].astype(BF),
                   row2(ln2_g[l]), row2(ln2_b[l]), min(128, seq))
    return out.reshape(batch, seq, d)
```

```python
import functools
import math

import numpy as np
import jax
import jax.numpy as jnp
from jax import lax
from jax.experimental import pallas as pl
from jax.experimental.pallas import tpu as pltpu

BF = jnp.bfloat16
F32 = jnp.float32
I32 = jnp.int32

D_MODEL = 1024
DEPTH = 1
NSA_HEADS = 8
NSA_GROUPS = 2
NSA_HPG = NSA_HEADS // NSA_GROUPS
NSA_DH = 64
CMP_LEN = 32
CMP_STRIDE = 16
CMP_HID = 256
SLC_LEN = 64
SLC_TOPN = 16
WIN = 512
MLA_HEADS = 8
MLA_NOPE = 64
MLA_ROPE = 32
MLA_DV = 64
MLA_Q_LORA = 768
MLA_KV_LORA = 256
ROPE_THETA = 10000.0
MEM_HEADS = 4
MEM_DH = 128
N_BRANCH = 3
BRANCH_W = NSA_HEADS * NSA_DH
REL_BUCKETS = 32
REL_MAX_DIST = 128
N_EXPERTS = 256
TOP_K = 8
N_EXPERT_GROUPS = 8
TOPK_GROUPS = 4
D_EXPERT = 256
ROUTE_SCALE = 2.5
EXPERT_BLOCK = 512
LN_EPS = 1e-5
RMS_EPS = 1e-6
NEG = -1e30
BIG = 1e30
ALPHA = (2 * DEPTH) ** 0.25
IN_SPLITS = (NSA_HEADS * NSA_DH, 6 * NSA_GROUPS * NSA_DH, 3 * NSA_HEADS, MLA_Q_LORA, MLA_KV_LORA,
             MLA_ROPE, MEM_HEADS * MEM_DH, N_BRANCH * D_MODEL)

LANE = 128
SUBLANES = 8
HEAD_PAD = 128
ROPE_LANE0 = MLA_NOPE
ROPE_HALF = MLA_ROPE // 2
VMEM_LIMIT = 56 * 1024 * 1024
NSA_TQ = 128
MLA_TQ = 512
FAR_CHUNK = 512


def _cparams(sem):
    return pltpu.CompilerParams(dimension_semantics=sem, vmem_limit_bytes=VMEM_LIMIT)


def _bucket_starts():
    max_exact = REL_BUCKETS // 2
    d = np.arange(0, 4 * REL_MAX_DIST)
    nf = np.maximum(d, 1).astype(np.float32)
    large = max_exact + (np.log(nf / np.float32(max_exact)) / np.float32(math.log(REL_MAX_DIST / max_exact))
                         * np.float32(REL_BUCKETS - max_exact)).astype(np.int32)
    large = np.minimum(large, REL_BUCKETS - 1)
    bucket = np.where(d < max_exact, d, large)
    return [int(np.argmax(bucket >= b)) for b in range(REL_BUCKETS)]


BUCKET_START = _bucket_starts()
FAR_DIST = BUCKET_START[REL_BUCKETS - 1]


def _rel_bias(dist, tab_ref, head):
    val = jnp.full(dist.shape, tab_ref[REL_BUCKETS - 1, head], F32)
    for b in range(REL_BUCKETS - 2, -1, -1):
        val = jnp.where(dist < BUCKET_START[b + 1], tab_ref[b, head], val)
    return val


def _layer_norm(x, g, b):
    mu = jnp.mean(x, -1, keepdims=True)
    xc = x - mu
    var = jnp.mean(xc * xc, -1, keepdims=True)
    return xc * lax.rsqrt(var + LN_EPS) * g + b


def _dot(a, b):
    return jnp.dot(a, b, preferred_element_type=F32)


def _dot_nt(a, b):
    return lax.dot_general(a, b, (((1,), (1,)), ((), ())), preferred_element_type=F32)


def _ln_inproj_kernel(*refs, scales, n_rowmajor):
    x_ref, g_ref, b_ref = refs[:3]
    n_w = (len(refs) - 4) // 2
    w_refs = refs[3:3 + n_w]
    h_ref = refs[3 + n_w]
    o_refs = refs[4 + n_w:]
    h = _layer_norm(x_ref[...], g_ref[...], b_ref[...])
    h_ref[...] = h
    hb = h.astype(BF)
    for j, (w, o) in enumerate(zip(w_refs, o_refs)):
        if j < n_rowmajor:
            y = _dot(hb, w[...])
            o[...] = (y if scales[j] == 1.0 else y * scales[j]).astype(o.dtype)
        else:
            o[...] = _dot_nt(w[...], hb).astype(o.dtype)


def _ln_inproj(x2, g, b, ws, scales, ws_t, tm):
    n = x2.shape[0]
    row = lambda i: (i, 0)
    col = lambda i: (0, i)
    const = lambda i: (0, 0)
    in_specs = [pl.BlockSpec((tm, D_MODEL), row), pl.BlockSpec((1, D_MODEL), const), pl.BlockSpec((1, D_MODEL), const)]
    in_specs += [pl.BlockSpec(w.shape, const) for w in ws + ws_t]
    out_shape = [jax.ShapeDtypeStruct((n, D_MODEL), F32)]
    out_shape += [jax.ShapeDtypeStruct((n, w.shape[1]), BF) for w in ws]
    out_shape += [jax.ShapeDtypeStruct((w.shape[0], n), BF) for w in ws_t]
    out_specs = [pl.BlockSpec((tm, D_MODEL), row)] + [pl.BlockSpec((tm, w.shape[1]), row) for w in ws]
    out_specs += [pl.BlockSpec((w.shape[0], tm), col) for w in ws_t]
    return pl.pallas_call(
        functools.partial(_ln_inproj_kernel, scales=tuple(scales), n_rowmajor=len(ws)),
        grid=(n // tm,), in_specs=in_specs, out_specs=out_specs, out_shape=out_shape,
        compiler_params=_cparams(("parallel",)),
    )(x2, g, b, *ws, *ws_t)


def _rope_lanes(x, c, s1, s2):
    return x * c + pltpu.roll(x, LANE - ROPE_HALF, 1) * s1 + pltpu.roll(x, ROPE_HALF, 1) * s2


def _mla_prep_kernel(cq_ref, ckv_ref, kr_ref, qn_ref, kvn_ref, wuq, wuk, wuv,
                     cq_t, s1q_t, s2q_t, ck_t, s1k_t, s2k_t, q_out, k_out, v_out):
    cq = cq_ref[...].astype(F32)
    rq = cq * lax.rsqrt(jnp.mean(cq * cq, -1, keepdims=True) + RMS_EPS) * qn_ref[...]
    q = _dot(rq.astype(BF), wuq[...])
    ckv = ckv_ref[...].astype(F32)
    rkv = (ckv * lax.rsqrt(jnp.mean(ckv * ckv, -1, keepdims=True) + RMS_EPS) * kvn_ref[...]).astype(BF)
    kn = _dot(rkv, wuk[...])
    v = _dot(rkv, wuv[...])
    pair_w = 2 * MLA_DV
    ones = jnp.ones((v.shape[0], pair_w), v_out.dtype)
    for p in range(MLA_HEADS // 2):
        v_out[:, 2 * p * pair_w:(2 * p + 1) * pair_w] = v[:, p * pair_w:(p + 1) * pair_w].astype(v_out.dtype)
        v_out[:, (2 * p + 1) * pair_w:(2 * p + 2) * pair_w] = ones
    kr = _rope_lanes(kr_ref[...].astype(F32), ck_t[...], s1k_t[...], s2k_t[...])
    cq_c, s1q, s2q = cq_t[...], s1q_t[...], s2q_t[...]
    for h in range(MLA_HEADS):
        sl = slice(h * HEAD_PAD, (h + 1) * HEAD_PAD)
        q_out[:, sl] = _rope_lanes(q[:, sl], cq_c, s1q, s2q).astype(q_out.dtype)
        k_out[:, sl] = (kn[:, sl] + kr).astype(k_out.dtype)


def _rope_tables(seq, scale):
    freq = ROPE_THETA ** (-jnp.arange(ROPE_HALF, dtype=F32) / ROPE_HALF)
    ang = jnp.arange(seq, dtype=F32)[:, None] * freq[None, :]
    cos, sin = jnp.cos(ang) * scale, jnp.sin(ang) * scale
    z = lambda w: jnp.zeros((seq, w), F32)
    tail = HEAD_PAD - ROPE_LANE0 - MLA_ROPE
    c = jnp.concatenate([jnp.full((seq, ROPE_LANE0), scale, F32), cos, cos, z(tail)], 1)
    s1 = jnp.concatenate([z(ROPE_LANE0), -sin, z(ROPE_HALF + tail)], 1)
    s2 = jnp.concatenate([z(ROPE_LANE0 + ROPE_HALF), sin, z(tail)], 1)
    return c, s1, s2


def _mla_prep(cq, ckv, kr, q_norm, kv_norm, wuq, wuk, wuv, seq, tm):
    n = cq.shape[0]
    nt = seq // tm
    row = lambda i: (i, 0)
    const = lambda i: (0, 0)
    pos = lambda i: (i % nt, 0)
    tabs = _rope_tables(seq, (MLA_NOPE + MLA_ROPE) ** -0.5 * math.log2(math.e)) + _rope_tables(seq, 1.0)
    in_specs = [pl.BlockSpec((tm, MLA_Q_LORA), row), pl.BlockSpec((tm, MLA_KV_LORA), row), pl.BlockSpec((tm, LANE), row),
                pl.BlockSpec((1, MLA_Q_LORA), const), pl.BlockSpec((1, MLA_KV_LORA), const),
                pl.BlockSpec(wuq.shape, const), pl.BlockSpec(wuk.shape, const), pl.BlockSpec(wuv.shape, const)]
    in_specs += [pl.BlockSpec((tm, LANE), pos)] * 6
    hq = MLA_HEADS * HEAD_PAD
    hv = 2 * MLA_HEADS * MLA_DV
    return pl.pallas_call(
        _mla_prep_kernel, grid=(n // tm,), in_specs=in_specs,
        out_specs=[pl.BlockSpec((tm, hq), row), pl.BlockSpec((tm, hq), row), pl.BlockSpec((tm, hv), row)],
        out_shape=[jax.ShapeDtypeStruct((n, hq), BF), jax.ShapeDtypeStruct((n, hq), BF), jax.ShapeDtypeStruct((n, hv), BF)],
        compiler_params=_cparams(("parallel",)),
    )(cq, ckv, kr, q_norm, kv_norm, wuq, wuk, wuv, *tabs)


def _mla_attn_kernel(q_ref, k_ref, v_ref, o_ref, sa_scr, sb_scr, m_scr, acc_scr):
    i = pl.program_id(2)
    t = q_ref.shape[0]
    reps = t // LANE
    pair_w = 2 * MLA_DV
    m_scr[...] = jnp.full(m_scr.shape, NEG, F32)
    acc_scr[...] = jnp.zeros(acc_scr.shape, F32)

    def logits(tile, s_scr):
        kstart = pl.multiple_of(tile * t, t)
        for hh in range(2):
            sl = slice(hh * HEAD_PAD, (hh + 1) * HEAD_PAD)
            s_scr[hh] = _dot_nt(q_ref[:, sl], k_ref[pl.ds(kstart, t), sl])

    def consume(tile, s_scr, diagonal):
        v = v_ref[pl.ds(pl.multiple_of(tile * t, t), t), :]
        for hh in range(2):
            s = s_scr[hh]
            if diagonal:
                row = lax.broadcasted_iota(I32, (t, t), 0)
                col = lax.broadcasted_iota(I32, (t, t), 1)
                s = jnp.where(col <= row, s, NEG)
            m_prev = m_scr[hh]
            m_new = jnp.maximum(m_prev, jnp.max(s, 1, keepdims=True))
            a = jnp.exp2(m_prev - m_new)
            e = jnp.exp2(s - jnp.tile(m_new, (1, reps)))
            acc_scr[hh] = jnp.tile(a, (1, 2)) * acc_scr[hh] + _dot(e.astype(BF), v)
            m_scr[hh] = m_new

    logits(0, sa_scr)

    def body(p, carry):
        logits(2 * p + 1, sb_scr)
        consume(2 * p, sa_scr, False)
        logits(2 * p + 2, sa_scr)
        consume(2 * p + 1, sb_scr, False)
        return carry

    lax.fori_loop(0, i // 2, body, 0)

    @pl.when(i % 2 == 0)
    def _():
        consume(i, sa_scr, True)

    @pl.when(i % 2 == 1)
    def _():
        logits(i, sb_scr)
        consume(i - 1, sa_scr, False)
        consume(i, sb_scr, True)

    lane = lax.broadcasted_iota(I32, (t, pair_w), 1)
    o = jnp.where(lane < MLA_DV, acc_scr[0, :, :pair_w] / acc_scr[0, :, pair_w:],
                  acc_scr[1, :, :pair_w] / acc_scr[1, :, pair_w:])
    o_ref[...] = o.astype(o_ref.dtype)


def _mla_attn(q, k, v, batch, seq, t):
    n = q.shape[0]
    nt = seq // t
    qmap = lambda b, hp, i: (b * nt + i, hp)
    kmap = lambda b, hp, i: (b, hp)
    return pl.pallas_call(
        _mla_attn_kernel, grid=(batch, MLA_HEADS // 2, nt),
        in_specs=[pl.BlockSpec((t, 2 * HEAD_PAD), qmap), pl.BlockSpec((seq, 2 * HEAD_PAD), kmap),
                  pl.BlockSpec((seq, 4 * MLA_DV), kmap)],
        out_specs=pl.BlockSpec((t, 2 * MLA_DV), qmap),
        out_shape=jax.ShapeDtypeStruct((n, MLA_HEADS * MLA_DV), BF),
        scratch_shapes=[pltpu.VMEM((2, t, t), F32), pltpu.VMEM((2, t, t), F32),
                        pltpu.VMEM((2, t, LANE), F32), pltpu.VMEM((2, t, 4 * MLA_DV), F32)],
        compiler_params=_cparams(("parallel", "parallel", "arbitrary")),
    )(q, k, v)


def _nsa_compress_kernel(c_ref, pos_ref, w1_ref, w2_ref, o_ref):
    nc = c_ref.shape[0]
    half = CMP_STRIDE * NSA_DH
    c = c_ref[...]
    top = _dot(c, w1_ref[:half, :])
    bot = _dot(c, w1_ref[half:, :])
    posb = _dot(jnp.broadcast_to(pos_ref[...], (8, 2 * half)).astype(BF), w1_ref[...])[:1]
    hid = top + pltpu.roll(bot, nc - 1, 0) + posb
    o_ref[...] = _dot(jax.nn.gelu(hid).astype(BF), w2_ref[...]).astype(o_ref.dtype)


def _nsa_compress(cc, pos, w1, w2):
    b, _, nc, half = cc.shape
    sq = pl.Squeezed()
    return pl.pallas_call(
        _nsa_compress_kernel, grid=(b, 2 * NSA_GROUPS),
        in_specs=[pl.BlockSpec((sq, sq, nc, half), lambda i, c: (i, c, 0, 0)),
                  pl.BlockSpec((sq, 1, 2 * half), lambda i, c: (c // NSA_GROUPS, 0, 0)),
                  pl.BlockSpec((sq, 2 * half, CMP_HID), lambda i, c: (c // NSA_GROUPS, 0, 0)),
                  pl.BlockSpec((sq, CMP_HID, NSA_DH), lambda i, c: (c // NSA_GROUPS, 0, 0))],
        out_specs=pl.BlockSpec((sq, sq, nc, NSA_DH), lambda i, c: (i, c, 0, 0)),
        out_shape=jax.ShapeDtypeStruct((b, 2 * NSA_GROUPS, nc, NSA_DH), BF),
        compiler_params=_cparams(("parallel", "parallel")),
    )(cc, pos, w1, w2)


CMP_TQ = 512
CMP_BIAS_COLS = LANE // 2
CMP_BIAS_BACK = -(-(FAR_DIST + CMP_LEN - 1) // CMP_STRIDE)
assert (CMP_TQ - CMP_LEN) // CMP_STRIDE + CMP_BIAS_BACK < CMP_BIAS_COLS


def _nsa_cmp_kernel(tab_ref, q_ref, kc_ref, vc_ref, ov_ref, oc_ref, sel_ref, e_scr, *, n_top):
    b, g, i = pl.program_id(0), pl.program_id(1), pl.program_id(2)
    tq = q_ref.shape[0]
    nc = kc_ref.shape[0]
    n_slc = ov_ref.shape[1]
    qs = i * tq

    @pl.when((b == 0) & (g == 0) & (i == 0))
    def _():
        q_io = lax.broadcasted_iota(I32, (tq, LANE), 0)
        lane = lax.broadcasted_iota(I32, (tq, LANE), 1)
        jj = lane % CMP_BIAS_COLS
        dist = q_io - CMP_STRIDE * (jj - CMP_BIAS_BACK) - (CMP_LEN - 1)
        live = (dist >= 0) & (lane < 2 * CMP_BIAS_COLS)
        for h in range(NSA_HEADS):
            e = jnp.where(live, (_rel_bias(dist, tab_ref, h) - tab_ref[REL_BUCKETS - 1, h]) * LOG2E, 0.0)
            hi = e.astype(BF)
            lo = (e - hi.astype(F32)).astype(BF)
            e_scr[h] = jnp.where(lane < CMP_BIAS_COLS, hi, lo)

    n0 = qs // CMP_STRIDE
    jrow = lax.broadcasted_iota(I32, (LANE, nc), 0)
    ncol = lax.broadcasted_iota(I32, (LANE, nc), 1)
    ft = jnp.where((ncol == n0 + (jrow % CMP_BIAS_COLS) - CMP_BIAS_BACK) & (jrow < 2 * CMP_BIAS_COLS), 1.0, 0.0).astype(BF)

    t = qs + lax.broadcasted_iota(I32, (tq, nc), 0)
    n_io = lax.broadcasted_iota(I32, (tq, nc), 1)
    mask_add = jnp.where((t >= n_io * CMP_STRIDE + (CMP_LEN - 1)) & (n_io < nc - 1), 0.0, NEG)
    has_key = qs + lax.broadcasted_iota(I32, (tq, 1), 0) >= CMP_LEN - 1
    kc = kc_ref[...]
    vc = vc_ref[...]
    psum = jnp.zeros((tq, nc), F32)
    for h in range(NSA_HPG):
        head = g * NSA_HPG + h
        qh = q_ref[:, h * NSA_DH:(h + 1) * NSA_DH]
        s = _dot_nt(qh, kc) + _dot(e_scr[head], ft) + mask_add
        e = jnp.exp2(s - jnp.max(s, 1, keepdims=True))
        p = e * jnp.where(has_key, 1.0 / jnp.sum(e, 1, keepdims=True), 0.0)
        oc_ref[:, h * NSA_DH:(h + 1) * NSA_DH] = _dot(p.astype(BF), vc).astype(oc_ref.dtype)
        psum = psum + p

    ov = ov_ref[...]
    p0 = psum.astype(BF)
    r1 = psum - p0.astype(F32)
    p1 = r1.astype(BF)
    p2 = (r1 - p1.astype(F32)).astype(BF)
    imp = (_dot(p0, ov) + _dot(p1, ov) + _dot(p2, ov)).T

    tj = qs + lax.broadcasted_iota(I32, (n_slc, tq), 1)
    j = lax.broadcasted_iota(I32, (n_slc, tq), 0)
    cur = tj // SLC_LEN
    forced = (j == 0) | (j == cur) | (j == cur - 1)
    work = jnp.where(j * SLC_LEN > tj, NEG, jnp.where(forced, BIG, imp))
    sel = jnp.full((n_slc, tq), NEG, F32)
    for _ in range(n_top):
        mx, first = _first_max(work, j, n_slc)
        hit = j == first
        sel = jnp.where(hit & (mx > 0.5 * NEG), 0.0, sel)
        work = jnp.where(hit, -jnp.inf, work)
    sel_ref[...] = sel


def _nsa_cmp(tab, qn, kvc, overlap, batch, seq):
    n = qn.shape[0]
    tq = min(CMP_TQ, seq)
    nq = seq // tq
    nc = kvc.shape[2]
    n_slc = overlap.shape[1]
    sq = pl.Squeezed()
    gw = NSA_HPG * NSA_DH
    return pl.pallas_call(
        functools.partial(_nsa_cmp_kernel, n_top=min(SLC_TOPN, n_slc)), grid=(batch, NSA_GROUPS, nq),
        in_specs=[pl.BlockSpec(memory_space=pltpu.SMEM),
                  pl.BlockSpec((tq, gw), lambda b, g, i: (b * nq + i, g)),
                  pl.BlockSpec((sq, sq, nc, NSA_DH), lambda b, g, i: (b, g, 0, 0)),
                  pl.BlockSpec((sq, sq, nc, NSA_DH), lambda b, g, i: (b, NSA_GROUPS + g, 0, 0)),
                  pl.BlockSpec((nc, n_slc), lambda b, g, i: (0, 0))],
        out_specs=[pl.BlockSpec((tq, gw), lambda b, g, i: (b * nq + i, g)),
                   pl.BlockSpec((sq, n_slc, tq), lambda b, g, i: (g, 0, b * nq + i))],
        out_shape=[jax.ShapeDtypeStruct((n, NSA_HEADS * NSA_DH), BF),
                   jax.ShapeDtypeStruct((NSA_GROUPS, n_slc, n), F32)],
        scratch_shapes=[pltpu.VMEM((NSA_HEADS, tq, LANE), BF)],
        compiler_params=_cparams(("arbitrary", "arbitrary", "arbitrary")),
    )(tab, qn, kvc, kvc, overlap)


LOG2E = math.log2(math.e)
DEN_ROWS = 16


def _nsa_attn_kernel(tab_ref, qt_ref, ks_ref, kw_ref, vst_ref, vwt_ref, sel_ref, gate_ref, gate_t_ref, oc_ref, o_ref,
                     d_scr, q_scr, sa_scr, sb_scr, sc_scr, m_scr, acc_scr):
    b, i = pl.program_id(0), pl.program_id(1)
    tq = NSA_TQ
    lanes = NSA_HPG * tq
    qs = i * tq
    near_w = 2 * tq
    far_w = FAR_CHUNK
    win_far_w = WIN - tq

    @pl.when((b == 0) & (i == 0))
    def _():
        kk = lax.broadcasted_iota(I32, (near_w, tq), 0)
        q_io = lax.broadcasted_iota(I32, (near_w, tq), 1)
        dist = jnp.maximum(q_io + tq - kk, 0)
        for g in range(NSA_GROUPS):
            d_scr[g] = jnp.concatenate(
                [(_rel_bias(dist, tab_ref, g * NSA_HPG + h) - tab_ref[REL_BUCKETS - 1, g * NSA_HPG + h]) * LOG2E
                 for h in range(NSA_HPG)], axis=1)

    qt = qt_ref[...]
    for g in range(NSA_GROUPS):
        qg = jnp.concatenate([qt[(g * NSA_HPG + h) * NSA_DH:(g * NSA_HPG + h + 1) * NSA_DH, :]
                              for h in range(NSA_HPG)], axis=1)
        parts = [jnp.zeros((NSA_DH, lanes), BF)] * NSA_GROUPS
        parts[g] = qg
        q_scr[g] = jnp.concatenate(parts, axis=0)

    def update(slot, s, vt):
        m_prev = m_scr[slot]
        m_new = jnp.maximum(m_prev, jnp.max(s, 0, keepdims=True))
        a = jnp.exp2(m_prev - m_new)
        e = jnp.exp2(s - m_new)
        vt_den = jnp.concatenate([vt, jnp.ones((DEN_ROWS, vt.shape[1]), BF)], axis=0)
        acc_scr[slot] = a * acc_scr[slot] + _dot(vt_den, e.astype(BF))
        m_scr[slot] = m_new

    def all_heads(x):
        return jnp.tile(x, (1, NSA_HPG))

    def sel_rows(g, first_block, n_blk):
        return jnp.concatenate(
            [jnp.broadcast_to(sel_ref[g, pl.ds(jnp.maximum(first_block + r, 0), 1), :], (SLC_LEN, tq))
             for r in range(n_blk)], axis=0)

    m_scr[...] = jnp.full(m_scr.shape, NEG, F32)
    acc_scr[...] = jnp.zeros(acc_scr.shape, F32)

    kk = lax.broadcasted_iota(I32, (near_w, tq), 0)
    q_io = lax.broadcasted_iota(I32, (near_w, tq), 1)
    kpos_near = qs - tq + kk
    causal_add = jnp.where((kpos_near >= 0) & (kpos_near <= qs + q_io), 0.0, NEG)
    start_a = pl.multiple_of(qs, tq)
    start_b = pl.multiple_of(jnp.maximum(qs - tq, 0), tq)

    def near_logits(br, s_scr):
        k_ref = ks_ref if br == 0 else kw_ref
        k = jnp.concatenate([k_ref[pl.ds(start_b, tq), :], k_ref[pl.ds(start_a, tq), :]], axis=0)
        for g in range(NSA_GROUPS):
            add = causal_add + sel_rows(g, 2 * i - 2, near_w // SLC_LEN) if br == 0 else causal_add
            s_scr[g, :near_w] = _dot(k, q_scr[g]) + d_scr[g] + all_heads(add)

    def near_consume(br, s_scr):
        vt_ref = vst_ref if br == 0 else vwt_ref
        for g in range(NSA_GROUPS):
            gs = slice(g * NSA_DH, (g + 1) * NSA_DH)
            vt = jnp.concatenate([vt_ref[gs, pl.ds(start_b, tq)], vt_ref[gs, pl.ds(start_a, tq)]], axis=1)
            update(2 * g + br, s_scr[g, :near_w], vt)

    ws = pl.multiple_of(jnp.maximum(qs - WIN, 0), tq)

    def win_logits(s_scr):
        kpos_w = ws + lax.broadcasted_iota(I32, (win_far_w, tq), 0)
        t_w = qs + lax.broadcasted_iota(I32, (win_far_w, tq), 1)
        add_w = all_heads(jnp.where((kpos_w < qs - tq) & (kpos_w > t_w - WIN), 0.0, NEG))
        k_w = kw_ref[pl.ds(ws, win_far_w), :]
        for g in range(NSA_GROUPS):
            s_scr[g, :win_far_w] = _dot(k_w, q_scr[g]) + add_w

    def win_consume(s_scr):
        for g in range(NSA_GROUPS):
            update(2 * g + 1, s_scr[g, :win_far_w], vwt_ref[g * NSA_DH:(g + 1) * NSA_DH, pl.ds(ws, win_far_w)])

    n_far = (jnp.maximum(i - 1, 0) + (far_w // tq - 1)) // (far_w // tq)
    last_chunk = ks_ref.shape[0] // far_w - 1
    krow = lax.broadcasted_iota(I32, (far_w, tq), 0)

    def far_logits(c, s_scr):
        c = jnp.minimum(c, last_chunk)
        base = pl.multiple_of(c * far_w, far_w)
        in_range = jnp.where(base + krow < qs - tq, 0.0, NEG)
        k = ks_ref[pl.ds(base, far_w), :]
        for g in range(NSA_GROUPS):
            add = all_heads(sel_rows(g, c * (far_w // SLC_LEN), far_w // SLC_LEN) + in_range)
            s_scr[g] = _dot(k, q_scr[g]) + add

    def far_consume(c, s_scr):
        base = pl.multiple_of(c * far_w, far_w)
        for g in range(NSA_GROUPS):
            update(2 * g, s_scr[g], vst_ref[g * NSA_DH:(g + 1) * NSA_DH, pl.ds(base, far_w)])

    near_logits(0, sc_scr)
    near_logits(1, sb_scr)
    near_consume(0, sc_scr)
    win_logits(sc_scr)
    near_consume(1, sb_scr)
    far_logits(0, sa_scr)
    win_consume(sc_scr)

    def far_body(p, carry):
        far_logits(2 * p + 1, sb_scr)
        far_consume(2 * p, sa_scr)
        far_logits(2 * p + 2, sa_scr)
        far_consume(2 * p + 1, sb_scr)
        return carry

    lax.fori_loop(0, n_far // 2, far_body, 0)

    @pl.when(n_far % 2 == 1)
    def _():
        far_consume(n_far - 1, sa_scr)

    gates = jax.nn.sigmoid(gate_ref[...].astype(F32))
    gates_t = jax.nn.sigmoid(gate_t_ref[...].astype(F32))
    for g in range(NSA_GROUPS):
        o_s = acc_scr[2 * g, :NSA_DH] / acc_scr[2 * g, NSA_DH:NSA_DH + 1]
        o_w = acc_scr[2 * g + 1, :NSA_DH] / acc_scr[2 * g + 1, NSA_DH:NSA_DH + 1]
        for h in range(NSA_HPG):
            head = g * NSA_HPG + h
            hl = slice(head * NSA_DH, (head + 1) * NSA_DH)
            cl = slice(h * tq, (h + 1) * tq)
            sw_t = (gates_t[3 * head + 1:3 * head + 2, :] * o_s[:, cl]
                    + gates_t[3 * head + 2:3 * head + 3, :] * o_w[:, cl])
            o = gates[:, 3 * head:3 * head + 1] * oc_ref[:, hl].astype(F32) + sw_t.T
            o_ref[:, hl] = o.astype(o_ref.dtype)


def _nsa_attn(tab, qn_t, kvn, vs_t, vw_t, sel_t, gn, gn_t, oc, batch, seq):
    n = kvn.shape[0]
    tq = NSA_TQ
    nq = seq // tq
    n_slc = sel_t.shape[1]
    hw = NSA_HEADS * NSA_DH
    gw = NSA_GROUPS * NSA_DH
    lanes = NSA_HPG * tq
    row = lambda b, i: (b * nq + i, 0)
    col = lambda b, i: (0, b * nq + i)
    k_spec = lambda kind: pl.BlockSpec((seq, gw), lambda b, i: (b, kind))
    vt_spec = pl.BlockSpec((gw, seq), lambda b, i: (0, b))
    n_state = 2 * NSA_GROUPS
    return pl.pallas_call(
        _nsa_attn_kernel, grid=(batch, nq),
        in_specs=[pl.BlockSpec(memory_space=pltpu.SMEM), pl.BlockSpec((hw, tq), col),
                  k_spec(2), k_spec(4), vt_spec, vt_spec,
                  pl.BlockSpec((NSA_GROUPS, n_slc, tq), lambda b, i: (0, 0, b * nq + i)),
                  pl.BlockSpec((tq, LANE), row), pl.BlockSpec((LANE, tq), col), pl.BlockSpec((tq, hw), row)],
        out_specs=pl.BlockSpec((tq, hw), row),
        out_shape=jax.ShapeDtypeStruct((n, hw), BF),
        scratch_shapes=[pltpu.VMEM((NSA_GROUPS, 2 * tq, lanes), F32), pltpu.VMEM((NSA_GROUPS, gw, lanes), BF),
                        pltpu.VMEM((NSA_GROUPS, FAR_CHUNK, lanes), F32), pltpu.VMEM((NSA_GROUPS, FAR_CHUNK, lanes), F32),
                        pltpu.VMEM((NSA_GROUPS, FAR_CHUNK, lanes), F32),
                        pltpu.VMEM((n_state, 1, lanes), F32), pltpu.VMEM((n_state, NSA_DH + DEN_ROWS, lanes), F32)],
        compiler_params=_cparams(("arbitrary", "arbitrary")),
    )(tab, qn_t, kvn, kvn, vs_t, vw_t, sel_t, gn, gn_t, oc)


def _mem_attn_kernel(q_ref, mem_ref, w_ref, o_ref, kv_scr):
    @pl.when(pl.program_id(1) == 0)
    def _():
        kv_scr[...] = _dot(mem_ref[...].astype(BF), w_ref[...]).astype(BF)

    hw = MEM_HEADS * MEM_DH
    for h in range(MEM_HEADS):
        sl = slice(h * MEM_DH, (h + 1) * MEM_DH)
        s = _dot_nt(q_ref[:, sl], kv_scr[:, sl])
        e = jnp.exp(s - jnp.max(s, 1, keepdims=True))
        p = e / jnp.sum(e, 1, keepdims=True)
        o_ref[:, sl] = _dot(p.astype(BF), kv_scr[:, hw + h * MEM_DH:hw + (h + 1) * MEM_DH]).astype(o_ref.dtype)


def _mem_attn(qm, mem2, w_kv, batch, seq, tq):
    n = qm.shape[0]
    nq = seq // tq
    m = mem2.shape[0] // batch
    hw = MEM_HEADS * MEM_DH
    return pl.pallas_call(
        _mem_attn_kernel, grid=(batch, nq),
        in_specs=[pl.BlockSpec((tq, hw), lambda b, i: (b * nq + i, 0)),
                  pl.BlockSpec((m, D_MODEL), lambda b, i: (b, 0)),
                  pl.BlockSpec((D_MODEL, 2 * hw), lambda b, i: (0, 0))],
        out_specs=pl.BlockSpec((tq, hw), lambda b, i: (b * nq + i, 0)),
        out_shape=jax.ShapeDtypeStruct((n, hw), BF),
        scratch_shapes=[pltpu.VMEM((m, 2 * hw), BF)],
        compiler_params=_cparams(("arbitrary", "arbitrary")),
    )(qm, mem2, w_kv)


def _merge_kernel(on_ref, ol_ref, om_ref, gm_ref, h_ref, wb_ref, wo_ref, g_ref, b_ref, h1_ref):
    merged = None
    for c, o_ref in enumerate((on_ref, ol_ref, om_ref)):
        gate = jax.nn.sigmoid(gm_ref[:, c * D_MODEL:(c + 1) * D_MODEL].astype(F32))
        term = gate * _dot(o_ref[...], wb_ref[c])
        merged = term if merged is None else merged + term
    y = ALPHA * h_ref[...] + _dot(merged.astype(BF), wo_ref[...])
    h1_ref[...] = _layer_norm(y, g_ref[...], b_ref[...])


def _merge(o_nsa, o_mla, o_mem, gm, h, wb, wo, g, b, tm):
    n = h.shape[0]
    row = lambda i: (i, 0)
    const = lambda i: (0, 0)
    return pl.pallas_call(
        _merge_kernel, grid=(n // tm,),
        in_specs=[pl.BlockSpec((tm, BRANCH_W), row)] * 3 + [
            pl.BlockSpec((tm, N_BRANCH * D_MODEL), row), pl.BlockSpec((tm, D_MODEL), row),
            pl.BlockSpec(wb.shape, lambda i: (0, 0, 0)), pl.BlockSpec(wo.shape, const),
            pl.BlockSpec((1, D_MODEL), const), pl.BlockSpec((1, D_MODEL), const)],
        out_specs=pl.BlockSpec((tm, D_MODEL), row),
        out_shape=jax.ShapeDtypeStruct((n, D_MODEL), F32),
        compiler_params=_cparams(("parallel",)),
    )(o_nsa, o_mla, o_mem, gm, h, wb, wo, g, b)


def _first_max(vals, idx, limit):
    mx = jnp.max(vals, 0, keepdims=True)
    first = jnp.min(jnp.where(vals == mx, idx, limit), 0, keepdims=True)
    return mx, first


def _router_kernel(h_ref, whi_ref, wlo_ref, b_ref, tri_ref, idx_ref, w_ref, rank_ref, cnt_ref):
    @pl.when(pl.program_id(0) == 0)
    def _():
        cnt_ref[...] = jnp.zeros(cnt_ref.shape, F32)

    h = h_ref[...]
    hhi = h.astype(BF)
    hlo = (h - hhi.astype(F32)).astype(BF)
    whi = whi_ref[...]
    logits = _dot_nt(whi, hhi) + _dot_nt(whi, hlo) + _dot_nt(wlo_ref[...], hhi)
    s = jax.nn.sigmoid(logits)
    sb = s + b_ref[...]
    tm = s.shape[1]
    gsz = N_EXPERTS // N_EXPERT_GROUPS
    e_io = lax.broadcasted_iota(I32, (gsz, tm), 0)
    scores = []
    for g in range(N_EXPERT_GROUPS):
        vals = sb[g * gsz:(g + 1) * gsz]
        m1, first = _first_max(vals, e_io, gsz)
        m2 = jnp.max(jnp.where(e_io == first, -jnp.inf, vals), 0, keepdims=True)
        scores.append(m1 + m2)
    gs = jnp.concatenate(scores, axis=0)
    g_io = lax.broadcasted_iota(I32, (N_EXPERT_GROUPS, tm), 0)
    x_io = lax.broadcasted_iota(I32, (N_EXPERTS, tm), 0)
    allowed = jnp.zeros((N_EXPERTS, tm), jnp.bool_)
    for _ in range(TOPK_GROUPS):
        _, first = _first_max(gs, g_io, N_EXPERT_GROUPS)
        gs = jnp.where(g_io == first, -jnp.inf, gs)
        allowed = allowed | (x_io // gsz == first)
    work = jnp.where(allowed, sb, NEG)
    base = cnt_ref[:, :1]
    tri = tri_ref[...]
    idxs, ws, ranks = [], [], []
    for _ in range(TOP_K):
        _, first = _first_max(work, x_io, N_EXPERTS)
        hit = x_io == first
        idxs.append(first)
        ws.append(jnp.sum(jnp.where(hit, s, 0.0), 0, keepdims=True))
        work = jnp.where(hit, -jnp.inf, work)
        onehot = jnp.where(hit, 1.0, 0.0)
        before = _dot(onehot.astype(BF), tri)
        ranks.append(jnp.sum(jnp.where(hit, base + before, 0.0), 0, keepdims=True))
        base = base + jnp.sum(onehot, 1, keepdims=True)
    wsel = jnp.concatenate(ws, axis=0)
    idx_ref[...] = jnp.concatenate(idxs, axis=0)
    w_ref[...] = wsel / jnp.sum(wsel, 0, keepdims=True) * ROUTE_SCALE
    rank_ref[...] = jnp.concatenate(ranks, axis=0).astype(I32)
    cnt_ref[...] = jnp.broadcast_to(base, cnt_ref.shape)


def _router(h1, whi, wlo, rb, tm):
    n = h1.shape[0]
    tri = jnp.asarray(np.triu(np.ones((tm, tm), np.float32), 1), BF)
    slot = pl.BlockSpec((TOP_K, tm), lambda i: (0, i))
    const = lambda i: (0, 0)
    return pl.pallas_call(
        _router_kernel, grid=(n // tm,),
        in_specs=[pl.BlockSpec((tm, D_MODEL), lambda i: (i, 0)),
                  pl.BlockSpec((N_EXPERTS, D_MODEL), const), pl.BlockSpec((N_EXPERTS, D_MODEL), const),
                  pl.BlockSpec((N_EXPERTS, 1), const), pl.BlockSpec((tm, tm), const)],
        out_specs=[slot, slot, slot, pl.BlockSpec((N_EXPERTS, LANE), const)],
        out_shape=[jax.ShapeDtypeStruct((TOP_K, n), I32), jax.ShapeDtypeStruct((TOP_K, n), F32),
                   jax.ShapeDtypeStruct((TOP_K, n), I32), jax.ShapeDtypeStruct((N_EXPERTS, LANE), F32)],
        compiler_params=_cparams(("arbitrary",)),
    )(h1, whi, wlo, rb, tri)


def _pos_kernel(idx_ref, rank_ref, start_ref, pos_ref):
    tm = idx_ref.shape[1]
    x_io = lax.broadcasted_iota(I32, (N_EXPERTS, tm), 0)
    start = start_ref[...]
    rows = [jnp.sum(jnp.where(x_io == idx_ref[k:k + 1, :], start, 0.0), 0, keepdims=True) for k in range(TOP_K)]
    pos_ref[...] = jnp.concatenate(rows, axis=0).astype(I32) + rank_ref[...]


def _positions(eidx_t, rank_t, pad_start, tm):
    n = eidx_t.shape[1]
    slot = pl.BlockSpec((TOP_K, tm), lambda i: (0, i))
    return pl.pallas_call(
        _pos_kernel, grid=(n // tm,),
        in_specs=[slot, slot, pl.BlockSpec((N_EXPERTS, 1), lambda i: (0, 0))],
        out_specs=slot, out_shape=jax.ShapeDtypeStruct((TOP_K, n), I32),
        compiler_params=_cparams(("parallel",)),
    )(eidx_t, rank_t, pad_start)


ROW_UNROLL = 8
PACKED_W = D_MODEL // 2
U32 = jnp.uint32


def _fill_groups():
    p = EXPERT_BLOCK // 2
    while p >= 1:
        yield p
        p //= 2


def _pack_rows(x):
    bits = lambda v: lax.bitcast_convert_type(v.astype(BF).astype(F32), U32)
    return (bits(x[:, :PACKED_W]) >> 16) | (bits(x[:, PACKED_W:]) & U32(0xFFFF0000))


def _unpack_rows(u):
    return (lax.bitcast_convert_type(u << 16, F32), lax.bitcast_convert_type(u & U32(0xFFFF0000), F32))


def _permute_kernel(fill_start_ref, fill_n_ref, pos_ref, h_ref, xs_hbm, xp, zbuf, row_sem, fill_sem, *, fills_per_step):
    i = pl.program_id(0)
    tm = h_ref.shape[0]
    n_fills = fill_n_ref.shape[0]
    zbuf[...] = jnp.zeros(zbuf.shape, zbuf.dtype)
    xp[...] = _pack_rows(h_ref[...])

    def fill_copies(e):
        n = fill_n_ref[e]
        start = fill_start_ref[e]
        for p in _fill_groups():
            @pl.when((n & p) != 0)
            def _(p=p):
                off = start + (n & (p - 1))
                if p < SUBLANES:
                    for r in range(p):
                        pltpu.make_async_copy(zbuf.at[pl.ds(r, 1)], xs_hbm.at[pl.ds(off + r, 1)], fill_sem).start()
                else:
                    pltpu.make_async_copy(zbuf.at[pl.ds(0, p)], xs_hbm.at[pl.ds(pl.multiple_of(off, SUBLANES), p)],
                                          fill_sem).start()

    def fill_waits(e):
        n = fill_n_ref[e]
        for p in _fill_groups():
            @pl.when((n & p) != 0)
            def _(p=p):
                pltpu.make_async_copy(zbuf.at[pl.ds(0, p)], zbuf.at[pl.ds(0, p)], fill_sem).wait()

    for q in range(fills_per_step):
        e = i * fills_per_step + q

        @pl.when(e < n_fills)
        def _(e=e):
            fill_copies(e)

    for t in range(tm):
        for k in range(TOP_K):
            pltpu.make_async_copy(xp.at[pl.ds(t, 1)], xs_hbm.at[pl.ds(pos_ref[k, t], 1)], row_sem).start()

    for q in range(fills_per_step):
        e = i * fills_per_step + q

        @pl.when(e < n_fills)
        def _(e=e):
            fill_waits(e)

    pltpu.make_async_copy(xs_hbm.at[pl.ds(0, TOP_K * tm)], xs_hbm.at[pl.ds(0, TOP_K * tm)], row_sem).wait()


def _permute(fill_start, fill_n, pos, h1, n_rows, tm):
    n = h1.shape[0]
    nt = n // tm
    grid_spec = pltpu.PrefetchScalarGridSpec(
        num_scalar_prefetch=2, grid=(nt,),
        in_specs=[pl.BlockSpec((TOP_K, tm), lambda i, fs, fn: (0, i), memory_space=pltpu.SMEM),
                  pl.BlockSpec((tm, D_MODEL), lambda i, fs, fn: (i, 0))],
        out_specs=pl.BlockSpec(memory_space=pl.ANY),
        scratch_shapes=[pltpu.VMEM((tm, PACKED_W), U32), pltpu.VMEM((EXPERT_BLOCK // 2, PACKED_W), U32),
                        pltpu.SemaphoreType.DMA(()), pltpu.SemaphoreType.DMA(())])
    return pl.pallas_call(
        functools.partial(_permute_kernel, fills_per_step=-(-fill_n.shape[0] // nt)), grid_spec=grid_spec,
        out_shape=jax.ShapeDtypeStruct((n_rows, PACKED_W), U32),
        compiler_params=_cparams(("arbitrary",)),
    )(fill_start, fill_n, pos, h1)


def _experts_kernel(blk_e_ref, nused_ref, x_ref, w1_ref, w3_ref, w2_ref, y_ref, w1b, w3b, w2b):
    j = pl.program_id(0)

    @pl.when(j < nused_ref[0])
    def _():
        first_of_expert = (j == 0) | (blk_e_ref[j] != blk_e_ref[jnp.maximum(j - 1, 0)])

        @pl.when(first_of_expert)
        def _():
            w1b[...] = w1_ref[...].astype(BF)
            w3b[...] = w3_ref[...].astype(BF)
            w2b[...] = w2_ref[...].astype(BF)

        x = jnp.concatenate([half.astype(BF) for half in _unpack_rows(x_ref[...])], axis=1)
        a = _dot(x, w1b[...])
        y_ref[...] = _pack_rows(_dot((a * jax.nn.sigmoid(a) * _dot(x, w3b[...])).astype(BF), w2b[...]))

    @pl.when(j >= nused_ref[0])
    def _():
        y_ref[...] = jnp.zeros(y_ref.shape, y_ref.dtype)


def _experts(blk_e, nused, xs, w1, w3, w2):
    n_blocks = blk_e.shape[0]
    rb = EXPERT_BLOCK
    sq = pl.Squeezed()
    wmap = lambda j, be, nu: (be[j], 0, 0)
    grid_spec = pltpu.PrefetchScalarGridSpec(
        num_scalar_prefetch=2, grid=(n_blocks,),
        in_specs=[pl.BlockSpec((rb, PACKED_W), lambda j, be, nu: (jnp.minimum(j, nu[0] - 1), 0)),
                  pl.BlockSpec((sq, D_MODEL, D_EXPERT), wmap), pl.BlockSpec((sq, D_MODEL, D_EXPERT), wmap),
                  pl.BlockSpec((sq, D_EXPERT, D_MODEL), wmap)],
        out_specs=pl.BlockSpec((rb, PACKED_W), lambda j, be, nu: (j, 0)),
        scratch_shapes=[pltpu.VMEM((D_MODEL, D_EXPERT), BF), pltpu.VMEM((D_MODEL, D_EXPERT), BF),
                        pltpu.VMEM((D_EXPERT, D_MODEL), BF)])
    return pl.pallas_call(
        _experts_kernel, grid_spec=grid_spec,
        out_shape=jax.ShapeDtypeStruct((n_blocks * rb, PACKED_W), U32),
        compiler_params=_cparams(("arbitrary",)),
    )(blk_e, nused, xs, w1, w3, w2)


def _combine_kernel(pos_ref, pos_next_ref, ys_hbm, w_ref, h_ref, s1_ref, s3_ref, s2_ref, g_ref, b_ref, o_ref, ybuf, sem):
    i = pl.program_id(0)
    nt = pl.num_programs(0)
    tm = h_ref.shape[0]

    def start_gather(p_ref, slot):
        def body(t, c):
            for k in range(TOP_K):
                pltpu.make_async_copy(ys_hbm.at[pl.ds(p_ref[k, t], 1)], ybuf.at[slot, k, pl.ds(t, 1)], sem.at[slot]).start()
            return c

        lax.fori_loop(0, tm, body, 0, unroll=ROW_UNROLL)

    @pl.when(i == 0)
    def _():
        start_gather(pos_ref, 0)

    slot = i % 2
    pltpu.make_async_copy(ybuf.at[slot], ybuf.at[slot], sem.at[slot]).wait()

    for t in range(tm):
        for k in range(TOP_K):
            pltpu.make_async_copy(ys_hbm.at[pl.ds(pos_next_ref[k, t], 1)], ybuf.at[1 - slot, k, pl.ds(t, 1)],
                                  sem.at[1 - slot]).start()

    h = h_ref[...]
    hb = h.astype(BF)
    a = _dot(hb, s1_ref[...])
    y = ALPHA * h + _dot((a * jax.nn.sigmoid(a) * _dot(hb, s3_ref[...])).astype(BF), s2_ref[...])
    w = w_ref[...]
    routed = None
    for k in range(TOP_K):
        terms = [w[:, k:k + 1] * half for half in _unpack_rows(ybuf[slot, k])]
        routed = terms if routed is None else [r + t for r, t in zip(routed, terms)]
    y = y + jnp.concatenate(routed, axis=1)
    o_ref[...] = _layer_norm(y, g_ref[...], b_ref[...])

    @pl.when(i == nt - 1)
    def _():
        pltpu.make_async_copy(ybuf.at[1 - slot], ybuf.at[1 - slot], sem.at[1 - slot]).wait()


def _combine(pos, ys, w, h1, s1, s3, s2, g, b, tm):
    n = h1.shape[0]
    nt = n // tm
    row = lambda i: (i, 0)
    const = lambda i: (0, 0)
    return pl.pallas_call(
        _combine_kernel, grid=(nt,),
        in_specs=[pl.BlockSpec((TOP_K, tm), lambda i: (0, i), memory_space=pltpu.SMEM),
                  pl.BlockSpec((TOP_K, tm), lambda i: (0, jnp.minimum(i + 1, nt - 1)), memory_space=pltpu.SMEM),
                  pl.BlockSpec(memory_space=pl.ANY),
                  pl.BlockSpec((tm, TOP_K), row), pl.BlockSpec((tm, D_MODEL), row),
                  pl.BlockSpec(s1.shape, const), pl.BlockSpec(s3.shape, const), pl.BlockSpec(s2.shape, const),
                  pl.BlockSpec((1, D_MODEL), const), pl.BlockSpec((1, D_MODEL), const)],
        out_specs=pl.BlockSpec((tm, D_MODEL), row),
        out_shape=jax.ShapeDtypeStruct((n, D_MODEL), F32),
        scratch_shapes=[pltpu.VMEM((2, TOP_K, tm, PACKED_W), U32), pltpu.SemaphoreType.DMA((2,))],
        compiler_params=_cparams(("arbitrary",)),
    )(pos, pos, ys, w, h1, s1, s3, s2, g, b)


def _overlap_matrix(nc, n_slc):
    cs = np.arange(nc) * CMP_STRIDE
    ce = cs + CMP_LEN - 1
    js = np.arange(n_slc) * SLC_LEN
    je = js + SLC_LEN - 1
    ov = ((cs[:, None] <= je[None, :]) & (ce[:, None] >= js[None, :])).astype(np.float32)
    ov[nc - 1] = 0.0
    return ov


def _block_layout(counts, n):
    rb = EXPERT_BLOCK
    counts = counts.astype(I32)
    padded = (counts + rb - 1) // rb * rb
    pad_end = jnp.cumsum(padded)
    pad_start = pad_end - padded
    n_blocks = -(-TOP_K * n // rb) + N_EXPERTS
    blk_start = jnp.arange(n_blocks, dtype=I32) * rb
    blk_e = jnp.minimum(jnp.sum(pad_end[None, :] <= blk_start[:, None], axis=1), N_EXPERTS - 1).astype(I32)
    nused = (pad_end[-1] // rb).astype(I32).reshape(1)
    half = rb // 2
    tail_start = pad_end[-1] + half * jnp.arange(2 * N_EXPERTS, dtype=I32)
    tail_n = jnp.where(tail_start < n_blocks * rb, half, 0).astype(I32)
    fill_start = jnp.concatenate([pad_start + counts, jnp.minimum(tail_start, n_blocks * rb - half)])
    fill_n = jnp.concatenate([padded - counts, tail_n])
    return pad_start, fill_start, fill_n, blk_e, nused


def kernel(x, mem, ln0_g, ln0_b, rel_bias, w_in, cmp_pos_k, cmp_pos_v, cmp_k_w1, cmp_k_w2, cmp_v_w1, cmp_v_w2, mla_q_norm, mla_w_uq, mla_kv_norm, mla_w_ukv, mem_w_kv, w_branch, w_out, ln1_g, ln1_b, router_w, router_b, exp_w1, exp_w3, exp_w2, sh_w1, sh_w3, sh_w2, ln2_g, ln2_b):
    batch, seq, d = x.shape
    n = batch * seq
    l = 0
    row2 = lambda v: v.reshape(1, -1)
    tm = min(256, seq)

    pts = np.cumsum((0,) + IN_SPLITS)
    wcol = lambda k: w_in[l][:, pts[k]:pts[k + 1]]
    pad_cols = lambda w, lo, tot: jnp.pad(w, ((0, 0), (lo, tot - lo - w.shape[1])))
    ws = [wcol(0) * (NSA_DH ** -0.5 * LOG2E), wcol(1), pad_cols(wcol(2), 0, LANE), wcol(3), wcol(4),
          pad_cols(wcol(5), ROPE_LANE0, LANE), wcol(6), wcol(7)]
    ws = [w.astype(BF) for w in ws]
    scales = [1.0] * 6 + [MEM_DH ** -0.5, 1.0]
    gw = NSA_GROUPS * NSA_DH
    ws_t = [wcol(0) * (NSA_DH ** -0.5 * LOG2E), wcol(1)[:, 3 * gw:4 * gw], wcol(1)[:, 5 * gw:6 * gw],
            pad_cols(wcol(2), 0, LANE)]
    ws_t = [w.T.astype(BF) for w in ws_t]
    hd = MLA_NOPE + MLA_ROPE
    wuq = jnp.pad(mla_w_uq[l].reshape(MLA_Q_LORA, MLA_HEADS, hd), ((0, 0), (0, 0), (0, HEAD_PAD - hd)))
    wuq = wuq.reshape(MLA_Q_LORA, MLA_HEADS * HEAD_PAD).astype(BF)
    wukv = mla_w_ukv[l].reshape(MLA_KV_LORA, MLA_HEADS, MLA_NOPE + MLA_DV)
    wuk = jnp.pad(wukv[:, :, :MLA_NOPE], ((0, 0), (0, 0), (0, HEAD_PAD - MLA_NOPE)))
    wuk = wuk.reshape(MLA_KV_LORA, MLA_HEADS * HEAD_PAD).astype(BF)
    wuv = wukv[:, :, MLA_NOPE:].reshape(MLA_KV_LORA, MLA_HEADS * MLA_DV).astype(BF)

    h, qn, kvn, gn, cq, ckv, kr, qm, gm, qn_t, vs_t, vw_t, gn_t = _ln_inproj(
        x.reshape(n, d), row2(ln0_g), row2(ln0_b), ws, scales, ws_t, tm)

    q_mla, k_mla, v_mla = _mla_prep(cq, ckv, kr, row2(mla_q_norm[l]), row2(mla_kv_norm[l]), wuq, wuk, wuv, seq, tm)
    o_mla = _mla_attn(q_mla, k_mla, v_mla, batch, seq, min(MLA_TQ, seq))

    nc = seq // CMP_STRIDE
    n_slc = seq // SLC_LEN
    cc = kvn[:, :2 * gw].reshape(batch, nc, CMP_STRIDE, 2 * NSA_GROUPS, NSA_DH)
    cc = cc.transpose(0, 3, 1, 2, 4).reshape(batch, 2 * NSA_GROUPS, nc, CMP_STRIDE * NSA_DH)
    pos = jnp.stack([cmp_pos_k[l], cmp_pos_v[l]]).reshape(2, 1, CMP_LEN * NSA_DH)
    w1c = jnp.stack([cmp_k_w1[l], cmp_v_w1[l]]).astype(BF)
    w2c = jnp.stack([cmp_k_w2[l], cmp_v_w2[l]]).astype(BF)
    kvc = _nsa_compress(cc, pos, w1c, w2c)
    overlap = jnp.asarray(_overlap_matrix(nc, n_slc), BF)
    o_cmp, sel_t = _nsa_cmp(rel_bias, qn, kvc, overlap, batch, seq)
    o_nsa = _nsa_attn(rel_bias, qn_t, kvn, vs_t, vw_t, sel_t, gn, gn_t, o_cmp, batch, seq)

    o_mem = _mem_attn(qm, mem.reshape(-1, d), mem_w_kv[l].astype(BF), batch, seq, min(512, seq))

    h1 = _merge(o_nsa, o_mla, o_mem, gm, h, w_branch[l].astype(BF), w_out[l].astype(BF),
                row2(ln1_g[l]), row2(ln1_b[l]), tm)

    rw_t = router_w[l].T
    rw_hi = rw_t.astype(BF)
    rw_lo = (rw_t - rw_hi.astype(F32)).astype(BF)
    eidx_t, w_t, rank_t, counts = _router(h1, rw_hi, rw_lo, router_b[l].reshape(N_EXPERTS, 1), tm)
    pad_start, fill_start, fill_n, blk_e, nused = _block_layout(counts[:, 0], n)
    pos = _positions(eidx_t, rank_t, pad_start.astype(F32).reshape(N_EXPERTS, 1), tm)
    xs = _permute(fill_start, fill_n, pos, h1, blk_e.shape[0] * EXPERT_BLOCK, tm)
    ys = _experts(blk_e, nused, xs, exp_w1[l], exp_w3[l], exp_w2[l])
    out = _combine(pos, ys, w_t.T, h1, sh_w1[l].astype(BF), sh_w3[l].astype(BF), sh_w2[l].astype(BF),
                   row2(ln2_g[l]), row2(ln2_b[l]), min(128, seq))
    return out.reshape(batch, seq, d)
```

```python
import functools
import math

import numpy as np
import jax
import jax.numpy as jnp
from jax import lax
from jax.experimental import pallas as pl
from jax.experimental.pallas import tpu as pltpu

BF = jnp.bfloat16
F32 = jnp.float32
I32 = jnp.int32

D_MODEL = 1024
DEPTH = 1
NSA_HEADS = 8
NSA_GROUPS = 2
NSA_HPG = NSA_HEADS // NSA_GROUPS
NSA_DH = 64
CMP_LEN = 32
CMP_STRIDE = 16
CMP_HID = 256
SLC_LEN = 64
SLC_TOPN = 16
WIN = 512
MLA_HEADS = 8
MLA_NOPE = 64
MLA_ROPE = 32
MLA_DV = 64
MLA_Q_LORA = 768
MLA_KV_LORA = 256
ROPE_THETA = 10000.0
MEM_HEADS = 4
MEM_DH = 128
N_BRANCH = 3
BRANCH_W = NSA_HEADS * NSA_DH
REL_BUCKETS = 32
REL_MAX_DIST = 128
N_EXPERTS = 256
TOP_K = 8
N_EXPERT_GROUPS = 8
TOPK_GROUPS = 4
D_EXPERT = 256
ROUTE_SCALE = 2.5
EXPERT_BLOCK = 512
LN_EPS = 1e-5
RMS_EPS = 1e-6
NEG = -1e30
BIG = 1e30
ALPHA = (2 * DEPTH) ** 0.25
IN_SPLITS = (NSA_HEADS * NSA_DH, 6 * NSA_GROUPS * NSA_DH, 3 * NSA_HEADS, MLA_Q_LORA, MLA_KV_LORA,
             MLA_ROPE, MEM_HEADS * MEM_DH, N_BRANCH * D_MODEL)

LANE = 128
SUBLANES = 8
HEAD_PAD = 128
ROPE_LANE0 = MLA_NOPE
ROPE_HALF = MLA_ROPE // 2
VMEM_LIMIT = 56 * 1024 * 1024
NSA_TQ = 128
MLA_TQ = 512
FAR_CHUNK = 512


def _cparams(sem):
    return pltpu.CompilerParams(dimension_semantics=sem, vmem_limit_bytes=VMEM_LIMIT)


def _bucket_starts():
    max_exact = REL_BUCKETS // 2
    d = np.arange(0, 4 * REL_MAX_DIST)
    nf = np.maximum(d, 1).astype(np.float32)
    large = max_exact + (np.log(nf / np.float32(max_exact)) / np.float32(math.log(REL_MAX_DIST / max_exact))
                         * np.float32(REL_BUCKETS - max_exact)).astype(np.int32)
    large = np.minimum(large, REL_BUCKETS - 1)
    bucket = np.where(d < max_exact, d, large)
    return [int(np.argmax(bucket >= b)) for b in range(REL_BUCKETS)]


BUCKET_START = _bucket_starts()
FAR_DIST = BUCKET_START[REL_BUCKETS - 1]


def _rel_bias(dist, tab_ref, head):
    val = jnp.full(dist.shape, tab_ref[REL_BUCKETS - 1, head], F32)
    for b in range(REL_BUCKETS - 2, -1, -1):
        val = jnp.where(dist < BUCKET_START[b + 1], tab_ref[b, head], val)
    return val


def _layer_norm(x, g, b):
    mu = jnp.mean(x, -1, keepdims=True)
    xc = x - mu
    var = jnp.mean(xc * xc, -1, keepdims=True)
    return xc * lax.rsqrt(var + LN_EPS) * g + b


def _dot(a, b):
    return jnp.dot(a, b, preferred_element_type=F32)


def _dot_nt(a, b):
    return lax.dot_general(a, b, (((1,), (1,)), ((), ())), preferred_element_type=F32)


def _ln_inproj_kernel(*refs, scales, n_rowmajor):
    x_ref, g_ref, b_ref = refs[:3]
    n_w = (len(refs) - 4) // 2
    w_refs = refs[3:3 + n_w]
    h_ref = refs[3 + n_w]
    o_refs = refs[4 + n_w:]
    h = _layer_norm(x_ref[...], g_ref[...], b_ref[...])
    h_ref[...] = h
    hb = h.astype(BF)
    for j, (w, o) in enumerate(zip(w_refs, o_refs)):
        if j < n_rowmajor:
            y = _dot(hb, w[...])
            o[...] = (y if scales[j] == 1.0 else y * scales[j]).astype(o.dtype)
        else:
            o[...] = _dot_nt(w[...], hb).astype(o.dtype)


def _ln_inproj(x2, g, b, ws, scales, ws_t, tm):
    n = x2.shape[0]
    row = lambda i: (i, 0)
    col = lambda i: (0, i)
    const = lambda i: (0, 0)
    in_specs = [pl.BlockSpec((tm, D_MODEL), row), pl.BlockSpec((1, D_MODEL), const), pl.BlockSpec((1, D_MODEL), const)]
    in_specs += [pl.BlockSpec(w.shape, const) for w in ws + ws_t]
    out_shape = [jax.ShapeDtypeStruct((n, D_MODEL), F32)]
    out_shape += [jax.ShapeDtypeStruct((n, w.shape[1]), BF) for w in ws]
    out_shape += [jax.ShapeDtypeStruct((w.shape[0], n), BF) for w in ws_t]
    out_specs = [pl.BlockSpec((tm, D_MODEL), row)] + [pl.BlockSpec((tm, w.shape[1]), row) for w in ws]
    out_specs += [pl.BlockSpec((w.shape[0], tm), col) for w in ws_t]
    return pl.pallas_call(
        functools.partial(_ln_inproj_kernel, scales=tuple(scales), n_rowmajor=len(ws)),
        grid=(n // tm,), in_specs=in_specs, out_specs=out_specs, out_shape=out_shape,
        compiler_params=_cparams(("parallel",)),
    )(x2, g, b, *ws, *ws_t)


def _rope_lanes(x, c, s1, s2):
    return x * c + pltpu.roll(x, LANE - ROPE_HALF, 1) * s1 + pltpu.roll(x, ROPE_HALF, 1) * s2


def _mla_prep_kernel(cq_ref, ckv_ref, kr_ref, qn_ref, kvn_ref, wuq, wuk, wuv,
                     cq_t, s1q_t, s2q_t, ck_t, s1k_t, s2k_t, q_out, k_out, v_out):
    cq = cq_ref[...].astype(F32)
    rq = cq * lax.rsqrt(jnp.mean(cq * cq, -1, keepdims=True) + RMS_EPS) * qn_ref[...]
    q = _dot(rq.astype(BF), wuq[...])
    ckv = ckv_ref[...].astype(F32)
    rkv = (ckv * lax.rsqrt(jnp.mean(ckv * ckv, -1, keepdims=True) + RMS_EPS) * kvn_ref[...]).astype(BF)
    kn = _dot(rkv, wuk[...])
    v = _dot(rkv, wuv[...])
    pair_w = 2 * MLA_DV
    ones = jnp.ones((v.shape[0], pair_w), v_out.dtype)
    for p in range(MLA_HEADS // 2):
        v_out[:, 2 * p * pair_w:(2 * p + 1) * pair_w] = v[:, p * pair_w:(p + 1) * pair_w].astype(v_out.dtype)
        v_out[:, (2 * p + 1) * pair_w:(2 * p + 2) * pair_w] = ones
    kr = _rope_lanes(kr_ref[...].astype(F32), ck_t[...], s1k_t[...], s2k_t[...])
    cq_c, s1q, s2q = cq_t[...], s1q_t[...], s2q_t[...]
    for h in range(MLA_HEADS):
        sl = slice(h * HEAD_PAD, (h + 1) * HEAD_PAD)
        q_out[:, sl] = _rope_lanes(q[:, sl], cq_c, s1q, s2q).astype(q_out.dtype)
        k_out[:, sl] = (kn[:, sl] + kr).astype(k_out.dtype)


def _rope_tables(seq, scale):
    freq = ROPE_THETA ** (-jnp.arange(ROPE_HALF, dtype=F32) / ROPE_HALF)
    ang = jnp.arange(seq, dtype=F32)[:, None] * freq[None, :]
    cos, sin = jnp.cos(ang) * scale, jnp.sin(ang) * scale
    z = lambda w: jnp.zeros((seq, w), F32)
    tail = HEAD_PAD - ROPE_LANE0 - MLA_ROPE
    c = jnp.concatenate([jnp.full((seq, ROPE_LANE0), scale, F32), cos, cos, z(tail)], 1)
    s1 = jnp.concatenate([z(ROPE_LANE0), -sin, z(ROPE_HALF + tail)], 1)
    s2 = jnp.concatenate([z(ROPE_LANE0 + ROPE_HALF), sin, z(tail)], 1)
    return c, s1, s2


def _mla_prep(cq, ckv, kr, q_norm, kv_norm, wuq, wuk, wuv, seq, tm):
    n = cq.shape[0]
    nt = seq // tm
    row = lambda i: (i, 0)
    const = lambda i: (0, 0)
    pos = lambda i: (i % nt, 0)
    tabs = _rope_tables(seq, (MLA_NOPE + MLA_ROPE) ** -0.5 * math.log2(math.e)) + _rope_tables(seq, 1.0)
    in_specs = [pl.BlockSpec((tm, MLA_Q_LORA), row), pl.BlockSpec((tm, MLA_KV_LORA), row), pl.BlockSpec((tm, LANE), row),
                pl.BlockSpec((1, MLA_Q_LORA), const), pl.BlockSpec((1, MLA_KV_LORA), const),
                pl.BlockSpec(wuq.shape, const), pl.BlockSpec(wuk.shape, const), pl.BlockSpec(wuv.shape, const)]
    in_specs += [pl.BlockSpec((tm, LANE), pos)] * 6
    hq = MLA_HEADS * HEAD_PAD
    hv = 2 * MLA_HEADS * MLA_DV
    return pl.pallas_call(
        _mla_prep_kernel, grid=(n // tm,), in_specs=in_specs,
        out_specs=[pl.BlockSpec((tm, hq), row), pl.BlockSpec((tm, hq), row), pl.BlockSpec((tm, hv), row)],
        out_shape=[jax.ShapeDtypeStruct((n, hq), BF), jax.ShapeDtypeStruct((n, hq), BF), jax.ShapeDtypeStruct((n, hv), BF)],
        compiler_params=_cparams(("parallel",)),
    )(cq, ckv, kr, q_norm, kv_norm, wuq, wuk, wuv, *tabs)


def _mla_attn_kernel(q_ref, k_ref, v_ref, o_ref, sa_scr, sb_scr, m_scr, acc_scr):
    i = pl.program_id(2)
    t = q_ref.shape[0]
    reps = t // LANE
    pair_w = 2 * MLA_DV
    m_scr[...] = jnp.full(m_scr.shape, NEG, F32)
    acc_scr[...] = jnp.zeros(acc_scr.shape, F32)

    def logits(tile, s_scr):
        kstart = pl.multiple_of(tile * t, t)
        for hh in range(2):
            sl = slice(hh * HEAD_PAD, (hh + 1) * HEAD_PAD)
            s_scr[hh] = _dot_nt(q_ref[:, sl], k_ref[pl.ds(kstart, t), sl])

    def consume(tile, s_scr, diagonal):
        v = v_ref[pl.ds(pl.multiple_of(tile * t, t), t), :]
        for hh in range(2):
            s = s_scr[hh]
            if diagonal:
                row = lax.broadcasted_iota(I32, (t, t), 0)
                col = lax.broadcasted_iota(I32, (t, t), 1)
                s = jnp.where(col <= row, s, NEG)
            m_prev = m_scr[hh]
            m_new = jnp.maximum(m_prev, jnp.max(s, 1, keepdims=True))
            a = jnp.exp2(m_prev - m_new)
            e = jnp.exp2(s - jnp.tile(m_new, (1, reps)))
            acc_scr[hh] = jnp.tile(a, (1, 2)) * acc_scr[hh] + _dot(e.astype(BF), v)
            m_scr[hh] = m_new

    logits(0, sa_scr)

    def body(p, carry):
        logits(2 * p + 1, sb_scr)
        consume(2 * p, sa_scr, False)
        logits(2 * p + 2, sa_scr)
        consume(2 * p + 1, sb_scr, False)
        return carry

    lax.fori_loop(0, i // 2, body, 0)

    @pl.when(i % 2 == 0)
    def _():
        consume(i, sa_scr, True)

    @pl.when(i % 2 == 1)
    def _():
        logits(i, sb_scr)
        consume(i - 1, sa_scr, False)
        consume(i, sb_scr, True)

    lane = lax.broadcasted_iota(I32, (t, pair_w), 1)
    o = jnp.where(lane < MLA_DV, acc_scr[0, :, :pair_w] / acc_scr[0, :, pair_w:],
                  acc_scr[1, :, :pair_w] / acc_scr[1, :, pair_w:])
    o_ref[...] = o.astype(o_ref.dtype)


def _mla_attn(q, k, v, batch, seq, t):
    n = q.shape[0]
    nt = seq // t
    qmap = lambda b, hp, i: (b * nt + i, hp)
    kmap = lambda b, hp, i: (b, hp)
    return pl.pallas_call(
        _mla_attn_kernel, grid=(batch, MLA_HEADS // 2, nt),
        in_specs=[pl.BlockSpec((t, 2 * HEAD_PAD), qmap), pl.BlockSpec((seq, 2 * HEAD_PAD), kmap),
                  pl.BlockSpec((seq, 4 * MLA_DV), kmap)],
        out_specs=pl.BlockSpec((t, 2 * MLA_DV), qmap),
        out_shape=jax.ShapeDtypeStruct((n, MLA_HEADS * MLA_DV), BF),
        scratch_shapes=[pltpu.VMEM((2, t, t), F32), pltpu.VMEM((2, t, t), F32),
                        pltpu.VMEM((2, t, LANE), F32), pltpu.VMEM((2, t, 4 * MLA_DV), F32)],
        compiler_params=_cparams(("parallel", "parallel", "arbitrary")),
    )(q, k, v)


def _nsa_compress_kernel(c_ref, pos_ref, w1_ref, w2_ref, o_ref):
    nc = c_ref.shape[0]
    half = CMP_STRIDE * NSA_DH
    c = c_ref[...]
    top = _dot(c, w1_ref[:half, :])
    bot = _dot(c, w1_ref[half:, :])
    posb = _dot(jnp.broadcast_to(pos_ref[...], (8, 2 * half)).astype(BF), w1_ref[...])[:1]
    hid = top + pltpu.roll(bot, nc - 1, 0) + posb
    o_ref[...] = _dot(jax.nn.gelu(hid).astype(BF), w2_ref[...]).astype(o_ref.dtype)


def _nsa_compress(cc, pos, w1, w2):
    b, _, nc, half = cc.shape
    sq = pl.Squeezed()
    return pl.pallas_call(
        _nsa_compress_kernel, grid=(b, 2 * NSA_GROUPS),
        in_specs=[pl.BlockSpec((sq, sq, nc, half), lambda i, c: (i, c, 0, 0)),
                  pl.BlockSpec((sq, 1, 2 * half), lambda i, c: (c // NSA_GROUPS, 0, 0)),
                  pl.BlockSpec((sq, 2 * half, CMP_HID), lambda i, c: (c // NSA_GROUPS, 0, 0)),
                  pl.BlockSpec((sq, CMP_HID, NSA_DH), lambda i, c: (c // NSA_GROUPS, 0, 0))],
        out_specs=pl.BlockSpec((sq, sq, nc, NSA_DH), lambda i, c: (i, c, 0, 0)),
        out_shape=jax.ShapeDtypeStruct((b, 2 * NSA_GROUPS, nc, NSA_DH), BF),
        compiler_params=_cparams(("parallel", "parallel")),
    )(cc, pos, w1, w2)


CMP_TQ = 512
CMP_BIAS_COLS = LANE // 2
CMP_BIAS_BACK = -(-(FAR_DIST + CMP_LEN - 1) // CMP_STRIDE)
assert (CMP_TQ - CMP_LEN) // CMP_STRIDE + CMP_BIAS_BACK < CMP_BIAS_COLS


def _nsa_cmp_kernel(tab_ref, q_ref, kc_ref, vc_ref, ov_ref, oc_ref, sel_ref, e_scr, *, n_top):
    b, g, i = pl.program_id(0), pl.program_id(1), pl.program_id(2)
    tq = q_ref.shape[0]
    nc = kc_ref.shape[0]
    n_slc = ov_ref.shape[1]
    qs = i * tq

    @pl.when((b == 0) & (g == 0) & (i == 0))
    def _():
        q_io = lax.broadcasted_iota(I32, (tq, LANE), 0)
        lane = lax.broadcasted_iota(I32, (tq, LANE), 1)
        jj = lane % CMP_BIAS_COLS
        dist = q_io - CMP_STRIDE * (jj - CMP_BIAS_BACK) - (CMP_LEN - 1)
        live = (dist >= 0) & (lane < 2 * CMP_BIAS_COLS)
        for h in range(NSA_HEADS):
            e = jnp.where(live, (_rel_bias(dist, tab_ref, h) - tab_ref[REL_BUCKETS - 1, h]) * LOG2E, 0.0)
            hi = e.astype(BF)
            lo = (e - hi.astype(F32)).astype(BF)
            e_scr[h] = jnp.where(lane < CMP_BIAS_COLS, hi, lo)

    n0 = qs // CMP_STRIDE
    jrow = lax.broadcasted_iota(I32, (LANE, nc), 0)
    ncol = lax.broadcasted_iota(I32, (LANE, nc), 1)
    ft = jnp.where((ncol == n0 + (jrow % CMP_BIAS_COLS) - CMP_BIAS_BACK) & (jrow < 2 * CMP_BIAS_COLS), 1.0, 0.0).astype(BF)

    t = qs + lax.broadcasted_iota(I32, (tq, nc), 0)
    n_io = lax.broadcasted_iota(I32, (tq, nc), 1)
    mask_add = jnp.where((t >= n_io * CMP_STRIDE + (CMP_LEN - 1)) & (n_io < nc - 1), 0.0, NEG)
    has_key = qs + lax.broadcasted_iota(I32, (tq, 1), 0) >= CMP_LEN - 1
    kc = kc_ref[...]
    vc = vc_ref[...]
    psum = jnp.zeros((tq, nc), F32)
    for h in range(NSA_HPG):
        head = g * NSA_HPG + h
        qh = q_ref[:, h * NSA_DH:(h + 1) * NSA_DH]
        s = _dot_nt(qh, kc) + _dot(e_scr[head], ft) + mask_add
        e = jnp.exp2(s - jnp.max(s, 1, keepdims=True))
        p = e * jnp.where(has_key, 1.0 / jnp.sum(e, 1, keepdims=True), 0.0)
        oc_ref[:, h * NSA_DH:(h + 1) * NSA_DH] = _dot(p.astype(BF), vc).astype(oc_ref.dtype)
        psum = psum + p

    ov = ov_ref[...]
    p0 = psum.astype(BF)
    r1 = psum - p0.astype(F32)
    p1 = r1.astype(BF)
    p2 = (r1 - p1.astype(F32)).astype(BF)
    imp = (_dot(p0, ov) + _dot(p1, ov) + _dot(p2, ov)).T

    tj = qs + lax.broadcasted_iota(I32, (n_slc, tq), 1)
    j = lax.broadcasted_iota(I32, (n_slc, tq), 0)
    cur = tj // SLC_LEN
    forced = (j == 0) | (j == cur) | (j == cur - 1)
    work = jnp.where(j * SLC_LEN > tj, NEG, jnp.where(forced, BIG, imp))
    sel = jnp.full((n_slc, tq), NEG, F32)
    for _ in range(n_top):
        mx, first = _first_max(work, j, n_slc)
        hit = j == first
        sel = jnp.where(hit & (mx > 0.5 * NEG), 0.0, sel)
        work = jnp.where(hit, -jnp.inf, work)
    sel_ref[...] = sel


def _nsa_cmp(tab, qn, kvc, overlap, batch, seq):
    n = qn.shape[0]
    tq = min(CMP_TQ, seq)
    nq = seq // tq
    nc = kvc.shape[2]
    n_slc = overlap.shape[1]
    sq = pl.Squeezed()
    gw = NSA_HPG * NSA_DH
    return pl.pallas_call(
        functools.partial(_nsa_cmp_kernel, n_top=min(SLC_TOPN, n_slc)), grid=(batch, NSA_GROUPS, nq),
        in_specs=[pl.BlockSpec(memory_space=pltpu.SMEM),
                  pl.BlockSpec((tq, gw), lambda b, g, i: (b * nq + i, g)),
                  pl.BlockSpec((sq, sq, nc, NSA_DH), lambda b, g, i: (b, g, 0, 0)),
                  pl.BlockSpec((sq, sq, nc, NSA_DH), lambda b, g, i: (b, NSA_GROUPS + g, 0, 0)),
                  pl.BlockSpec((nc, n_slc), lambda b, g, i: (0, 0))],
        out_specs=[pl.BlockSpec((tq, gw), lambda b, g, i: (b * nq + i, g)),
                   pl.BlockSpec((sq, n_slc, tq), lambda b, g, i: (g, 0, b * nq + i))],
        out_shape=[jax.ShapeDtypeStruct((n, NSA_HEADS * NSA_DH), BF),
                   jax.ShapeDtypeStruct((NSA_GROUPS, n_slc, n), F32)],
        scratch_shapes=[pltpu.VMEM((NSA_HEADS, tq, LANE), BF)],
        compiler_params=_cparams(("arbitrary", "arbitrary", "arbitrary")),
    )(tab, qn, kvc, kvc, overlap)


LOG2E = math.log2(math.e)
DEN_ROWS = 16
MASK_ROWS = 16


def _nsa_attn_kernel(tab_ref, qt_ref, ks_ref, kw_ref, vst_ref, vwt_ref, sel_ref, gate_ref, gate_t_ref, oc_ref, o_ref,
                     d_scr, q_scr, sa_scr, sb_scr, sc_scr, m_scr, acc_scr):
    b, i = pl.program_id(0), pl.program_id(1)
    tq = NSA_TQ
    lanes = NSA_HPG * tq
    qs = i * tq
    near_w = 2 * tq
    far_w = FAR_CHUNK
    win_far_w = WIN - tq

    @pl.when((b == 0) & (i == 0))
    def _():
        kk = lax.broadcasted_iota(I32, (near_w, tq), 0)
        q_io = lax.broadcasted_iota(I32, (near_w, tq), 1)
        dist = jnp.maximum(q_io + tq - kk, 0)
        for g in range(NSA_GROUPS):
            d_scr[g] = jnp.concatenate(
                [(_rel_bias(dist, tab_ref, g * NSA_HPG + h) - tab_ref[REL_BUCKETS - 1, g * NSA_HPG + h]) * LOG2E
                 for h in range(NSA_HPG)], axis=1)

    gw = NSA_GROUPS * NSA_DH
    qt = qt_ref[...]
    for g in range(NSA_GROUPS):
        qg = jnp.concatenate([qt[(g * NSA_HPG + h) * NSA_DH:(g * NSA_HPG + h + 1) * NSA_DH, :]
                              for h in range(NSA_HPG)], axis=1)
        parts = [jnp.zeros((NSA_DH, lanes), BF)] * NSA_GROUPS
        parts[g] = qg
        for role in range(2):
            q_scr[role, g, :gw] = jnp.concatenate(parts, axis=0)
            q_scr[role, g, gw:] = jnp.zeros((gw, lanes), BF)

    def update(slot, s, vt):
        m_prev = m_scr[slot]
        m_new = jnp.maximum(m_prev, jnp.max(s, 0, keepdims=True))
        a = jnp.exp2(m_prev - m_new)
        e = jnp.exp2(s - m_new)
        vt_den = jnp.concatenate([vt, jnp.ones((DEN_ROWS, vt.shape[1]), BF)], axis=0)
        acc_scr[slot] = a * acc_scr[slot] + _dot(vt_den, e.astype(BF))
        m_scr[slot] = m_new

    def all_heads(x):
        return jnp.tile(x, (1, NSA_HPG))

    def sel_rows(g, first_block, n_blk):
        return jnp.concatenate(
            [jnp.broadcast_to(sel_ref[g, pl.ds(jnp.maximum(first_block + r, 0), 1), :], (SLC_LEN, tq))
             for r in range(n_blk)], axis=0)

    m_scr[...] = jnp.full(m_scr.shape, NEG, F32)
    acc_scr[...] = jnp.zeros(acc_scr.shape, F32)

    kk = lax.broadcasted_iota(I32, (near_w, tq), 0)
    q_io = lax.broadcasted_iota(I32, (near_w, tq), 1)
    kpos_near = qs - tq + kk
    causal_add = jnp.where((kpos_near >= 0) & (kpos_near <= qs + q_io), 0.0, NEG)
    start_a = pl.multiple_of(qs, tq)
    start_b = pl.multiple_of(jnp.maximum(qs - tq, 0), tq)

    def near_logits(br, s_scr):
        k_ref = ks_ref if br == 0 else kw_ref
        k = jnp.concatenate([k_ref[pl.ds(start_b, tq), :], k_ref[pl.ds(start_a, tq), :]], axis=0)
        for g in range(NSA_GROUPS):
            add = causal_add + sel_rows(g, 2 * i - 2, near_w // SLC_LEN) if br == 0 else causal_add
            s_scr[g, :near_w] = _dot(k, q_scr[0, g, :gw]) + d_scr[g] + all_heads(add)

    def near_consume(br, s_scr):
        vt_ref = vst_ref if br == 0 else vwt_ref
        for g in range(NSA_GROUPS):
            gs = slice(g * NSA_DH, (g + 1) * NSA_DH)
            vt = jnp.concatenate([vt_ref[gs, pl.ds(start_b, tq)], vt_ref[gs, pl.ds(start_a, tq)]], axis=1)
            update(2 * g + br, s_scr[g, :near_w], vt)

    ws = pl.multiple_of(jnp.maximum(qs - WIN, 0), tq)

    def win_logits(s_scr):
        kpos_w = ws + lax.broadcasted_iota(I32, (win_far_w, tq), 0)
        t_w = qs + lax.broadcasted_iota(I32, (win_far_w, tq), 1)
        add_w = all_heads(jnp.where((kpos_w < qs - tq) & (kpos_w > t_w - WIN), 0.0, NEG))
        k_w = kw_ref[pl.ds(ws, win_far_w), :]
        for g in range(NSA_GROUPS):
            s_scr[g, :win_far_w] = _dot(k_w, q_scr[0, g, :gw]) + add_w

    def win_consume(s_scr):
        for g in range(NSA_GROUPS):
            update(2 * g + 1, s_scr[g, :win_far_w], vwt_ref[g * NSA_DH:(g + 1) * NSA_DH, pl.ds(ws, win_far_w)])

    n_far = (jnp.maximum(i - 1, 0) + (far_w // tq - 1)) // (far_w // tq)
    last_chunk = ks_ref.shape[0] // far_w - 1
    n_blk = far_w // SLC_LEN
    assert n_blk + 1 <= MASK_ROWS
    krow = lax.broadcasted_iota(I32, (far_w, gw), 0)
    kcol = lax.broadcasted_iota(I32, (far_w, gw), 1)
    in_block = jnp.where(kcol == krow // SLC_LEN, 1.0, 0.0)

    def far_logits(c, s_scr, role):
        c = jnp.minimum(c, last_chunk)
        base = pl.multiple_of(c * far_w, far_w)
        past = jnp.where(base + krow >= qs - tq, 1.0, 0.0)
        k_aug = jnp.concatenate([ks_ref[pl.ds(base, far_w), :],
                                 jnp.where(kcol == n_blk, past, in_block).astype(BF)], axis=1)
        for g in range(NSA_GROUPS):
            sel_blk = sel_ref[g, pl.ds(pl.multiple_of(c * n_blk, n_blk), n_blk), :]
            rows = jnp.concatenate([all_heads(sel_blk), jnp.full((1, lanes), NEG, F32),
                                    jnp.zeros((MASK_ROWS - n_blk - 1, lanes), F32)], axis=0)
            q_scr[role, g, gw:gw + MASK_ROWS] = rows.astype(BF)
            s_scr[g] = _dot(k_aug, q_scr[role, g])

    def far_consume(c, s_scr):
        base = pl.multiple_of(c * far_w, far_w)
        for g in range(NSA_GROUPS):
            update(2 * g, s_scr[g], vst_ref[g * NSA_DH:(g + 1) * NSA_DH, pl.ds(base, far_w)])

    near_logits(0, sc_scr)
    near_logits(1, sb_scr)
    near_consume(0, sc_scr)
    win_logits(sc_scr)
    near_consume(1, sb_scr)
    far_logits(0, sa_scr, 0)
    win_consume(sc_scr)

    def far_body(p, carry):
        far_logits(2 * p + 1, sb_scr, 1)
        far_consume(2 * p, sa_scr)
        far_logits(2 * p + 2, sa_scr, 0)
        far_consume(2 * p + 1, sb_scr)
        return carry

    lax.fori_loop(0, n_far // 2, far_body, 0)

    @pl.when(n_far % 2 == 1)
    def _():
        far_consume(n_far - 1, sa_scr)

    gates = jax.nn.sigmoid(gate_ref[...].astype(F32))
    gates_t = jax.nn.sigmoid(gate_t_ref[...].astype(F32))
    for g in range(NSA_GROUPS):
        o_s = acc_scr[2 * g, :NSA_DH] / acc_scr[2 * g, NSA_DH:NSA_DH + 1]
        o_w = acc_scr[2 * g + 1, :NSA_DH] / acc_scr[2 * g + 1, NSA_DH:NSA_DH + 1]
        for h in range(NSA_HPG):
            head = g * NSA_HPG + h
            hl = slice(head * NSA_DH, (head + 1) * NSA_DH)
            cl = slice(h * tq, (h + 1) * tq)
            sw_t = (gates_t[3 * head + 1:3 * head + 2, :] * o_s[:, cl]
                    + gates_t[3 * head + 2:3 * head + 3, :] * o_w[:, cl])
            o = gates[:, 3 * head:3 * head + 1] * oc_ref[:, hl].astype(F32) + sw_t.T
            o_ref[:, hl] = o.astype(o_ref.dtype)


def _nsa_attn(tab, qn_t, kvn, vs_t, vw_t, sel_t, gn, gn_t, oc, batch, seq):
    n = kvn.shape[0]
    tq = NSA_TQ
    nq = seq // tq
    n_slc = sel_t.shape[1]
    hw = NSA_HEADS * NSA_DH
    gw = NSA_GROUPS * NSA_DH
    lanes = NSA_HPG * tq
    row = lambda b, i: (b * nq + i, 0)
    col = lambda b, i: (0, b * nq + i)
    k_spec = lambda kind: pl.BlockSpec((seq, gw), lambda b, i: (b, kind))
    vt_spec = pl.BlockSpec((gw, seq), lambda b, i: (0, b))
    n_state = 2 * NSA_GROUPS
    return pl.pallas_call(
        _nsa_attn_kernel, grid=(batch, nq),
        in_specs=[pl.BlockSpec(memory_space=pltpu.SMEM), pl.BlockSpec((hw, tq), col),
                  k_spec(2), k_spec(4), vt_spec, vt_spec,
                  pl.BlockSpec((NSA_GROUPS, n_slc, tq), lambda b, i: (0, 0, b * nq + i)),
                  pl.BlockSpec((tq, LANE), row), pl.BlockSpec((LANE, tq), col), pl.BlockSpec((tq, hw), row)],
        out_specs=pl.BlockSpec((tq, hw), row),
        out_shape=jax.ShapeDtypeStruct((n, hw), BF),
        scratch_shapes=[pltpu.VMEM((NSA_GROUPS, 2 * tq, lanes), F32), pltpu.VMEM((2, NSA_GROUPS, 2 * gw, lanes), BF),
                        pltpu.VMEM((NSA_GROUPS, FAR_CHUNK, lanes), F32), pltpu.VMEM((NSA_GROUPS, FAR_CHUNK, lanes), F32),
                        pltpu.VMEM((NSA_GROUPS, FAR_CHUNK, lanes), F32),
                        pltpu.VMEM((n_state, 1, lanes), F32), pltpu.VMEM((n_state, NSA_DH + DEN_ROWS, lanes), F32)],
        compiler_params=_cparams(("arbitrary", "arbitrary")),
    )(tab, qn_t, kvn, kvn, vs_t, vw_t, sel_t, gn, gn_t, oc)


def _mem_attn_kernel(q_ref, mem_ref, w_ref, o_ref, kv_scr):
    @pl.when(pl.program_id(1) == 0)
    def _():
        kv_scr[...] = _dot(mem_ref[...].astype(BF), w_ref[...]).astype(BF)

    hw = MEM_HEADS * MEM_DH
    for h in range(MEM_HEADS):
        sl = slice(h * MEM_DH, (h + 1) * MEM_DH)
        s = _dot_nt(q_ref[:, sl], kv_scr[:, sl])
        e = jnp.exp(s - jnp.max(s, 1, keepdims=True))
        p = e / jnp.sum(e, 1, keepdims=True)
        o_ref[:, sl] = _dot(p.astype(BF), kv_scr[:, hw + h * MEM_DH:hw + (h + 1) * MEM_DH]).astype(o_ref.dtype)


def _mem_attn(qm, mem2, w_kv, batch, seq, tq):
    n = qm.shape[0]
    nq = seq // tq
    m = mem2.shape[0] // batch
    hw = MEM_HEADS * MEM_DH
    return pl.pallas_call(
        _mem_attn_kernel, grid=(batch, nq),
        in_specs=[pl.BlockSpec((tq, hw), lambda b, i: (b * nq + i, 0)),
                  pl.BlockSpec((m, D_MODEL), lambda b, i: (b, 0)),
                  pl.BlockSpec((D_MODEL, 2 * hw), lambda b, i: (0, 0))],
        out_specs=pl.BlockSpec((tq, hw), lambda b, i: (b * nq + i, 0)),
        out_shape=jax.ShapeDtypeStruct((n, hw), BF),
        scratch_shapes=[pltpu.VMEM((m, 2 * hw), BF)],
        compiler_params=_cparams(("arbitrary", "arbitrary")),
    )(qm, mem2, w_kv)


def _merge_kernel(on_ref, ol_ref, om_ref, gm_ref, h_ref, wb_ref, wo_ref, g_ref, b_ref, h1_ref):
    merged = None
    for c, o_ref in enumerate((on_ref, ol_ref, om_ref)):
        gate = jax.nn.sigmoid(gm_ref[:, c * D_MODEL:(c + 1) * D_MODEL].astype(F32))
        term = gate * _dot(o_ref[...], wb_ref[c])
        merged = term if merged is None else merged + term
    y = ALPHA * h_ref[...] + _dot(merged.astype(BF), wo_ref[...])
    h1_ref[...] = _layer_norm(y, g_ref[...], b_ref[...])


def _merge(o_nsa, o_mla, o_mem, gm, h, wb, wo, g, b, tm):
    n = h.shape[0]
    row = lambda i: (i, 0)
    const = lambda i: (0, 0)
    return pl.pallas_call(
        _merge_kernel, grid=(n // tm,),
        in_specs=[pl.BlockSpec((tm, BRANCH_W), row)] * 3 + [
            pl.BlockSpec((tm, N_BRANCH * D_MODEL), row), pl.BlockSpec((tm, D_MODEL), row),
            pl.BlockSpec(wb.shape, lambda i: (0, 0, 0)), pl.BlockSpec(wo.shape, const),
            pl.BlockSpec((1, D_MODEL), const), pl.BlockSpec((1, D_MODEL), const)],
        out_specs=pl.BlockSpec((tm, D_MODEL), row),
        out_shape=jax.ShapeDtypeStruct((n, D_MODEL), F32),
        compiler_params=_cparams(("parallel",)),
    )(o_nsa, o_mla, o_mem, gm, h, wb, wo, g, b)


def _first_max(vals, idx, limit):
    mx = jnp.max(vals, 0, keepdims=True)
    first = jnp.min(jnp.where(vals == mx, idx, limit), 0, keepdims=True)
    return mx, first


def _router_kernel(h_ref, whi_ref, wlo_ref, b_ref, tri_ref, idx_ref, w_ref, rank_ref, cnt_ref):
    @pl.when(pl.program_id(0) == 0)
    def _():
        cnt_ref[...] = jnp.zeros(cnt_ref.shape, F32)

    h = h_ref[...]
    hhi = h.astype(BF)
    hlo = (h - hhi.astype(F32)).astype(BF)
    whi = whi_ref[...]
    logits = _dot_nt(whi, hhi) + _dot_nt(whi, hlo) + _dot_nt(wlo_ref[...], hhi)
    s = jax.nn.sigmoid(logits)
    sb = s + b_ref[...]
    tm = s.shape[1]
    gsz = N_EXPERTS // N_EXPERT_GROUPS
    e_io = lax.broadcasted_iota(I32, (gsz, tm), 0)
    scores = []
    for g in range(N_EXPERT_GROUPS):
        vals = sb[g * gsz:(g + 1) * gsz]
        m1, first = _first_max(vals, e_io, gsz)
        m2 = jnp.max(jnp.where(e_io == first, -jnp.inf, vals), 0, keepdims=True)
        scores.append(m1 + m2)
    gs = jnp.concatenate(scores, axis=0)
    g_io = lax.broadcasted_iota(I32, (N_EXPERT_GROUPS, tm), 0)
    x_io = lax.broadcasted_iota(I32, (N_EXPERTS, tm), 0)
    allowed = jnp.zeros((N_EXPERTS, tm), jnp.bool_)
    for _ in range(TOPK_GROUPS):
        _, first = _first_max(gs, g_io, N_EXPERT_GROUPS)
        gs = jnp.where(g_io == first, -jnp.inf, gs)
        allowed = allowed | (x_io // gsz == first)
    work = jnp.where(allowed, sb, NEG)
    base = cnt_ref[:, :1]
    tri = tri_ref[...]
    idxs, ws, ranks = [], [], []
    for _ in range(TOP_K):
        _, first = _first_max(work, x_io, N_EXPERTS)
        hit = x_io == first
        idxs.append(first)
        ws.append(jnp.sum(jnp.where(hit, s, 0.0), 0, keepdims=True))
        work = jnp.where(hit, -jnp.inf, work)
        onehot = jnp.where(hit, 1.0, 0.0)
        before = _dot(onehot.astype(BF), tri)
        ranks.append(jnp.sum(jnp.where(hit, base + before, 0.0), 0, keepdims=True))
        base = base + jnp.sum(onehot, 1, keepdims=True)
    wsel = jnp.concatenate(ws, axis=0)
    idx_ref[...] = jnp.concatenate(idxs, axis=0)
    w_ref[...] = wsel / jnp.sum(wsel, 0, keepdims=True) * ROUTE_SCALE
    rank_ref[...] = jnp.concatenate(ranks, axis=0).astype(I32)
    cnt_ref[...] = jnp.broadcast_to(base, cnt_ref.shape)


def _router(h1, whi, wlo, rb, tm):
    n = h1.shape[0]
    tri = jnp.asarray(np.triu(np.ones((tm, tm), np.float32), 1), BF)
    slot = pl.BlockSpec((TOP_K, tm), lambda i: (0, i))
    const = lambda i: (0, 0)
    return pl.pallas_call(
        _router_kernel, grid=(n // tm,),
        in_specs=[pl.BlockSpec((tm, D_MODEL), lambda i: (i, 0)),
                  pl.BlockSpec((N_EXPERTS, D_MODEL), const), pl.BlockSpec((N_EXPERTS, D_MODEL), const),
                  pl.BlockSpec((N_EXPERTS, 1), const), pl.BlockSpec((tm, tm), const)],
        out_specs=[slot, slot, slot, pl.BlockSpec((N_EXPERTS, LANE), const)],
        out_shape=[jax.ShapeDtypeStruct((TOP_K, n), I32), jax.ShapeDtypeStruct((TOP_K, n), F32),
                   jax.ShapeDtypeStruct((TOP_K, n), I32), jax.ShapeDtypeStruct((N_EXPERTS, LANE), F32)],
        compiler_params=_cparams(("arbitrary",)),
    )(h1, whi, wlo, rb, tri)


def _pos_kernel(idx_ref, rank_ref, start_ref, pos_ref):
    tm = idx_ref.shape[1]
    x_io = lax.broadcasted_iota(I32, (N_EXPERTS, tm), 0)
    start = start_ref[...]
    rows = [jnp.sum(jnp.where(x_io == idx_ref[k:k + 1, :], start, 0.0), 0, keepdims=True) for k in range(TOP_K)]
    pos_ref[...] = jnp.concatenate(rows, axis=0).astype(I32) + rank_ref[...]


def _positions(eidx_t, rank_t, pad_start, tm):
    n = eidx_t.shape[1]
    slot = pl.BlockSpec((TOP_K, tm), lambda i: (0, i))
    return pl.pallas_call(
        _pos_kernel, grid=(n // tm,),
        in_specs=[slot, slot, pl.BlockSpec((N_EXPERTS, 1), lambda i: (0, 0))],
        out_specs=slot, out_shape=jax.ShapeDtypeStruct((TOP_K, n), I32),
        compiler_params=_cparams(("parallel",)),
    )(eidx_t, rank_t, pad_start)


ROW_UNROLL = 8
PACKED_W = D_MODEL // 2
U32 = jnp.uint32


def _fill_groups():
    p = EXPERT_BLOCK // 2
    while p >= 1:
        yield p
        p //= 2


def _pack_rows(x):
    bits = lambda v: lax.bitcast_convert_type(v.astype(BF).astype(F32), U32)
    return (bits(x[:, :PACKED_W]) >> 16) | (bits(x[:, PACKED_W:]) & U32(0xFFFF0000))


def _unpack_rows(u):
    return (lax.bitcast_convert_type(u << 16, F32), lax.bitcast_convert_type(u & U32(0xFFFF0000), F32))


def _permute_kernel(fill_start_ref, fill_n_ref, pos_ref, h_ref, xs_hbm, xp, zbuf, row_sem, fill_sem, *, fills_per_step):
    i = pl.program_id(0)
    tm = h_ref.shape[0]
    n_fills = fill_n_ref.shape[0]
    zbuf[...] = jnp.zeros(zbuf.shape, zbuf.dtype)
    xp[...] = _pack_rows(h_ref[...])

    def fill_copies(e):
        n = fill_n_ref[e]
        start = fill_start_ref[e]
        for p in _fill_groups():
            @pl.when((n & p) != 0)
            def _(p=p):
                off = start + (n & (p - 1))
                if p < SUBLANES:
                    for r in range(p):
                        pltpu.make_async_copy(zbuf.at[pl.ds(r, 1)], xs_hbm.at[pl.ds(off + r, 1)], fill_sem).start()
                else:
                    pltpu.make_async_copy(zbuf.at[pl.ds(0, p)], xs_hbm.at[pl.ds(pl.multiple_of(off, SUBLANES), p)],
                                          fill_sem).start()

    def fill_waits(e):
        n = fill_n_ref[e]
        for p in _fill_groups():
            @pl.when((n & p) != 0)
            def _(p=p):
                pltpu.make_async_copy(zbuf.at[pl.ds(0, p)], zbuf.at[pl.ds(0, p)], fill_sem).wait()

    for q in range(fills_per_step):
        e = i * fills_per_step + q

        @pl.when(e < n_fills)
        def _(e=e):
            fill_copies(e)

    for t in range(tm):
        for k in range(TOP_K):
            pltpu.make_async_copy(xp.at[pl.ds(t, 1)], xs_hbm.at[pl.ds(pos_ref[k, t], 1)], row_sem).start()

    for q in range(fills_per_step):
        e = i * fills_per_step + q

        @pl.when(e < n_fills)
        def _(e=e):
            fill_waits(e)

    pltpu.make_async_copy(xs_hbm.at[pl.ds(0, TOP_K * tm)], xs_hbm.at[pl.ds(0, TOP_K * tm)], row_sem).wait()


def _permute(fill_start, fill_n, pos, h1, n_rows, tm):
    n = h1.shape[0]
    nt = n // tm
    grid_spec = pltpu.PrefetchScalarGridSpec(
        num_scalar_prefetch=2, grid=(nt,),
        in_specs=[pl.BlockSpec((TOP_K, tm), lambda i, fs, fn: (0, i), memory_space=pltpu.SMEM),
                  pl.BlockSpec((tm, D_MODEL), lambda i, fs, fn: (i, 0))],
        out_specs=pl.BlockSpec(memory_space=pl.ANY),
        scratch_shapes=[pltpu.VMEM((tm, PACKED_W), U32), pltpu.VMEM((EXPERT_BLOCK // 2, PACKED_W), U32),
                        pltpu.SemaphoreType.DMA(()), pltpu.SemaphoreType.DMA(())])
    return pl.pallas_call(
        functools.partial(_permute_kernel, fills_per_step=-(-fill_n.shape[0] // nt)), grid_spec=grid_spec,
        out_shape=jax.ShapeDtypeStruct((n_rows, PACKED_W), U32),
        compiler_params=_cparams(("arbitrary",)),
    )(fill_start, fill_n, pos, h1)


def _experts_kernel(blk_e_ref, nused_ref, next_e_ref, x_ref, w1_hbm, w3_hbm, w2_hbm, y_ref,
                    w1f, w3f, w2f, w1b, w3b, w2b, ord_ref, sem):
    j = pl.program_id(0)

    def weight_copies(e, slot):
        return (pltpu.make_async_copy(w1_hbm.at[e], w1f.at[slot], sem.at[slot, 0]),
                pltpu.make_async_copy(w3_hbm.at[e], w3f.at[slot], sem.at[slot, 1]),
                pltpu.make_async_copy(w2_hbm.at[e], w2f.at[slot], sem.at[slot, 2]))

    @pl.when(j < nused_ref[0])
    def _():
        e = blk_e_ref[j]
        first_of_expert = (j == 0) | (e != blk_e_ref[jnp.maximum(j - 1, 0)])

        @pl.when(first_of_expert)
        def _():
            @pl.when(j == 0)
            def _():
                ord_ref[0] = 0
                for c in weight_copies(e, 0):
                    c.start()

            n = ord_ref[0]
            slot = n % 2
            for c in weight_copies(e, slot):
                c.wait()
            nxt = next_e_ref[e]

            @pl.when(nxt >= 0)
            def _():
                for c in weight_copies(nxt, 1 - slot):
                    c.start()

            w1b[...] = w1f[slot].astype(BF)
            w3b[...] = w3f[slot].astype(BF)
            w2b[...] = w2f[slot].astype(BF)
            ord_ref[0] = n + 1

        x = jnp.concatenate([half.astype(BF) for half in _unpack_rows(x_ref[...])], axis=1)
        a = _dot(x, w1b[...])
        y_ref[...] = _pack_rows(_dot((a * jax.nn.sigmoid(a) * _dot(x, w3b[...])).astype(BF), w2b[...]))

    @pl.when(j >= nused_ref[0])
    def _():
        y_ref[...] = jnp.zeros(y_ref.shape, y_ref.dtype)


def _experts(blk_e, nused, next_e, xs, w1, w3, w2):
    n_blocks = blk_e.shape[0]
    rb = EXPERT_BLOCK
    anyspec = pl.BlockSpec(memory_space=pl.ANY)
    grid_spec = pltpu.PrefetchScalarGridSpec(
        num_scalar_prefetch=3, grid=(n_blocks,),
        in_specs=[pl.BlockSpec((rb, PACKED_W), lambda j, be, nu, ne: (jnp.minimum(j, nu[0] - 1), 0)),
                  anyspec, anyspec, anyspec],
        out_specs=pl.BlockSpec((rb, PACKED_W), lambda j, be, nu, ne: (j, 0)),
        scratch_shapes=[pltpu.VMEM((2, D_MODEL, D_EXPERT), F32), pltpu.VMEM((2, D_MODEL, D_EXPERT), F32),
                        pltpu.VMEM((2, D_EXPERT, D_MODEL), F32),
                        pltpu.VMEM((D_MODEL, D_EXPERT), BF), pltpu.VMEM((D_MODEL, D_EXPERT), BF),
                        pltpu.VMEM((D_EXPERT, D_MODEL), BF),
                        pltpu.SMEM((1,), I32), pltpu.SemaphoreType.DMA((2, 3))])
    return pl.pallas_call(
        _experts_kernel, grid_spec=grid_spec,
        out_shape=jax.ShapeDtypeStruct((n_blocks * rb, PACKED_W), U32),
        compiler_params=_cparams(("arbitrary",)),
    )(blk_e, nused, next_e, xs, w1, w3, w2)


def _combine_kernel(pos_ref, pos_next_ref, ys_hbm, w_ref, h_ref, s1_ref, s3_ref, s2_ref, g_ref, b_ref, o_ref, ybuf, sem):
    i = pl.program_id(0)
    nt = pl.num_programs(0)
    tm = h_ref.shape[0]

    def start_gather(p_ref, slot):
        def body(t, c):
            for k in range(TOP_K):
                pltpu.make_async_copy(ys_hbm.at[pl.ds(p_ref[k, t], 1)], ybuf.at[slot, k, pl.ds(t, 1)], sem.at[slot]).start()
            return c

        lax.fori_loop(0, tm, body, 0, unroll=ROW_UNROLL)

    @pl.when(i == 0)
    def _():
        start_gather(pos_ref, 0)

    slot = i % 2
    pltpu.make_async_copy(ybuf.at[slot], ybuf.at[slot], sem.at[slot]).wait()

    for t in range(tm):
        for k in range(TOP_K):
            pltpu.make_async_copy(ys_hbm.at[pl.ds(pos_next_ref[k, t], 1)], ybuf.at[1 - slot, k, pl.ds(t, 1)],
                                  sem.at[1 - slot]).start()

    h = h_ref[...]
    hb = h.astype(BF)
    a = _dot(hb, s1_ref[...])
    y = ALPHA * h + _dot((a * jax.nn.sigmoid(a) * _dot(hb, s3_ref[...])).astype(BF), s2_ref[...])
    w = w_ref[...]
    routed = None
    for k in range(TOP_K):
        terms = [w[:, k:k + 1] * half for half in _unpack_rows(ybuf[slot, k])]
        routed = terms if routed is None else [r + t for r, t in zip(routed, terms)]
    y = y + jnp.concatenate(routed, axis=1)
    o_ref[...] = _layer_norm(y, g_ref[...], b_ref[...])

    @pl.when(i == nt - 1)
    def _():
        pltpu.make_async_copy(ybuf.at[1 - slot], ybuf.at[1 - slot], sem.at[1 - slot]).wait()


def _combine(pos, ys, w, h1, s1, s3, s2, g, b, tm):
    n = h1.shape[0]
    nt = n // tm
    row = lambda i: (i, 0)
    const = lambda i: (0, 0)
    return pl.pallas_call(
        _combine_kernel, grid=(nt,),
        in_specs=[pl.BlockSpec((TOP_K, tm), lambda i: (0, i), memory_space=pltpu.SMEM),
                  pl.BlockSpec((TOP_K, tm), lambda i: (0, jnp.minimum(i + 1, nt - 1)), memory_space=pltpu.SMEM),
                  pl.BlockSpec(memory_space=pl.ANY),
                  pl.BlockSpec((tm, TOP_K), row), pl.BlockSpec((tm, D_MODEL), row),
                  pl.BlockSpec(s1.shape, const), pl.BlockSpec(s3.shape, const), pl.BlockSpec(s2.shape, const),
                  pl.BlockSpec((1, D_MODEL), const), pl.BlockSpec((1, D_MODEL), const)],
        out_specs=pl.BlockSpec((tm, D_MODEL), row),
        out_shape=jax.ShapeDtypeStruct((n, D_MODEL), F32),
        scratch_shapes=[pltpu.VMEM((2, TOP_K, tm, PACKED_W), U32), pltpu.SemaphoreType.DMA((2,))],
        compiler_params=_cparams(("arbitrary",)),
    )(pos, pos, ys, w, h1, s1, s3, s2, g, b)


def _overlap_matrix(nc, n_slc):
    cs = np.arange(nc) * CMP_STRIDE
    ce = cs + CMP_LEN - 1
    js = np.arange(n_slc) * SLC_LEN
    je = js + SLC_LEN - 1
    ov = ((cs[:, None] <= je[None, :]) & (ce[:, None] >= js[None, :])).astype(np.float32)
    ov[nc - 1] = 0.0
    return ov


def _block_layout(counts, n):
    rb = EXPERT_BLOCK
    counts = counts.astype(I32)
    padded = (counts + rb - 1) // rb * rb
    pad_end = jnp.cumsum(padded)
    pad_start = pad_end - padded
    n_blocks = -(-TOP_K * n // rb) + N_EXPERTS
    blk_start = jnp.arange(n_blocks, dtype=I32) * rb
    blk_e = jnp.minimum(jnp.sum(pad_end[None, :] <= blk_start[:, None], axis=1), N_EXPERTS - 1).astype(I32)
    nused = (pad_end[-1] // rb).astype(I32).reshape(1)
    half = rb // 2
    tail_start = pad_end[-1] + half * jnp.arange(2 * N_EXPERTS, dtype=I32)
    tail_n = jnp.where(tail_start < n_blocks * rb, half, 0).astype(I32)
    fill_start = jnp.concatenate([pad_start + counts, jnp.minimum(tail_start, n_blocks * rb - half)])
    fill_n = jnp.concatenate([padded - counts, tail_n])
    ids = jnp.where(counts > 0, jnp.arange(N_EXPERTS, dtype=I32), N_EXPERTS)
    at_or_after = lax.cummin(ids[::-1])[::-1]
    nxt = jnp.concatenate([at_or_after[1:], jnp.full((1,), N_EXPERTS, I32)])
    next_e = jnp.where(nxt < N_EXPERTS, nxt, -1).astype(I32)
    return pad_start, fill_start, fill_n, blk_e, nused, next_e


def kernel(x, mem, ln0_g, ln0_b, rel_bias, w_in, cmp_pos_k, cmp_pos_v, cmp_k_w1, cmp_k_w2, cmp_v_w1, cmp_v_w2, mla_q_norm, mla_w_uq, mla_kv_norm, mla_w_ukv, mem_w_kv, w_branch, w_out, ln1_g, ln1_b, router_w, router_b, exp_w1, exp_w3, exp_w2, sh_w1, sh_w3, sh_w2, ln2_g, ln2_b):
    batch, seq, d = x.shape
    n = batch * seq
    l = 0
    row2 = lambda v: v.reshape(1, -1)
    tm = min(256, seq)

    pts = np.cumsum((0,) + IN_SPLITS)
    wcol = lambda k: w_in[l][:, pts[k]:pts[k + 1]]
    pad_cols = lambda w, lo, tot: jnp.pad(w, ((0, 0), (lo, tot - lo - w.shape[1])))
    ws = [wcol(0) * (NSA_DH ** -0.5 * LOG2E), wcol(1), pad_cols(wcol(2), 0, LANE), wcol(3), wcol(4),
          pad_cols(wcol(5), ROPE_LANE0, LANE), wcol(6), wcol(7)]
    ws = [w.astype(BF) for w in ws]
    scales = [1.0] * 6 + [MEM_DH ** -0.5, 1.0]
    gw = NSA_GROUPS * NSA_DH
    ws_t = [wcol(0) * (NSA_DH ** -0.5 * LOG2E), wcol(1)[:, 3 * gw:4 * gw], wcol(1)[:, 5 * gw:6 * gw],
            pad_cols(wcol(2), 0, LANE)]
    ws_t = [w.T.astype(BF) for w in ws_t]
    hd = MLA_NOPE + MLA_ROPE
    wuq = jnp.pad(mla_w_uq[l].reshape(MLA_Q_LORA, MLA_HEADS, hd), ((0, 0), (0, 0), (0, HEAD_PAD - hd)))
    wuq = wuq.reshape(MLA_Q_LORA, MLA_HEADS * HEAD_PAD).astype(BF)
    wukv = mla_w_ukv[l].reshape(MLA_KV_LORA, MLA_HEADS, MLA_NOPE + MLA_DV)
    wuk = jnp.pad(wukv[:, :, :MLA_NOPE], ((0, 0), (0, 0), (0, HEAD_PAD - MLA_NOPE)))
    wuk = wuk.reshape(MLA_KV_LORA, MLA_HEADS * HEAD_PAD).astype(BF)
    wuv = wukv[:, :, MLA_NOPE:].reshape(MLA_KV_LORA, MLA_HEADS * MLA_DV).astype(BF)

    h, qn, kvn, gn, cq, ckv, kr, qm, gm, qn_t, vs_t, vw_t, gn_t = _ln_inproj(
        x.reshape(n, d), row2(ln0_g), row2(ln0_b), ws, scales, ws_t, tm)

    q_mla, k_mla, v_mla = _mla_prep(cq, ckv, kr, row2(mla_q_norm[l]), row2(mla_kv_norm[l]), wuq, wuk, wuv, seq, tm)
    o_mla = _mla_attn(q_mla, k_mla, v_mla, batch, seq, min(MLA_TQ, seq))

    nc = seq // CMP_STRIDE
    n_slc = seq // SLC_LEN
    cc = kvn[:, :2 * gw].reshape(batch, nc, CMP_STRIDE, 2 * NSA_GROUPS, NSA_DH)
    cc = cc.transpose(0, 3, 1, 2, 4).reshape(batch, 2 * NSA_GROUPS, nc, CMP_STRIDE * NSA_DH)
    pos = jnp.stack([cmp_pos_k[l], cmp_pos_v[l]]).reshape(2, 1, CMP_LEN * NSA_DH)
    w1c = jnp.stack([cmp_k_w1[l], cmp_v_w1[l]]).astype(BF)
    w2c = jnp.stack([cmp_k_w2[l], cmp_v_w2[l]]).astype(BF)
    kvc = _nsa_compress(cc, pos, w1c, w2c)
    overlap = jnp.asarray(_overlap_matrix(nc, n_slc), BF)
    o_cmp, sel_t = _nsa_cmp(rel_bias, qn, kvc, overlap, batch, seq)
    o_nsa = _nsa_attn(rel_bias, qn_t, kvn, vs_t, vw_t, sel_t, gn, gn_t, o_cmp, batch, seq)

    o_mem = _mem_attn(qm, mem.reshape(-1, d), mem_w_kv[l].astype(BF), batch, seq, min(512, seq))

    h1 = _merge(o_nsa, o_mla, o_mem, gm, h, w_branch[l].astype(BF), w_out[l].astype(BF),
                row2(ln1_g[l]), row2(ln1_b[l]), tm)

    rw_t = router_w[l].T
    rw_hi = rw_t.astype(BF)
    rw_lo = (rw_t - rw_hi.astype(F32)).astype(BF)
    eidx_t, w_t, rank_t, counts = _router(h1, rw_hi, rw_lo, router_b[l].reshape(N_EXPERTS, 1), tm)
    pad_start, fill_start, fill_n, blk_e, nused, next_e = _block_layout(counts[:, 0], n)
    pos = _positions(eidx_t, rank_t, pad_start.astype(F32).reshape(N_EXPERTS, 1), tm)
    xs = _permute(fill_start, fill_n, pos, h1, blk_e.shape[0] * EXPERT_BLOCK, tm)
    ys = _experts(blk_e, nused, next_e, xs, exp_w1[l], exp_w3[l], exp_w2[l])
    out = _combine(pos, ys, w_t.T, h1, sh_w1[l].astype(BF), sh_w3[l].astype(BF), sh_w2[l].astype(BF),
                   row2(ln2_g[l]), row2(ln2_b[l]), min(128, seq))
    return out.reshape(batch, seq, d)
```

```python
import functools
import math

import numpy as np
import jax
import jax.numpy as jnp
from jax import lax
from jax.experimental import pallas as pl
from jax.experimental.pallas import tpu as pltpu

BF = jnp.bfloat16
F32 = jnp.float32
I32 = jnp.int32

D_MODEL = 1024
DEPTH = 1
NSA_HEADS = 8
NSA_GROUPS = 2
NSA_HPG = NSA_HEADS // NSA_GROUPS
NSA_DH = 64
CMP_LEN = 32
CMP_STRIDE = 16
CMP_HID = 256
SLC_LEN = 64
SLC_TOPN = 16
WIN = 512
MLA_HEADS = 8
MLA_NOPE = 64
MLA_ROPE = 32
MLA_DV = 64
MLA_Q_LORA = 768
MLA_KV_LORA = 256
ROPE_THETA = 10000.0
MEM_HEADS = 4
MEM_DH = 128
N_BRANCH = 3
BRANCH_W = NSA_HEADS * NSA_DH
REL_BUCKETS = 32
REL_MAX_DIST = 128
N_EXPERTS = 256
TOP_K = 8
N_EXPERT_GROUPS = 8
TOPK_GROUPS = 4
D_EXPERT = 256
ROUTE_SCALE = 2.5
EXPERT_BLOCK = 512
LN_EPS = 1e-5
RMS_EPS = 1e-6
NEG = -1e30
BIG = 1e30
ALPHA = (2 * DEPTH) ** 0.25
IN_SPLITS = (NSA_HEADS * NSA_DH, 6 * NSA_GROUPS * NSA_DH, 3 * NSA_HEADS, MLA_Q_LORA, MLA_KV_LORA,
             MLA_ROPE, MEM_HEADS * MEM_DH, N_BRANCH * D_MODEL)

LANE = 128
SUBLANES = 8
HEAD_PAD = 128
ROPE_LANE0 = MLA_NOPE
ROPE_HALF = MLA_ROPE // 2
VMEM_LIMIT = 56 * 1024 * 1024
NSA_TQ = 128
MLA_TQ = 512
FAR_CHUNK = 512


def _cparams(sem):
    return pltpu.CompilerParams(dimension_semantics=sem, vmem_limit_bytes=VMEM_LIMIT)


def _bucket_starts():
    max_exact = REL_BUCKETS // 2
    d = np.arange(0, 4 * REL_MAX_DIST)
    nf = np.maximum(d, 1).astype(np.float32)
    large = max_exact + (np.log(nf / np.float32(max_exact)) / np.float32(math.log(REL_MAX_DIST / max_exact))
                         * np.float32(REL_BUCKETS - max_exact)).astype(np.int32)
    large = np.minimum(large, REL_BUCKETS - 1)
    bucket = np.where(d < max_exact, d, large)
    return [int(np.argmax(bucket >= b)) for b in range(REL_BUCKETS)]


BUCKET_START = _bucket_starts()
FAR_DIST = BUCKET_START[REL_BUCKETS - 1]


def _rel_bias(dist, tab_ref, head):
    val = jnp.full(dist.shape, tab_ref[REL_BUCKETS - 1, head], F32)
    for b in range(REL_BUCKETS - 2, -1, -1):
        val = jnp.where(dist < BUCKET_START[b + 1], tab_ref[b, head], val)
    return val


def _layer_norm(x, g, b):
    mu = jnp.mean(x, -1, keepdims=True)
    xc = x - mu
    var = jnp.mean(xc * xc, -1, keepdims=True)
    return xc * lax.rsqrt(var + LN_EPS) * g + b


def _dot(a, b):
    return jnp.dot(a, b, preferred_element_type=F32)


def _dot_nt(a, b):
    return lax.dot_general(a, b, (((1,), (1,)), ((), ())), preferred_element_type=F32)


def _ln_inproj_kernel(*refs, scales, n_rowmajor):
    x_ref, g_ref, b_ref = refs[:3]
    n_w = (len(refs) - 4) // 2
    w_refs = refs[3:3 + n_w]
    h_ref = refs[3 + n_w]
    o_refs = refs[4 + n_w:]
    h = _layer_norm(x_ref[...], g_ref[...], b_ref[...])
    h_ref[...] = h
    hb = h.astype(BF)
    for j, (w, o) in enumerate(zip(w_refs, o_refs)):
        if j < n_rowmajor:
            y = _dot(hb, w[...])
            o[...] = (y if scales[j] == 1.0 else y * scales[j]).astype(o.dtype)
        else:
            o[...] = _dot_nt(w[...], hb).astype(o.dtype)


def _ln_inproj(x2, g, b, ws, scales, ws_t, tm):
    n = x2.shape[0]
    row = lambda i: (i, 0)
    col = lambda i: (0, i)
    const = lambda i: (0, 0)
    in_specs = [pl.BlockSpec((tm, D_MODEL), row), pl.BlockSpec((1, D_MODEL), const), pl.BlockSpec((1, D_MODEL), const)]
    in_specs += [pl.BlockSpec(w.shape, const) for w in ws + ws_t]
    out_shape = [jax.ShapeDtypeStruct((n, D_MODEL), F32)]
    out_shape += [jax.ShapeDtypeStruct((n, w.shape[1]), BF) for w in ws]
    out_shape += [jax.ShapeDtypeStruct((w.shape[0], n), BF) for w in ws_t]
    out_specs = [pl.BlockSpec((tm, D_MODEL), row)] + [pl.BlockSpec((tm, w.shape[1]), row) for w in ws]
    out_specs += [pl.BlockSpec((w.shape[0], tm), col) for w in ws_t]
    return pl.pallas_call(
        functools.partial(_ln_inproj_kernel, scales=tuple(scales), n_rowmajor=len(ws)),
        grid=(n // tm,), in_specs=in_specs, out_specs=out_specs, out_shape=out_shape,
        compiler_params=_cparams(("parallel",)),
    )(x2, g, b, *ws, *ws_t)


def _rope_lanes(x, c, s1, s2):
    return x * c + pltpu.roll(x, LANE - ROPE_HALF, 1) * s1 + pltpu.roll(x, ROPE_HALF, 1) * s2


def _mla_prep_kernel(cq_ref, ckv_ref, kr_ref, qn_ref, kvn_ref, wuq, wuk, wuv,
                     cq_t, s1q_t, s2q_t, ck_t, s1k_t, s2k_t, q_out, k_out, v_out):
    cq = cq_ref[...].astype(F32)
    rq = cq * lax.rsqrt(jnp.mean(cq * cq, -1, keepdims=True) + RMS_EPS) * qn_ref[...]
    q = _dot(rq.astype(BF), wuq[...])
    ckv = ckv_ref[...].astype(F32)
    rkv = (ckv * lax.rsqrt(jnp.mean(ckv * ckv, -1, keepdims=True) + RMS_EPS) * kvn_ref[...]).astype(BF)
    kn = _dot(rkv, wuk[...])
    v = _dot(rkv, wuv[...])
    pair_w = 2 * MLA_DV
    ones = jnp.ones((v.shape[0], pair_w), v_out.dtype)
    for p in range(MLA_HEADS // 2):
        v_out[:, 2 * p * pair_w:(2 * p + 1) * pair_w] = v[:, p * pair_w:(p + 1) * pair_w].astype(v_out.dtype)
        v_out[:, (2 * p + 1) * pair_w:(2 * p + 2) * pair_w] = ones
    kr = _rope_lanes(kr_ref[...].astype(F32), ck_t[...], s1k_t[...], s2k_t[...])
    cq_c, s1q, s2q = cq_t[...], s1q_t[...], s2q_t[...]
    for h in range(MLA_HEADS):
        sl = slice(h * HEAD_PAD, (h + 1) * HEAD_PAD)
        q_out[:, sl] = _rope_lanes(q[:, sl], cq_c, s1q, s2q).astype(q_out.dtype)
        k_out[:, sl] = (kn[:, sl] + kr).astype(k_out.dtype)


def _rope_tables(seq, scale):
    freq = ROPE_THETA ** (-jnp.arange(ROPE_HALF, dtype=F32) / ROPE_HALF)
    ang = jnp.arange(seq, dtype=F32)[:, None] * freq[None, :]
    cos, sin = jnp.cos(ang) * scale, jnp.sin(ang) * scale
    z = lambda w: jnp.zeros((seq, w), F32)
    tail = HEAD_PAD - ROPE_LANE0 - MLA_ROPE
    c = jnp.concatenate([jnp.full((seq, ROPE_LANE0), scale, F32), cos, cos, z(tail)], 1)
    s1 = jnp.concatenate([z(ROPE_LANE0), -sin, z(ROPE_HALF + tail)], 1)
    s2 = jnp.concatenate([z(ROPE_LANE0 + ROPE_HALF), sin, z(tail)], 1)
    return c, s1, s2


def _mla_prep(cq, ckv, kr, q_norm, kv_norm, wuq, wuk, wuv, seq, tm):
    n = cq.shape[0]
    nt = seq // tm
    row = lambda i: (i, 0)
    const = lambda i: (0, 0)
    pos = lambda i: (i % nt, 0)
    tabs = _rope_tables(seq, (MLA_NOPE + MLA_ROPE) ** -0.5 * math.log2(math.e)) + _rope_tables(seq, 1.0)
    in_specs = [pl.BlockSpec((tm, MLA_Q_LORA), row), pl.BlockSpec((tm, MLA_KV_LORA), row), pl.BlockSpec((tm, LANE), row),
                pl.BlockSpec((1, MLA_Q_LORA), const), pl.BlockSpec((1, MLA_KV_LORA), const),
                pl.BlockSpec(wuq.shape, const), pl.BlockSpec(wuk.shape, const), pl.BlockSpec(wuv.shape, const)]
    in_specs += [pl.BlockSpec((tm, LANE), pos)] * 6
    hq = MLA_HEADS * HEAD_PAD
    hv = 2 * MLA_HEADS * MLA_DV
    return pl.pallas_call(
        _mla_prep_kernel, grid=(n // tm,), in_specs=in_specs,
        out_specs=[pl.BlockSpec((tm, hq), row), pl.BlockSpec((tm, hq), row), pl.BlockSpec((tm, hv), row)],
        out_shape=[jax.ShapeDtypeStruct((n, hq), BF), jax.ShapeDtypeStruct((n, hq), BF), jax.ShapeDtypeStruct((n, hv), BF)],
        compiler_params=_cparams(("parallel",)),
    )(cq, ckv, kr, q_norm, kv_norm, wuq, wuk, wuv, *tabs)


def _mla_attn_kernel(q_ref, k_ref, v_ref, o_ref, sa_scr, sb_scr, m_scr, acc_scr):
    i = pl.program_id(2)
    t = q_ref.shape[0]
    reps = t // LANE
    pair_w = 2 * MLA_DV
    m_scr[...] = jnp.full(m_scr.shape, NEG, F32)
    acc_scr[...] = jnp.zeros(acc_scr.shape, F32)

    def logits(tile, s_scr):
        kstart = pl.multiple_of(tile * t, t)
        for hh in range(2):
            sl = slice(hh * HEAD_PAD, (hh + 1) * HEAD_PAD)
            s_scr[hh] = _dot_nt(q_ref[:, sl], k_ref[pl.ds(kstart, t), sl])

    def consume(tile, s_scr, diagonal):
        v = v_ref[pl.ds(pl.multiple_of(tile * t, t), t), :]
        for hh in range(2):
            s = s_scr[hh]
            if diagonal:
                row = lax.broadcasted_iota(I32, (t, t), 0)
                col = lax.broadcasted_iota(I32, (t, t), 1)
                s = jnp.where(col <= row, s, NEG)
            m_prev = m_scr[hh]
            m_new = jnp.maximum(m_prev, jnp.max(s, 1, keepdims=True))
            a = jnp.exp2(m_prev - m_new)
            e = jnp.exp2(s - jnp.tile(m_new, (1, reps)))
            acc_scr[hh] = jnp.tile(a, (1, 2)) * acc_scr[hh] + _dot(e.astype(BF), v)
            m_scr[hh] = m_new

    logits(0, sa_scr)

    def body(p, carry):
        logits(2 * p + 1, sb_scr)
        consume(2 * p, sa_scr, False)
        logits(2 * p + 2, sa_scr)
        consume(2 * p + 1, sb_scr, False)
        return carry

    lax.fori_loop(0, i // 2, body, 0)

    @pl.when(i % 2 == 0)
    def _():
        consume(i, sa_scr, True)

    @pl.when(i % 2 == 1)
    def _():
        logits(i, sb_scr)
        consume(i - 1, sa_scr, False)
        consume(i, sb_scr, True)

    lane = lax.broadcasted_iota(I32, (t, pair_w), 1)
    o = jnp.where(lane < MLA_DV, acc_scr[0, :, :pair_w] / acc_scr[0, :, pair_w:],
                  acc_scr[1, :, :pair_w] / acc_scr[1, :, pair_w:])
    o_ref[...] = o.astype(o_ref.dtype)


def _mla_attn(q, k, v, batch, seq, t):
    n = q.shape[0]
    nt = seq // t
    qmap = lambda b, hp, i: (b * nt + i, hp)
    kmap = lambda b, hp, i: (b, hp)
    return pl.pallas_call(
        _mla_attn_kernel, grid=(batch, MLA_HEADS // 2, nt),
        in_specs=[pl.BlockSpec((t, 2 * HEAD_PAD), qmap), pl.BlockSpec((seq, 2 * HEAD_PAD), kmap),
                  pl.BlockSpec((seq, 4 * MLA_DV), kmap)],
        out_specs=pl.BlockSpec((t, 2 * MLA_DV), qmap),
        out_shape=jax.ShapeDtypeStruct((n, MLA_HEADS * MLA_DV), BF),
        scratch_shapes=[pltpu.VMEM((2, t, t), F32), pltpu.VMEM((2, t, t), F32),
                        pltpu.VMEM((2, t, LANE), F32), pltpu.VMEM((2, t, 4 * MLA_DV), F32)],
        compiler_params=_cparams(("parallel", "parallel", "arbitrary")),
    )(q, k, v)


def _nsa_compress_kernel(c_ref, pos_ref, w1_ref, w2_ref, o_ref):
    nc = c_ref.shape[0]
    half = CMP_STRIDE * NSA_DH
    c = c_ref[...]
    top = _dot(c, w1_ref[:half, :])
    bot = _dot(c, w1_ref[half:, :])
    posb = _dot(jnp.broadcast_to(pos_ref[...], (8, 2 * half)).astype(BF), w1_ref[...])[:1]
    hid = top + pltpu.roll(bot, nc - 1, 0) + posb
    o_ref[...] = _dot(jax.nn.gelu(hid).astype(BF), w2_ref[...]).astype(o_ref.dtype)


def _nsa_compress(cc, pos, w1, w2):
    b, _, nc, half = cc.shape
    sq = pl.Squeezed()
    return pl.pallas_call(
        _nsa_compress_kernel, grid=(b, 2 * NSA_GROUPS),
        in_specs=[pl.BlockSpec((sq, sq, nc, half), lambda i, c: (i, c, 0, 0)),
                  pl.BlockSpec((sq, 1, 2 * half), lambda i, c: (c // NSA_GROUPS, 0, 0)),
                  pl.BlockSpec((sq, 2 * half, CMP_HID), lambda i, c: (c // NSA_GROUPS, 0, 0)),
                  pl.BlockSpec((sq, CMP_HID, NSA_DH), lambda i, c: (c // NSA_GROUPS, 0, 0))],
        out_specs=pl.BlockSpec((sq, sq, nc, NSA_DH), lambda i, c: (i, c, 0, 0)),
        out_shape=jax.ShapeDtypeStruct((b, 2 * NSA_GROUPS, nc, NSA_DH), BF),
        compiler_params=_cparams(("parallel", "parallel")),
    )(cc, pos, w1, w2)


CMP_TQ = 512
CMP_BIAS_COLS = LANE // 2
CMP_BIAS_BACK = -(-(FAR_DIST + CMP_LEN - 1) // CMP_STRIDE)
assert (CMP_TQ - CMP_LEN) // CMP_STRIDE + CMP_BIAS_BACK < CMP_BIAS_COLS


def _nsa_cmp_kernel(tab_ref, q_ref, kc_ref, vc_ref, ov_ref, oc_ref, sel_ref, e_scr, *, n_top):
    b, g, i = pl.program_id(0), pl.program_id(1), pl.program_id(2)
    tq = q_ref.shape[0]
    nc = kc_ref.shape[0]
    n_slc = ov_ref.shape[1]
    qs = i * tq

    @pl.when((b == 0) & (g == 0) & (i == 0))
    def _():
        q_io = lax.broadcasted_iota(I32, (tq, LANE), 0)
        lane = lax.broadcasted_iota(I32, (tq, LANE), 1)
        jj = lane % CMP_BIAS_COLS
        dist = q_io - CMP_STRIDE * (jj - CMP_BIAS_BACK) - (CMP_LEN - 1)
        live = (dist >= 0) & (lane < 2 * CMP_BIAS_COLS)
        for h in range(NSA_HEADS):
            e = jnp.where(live, (_rel_bias(dist, tab_ref, h) - tab_ref[REL_BUCKETS - 1, h]) * LOG2E, 0.0)
            hi = e.astype(BF)
            lo = (e - hi.astype(F32)).astype(BF)
            e_scr[h] = jnp.where(lane < CMP_BIAS_COLS, hi, lo)

    n0 = qs // CMP_STRIDE
    nrow = lax.broadcasted_iota(I32, (nc, LANE), 0)
    jcol = lax.broadcasted_iota(I32, (nc, LANE), 1)
    near_blk = jnp.where(nrow == n0 + (jcol % CMP_BIAS_COLS) - CMP_BIAS_BACK, 1.0, 0.0).astype(BF)
    kc_aug = jnp.concatenate([near_blk, kc_ref[...]], axis=1)

    t = qs + lax.broadcasted_iota(I32, (tq, nc), 0)
    n_io = lax.broadcasted_iota(I32, (tq, nc), 1)
    mask_add = jnp.where((t >= n_io * CMP_STRIDE + (CMP_LEN - 1)) & (n_io < nc - 1), 0.0, NEG)
    has_key = qs + lax.broadcasted_iota(I32, (tq, 1), 0) >= CMP_LEN - 1
    vc = vc_ref[...]
    psum = jnp.zeros((tq, nc), F32)
    for h in range(NSA_HPG):
        head = g * NSA_HPG + h
        q_aug = jnp.concatenate([e_scr[head], q_ref[:, h * NSA_DH:(h + 1) * NSA_DH]], axis=1)
        s = _dot_nt(q_aug, kc_aug) + mask_add
        e = jnp.exp2(s - jnp.max(s, 1, keepdims=True))
        p = e * jnp.where(has_key, 1.0 / jnp.sum(e, 1, keepdims=True), 0.0)
        oc_ref[:, h * NSA_DH:(h + 1) * NSA_DH] = _dot(p.astype(BF), vc).astype(oc_ref.dtype)
        psum = psum + p

    ov = ov_ref[...]
    p0 = psum.astype(BF)
    r1 = psum - p0.astype(F32)
    p1 = r1.astype(BF)
    p2 = (r1 - p1.astype(F32)).astype(BF)
    imp = (_dot(p0, ov) + _dot(p1, ov) + _dot(p2, ov)).T

    tj = qs + lax.broadcasted_iota(I32, (n_slc, tq), 1)
    j = lax.broadcasted_iota(I32, (n_slc, tq), 0)
    cur = tj // SLC_LEN
    forced = (j == 0) | (j == cur) | (j == cur - 1)
    work = jnp.where(j * SLC_LEN > tj, NEG, jnp.where(forced, BIG, imp))
    sel = jnp.full((n_slc, tq), NEG, F32)
    for _ in range(n_top):
        mx, first = _first_max(work, j, n_slc)
        hit = j == first
        sel = jnp.where(hit & (mx > 0.5 * NEG), 0.0, sel)
        work = jnp.where(hit, -jnp.inf, work)
    sel_ref[...] = sel


def _nsa_cmp(tab, qn, kvc, overlap, batch, seq):
    n = qn.shape[0]
    tq = min(CMP_TQ, seq)
    nq = seq // tq
    nc = kvc.shape[2]
    n_slc = overlap.shape[1]
    sq = pl.Squeezed()
    gw = NSA_HPG * NSA_DH
    return pl.pallas_call(
        functools.partial(_nsa_cmp_kernel, n_top=min(SLC_TOPN, n_slc)), grid=(batch, NSA_GROUPS, nq),
        in_specs=[pl.BlockSpec(memory_space=pltpu.SMEM),
                  pl.BlockSpec((tq, gw), lambda b, g, i: (b * nq + i, g)),
                  pl.BlockSpec((sq, sq, nc, NSA_DH), lambda b, g, i: (b, g, 0, 0)),
                  pl.BlockSpec((sq, sq, nc, NSA_DH), lambda b, g, i: (b, NSA_GROUPS + g, 0, 0)),
                  pl.BlockSpec((nc, n_slc), lambda b, g, i: (0, 0))],
        out_specs=[pl.BlockSpec((tq, gw), lambda b, g, i: (b * nq + i, g)),
                   pl.BlockSpec((sq, n_slc, tq), lambda b, g, i: (g, 0, b * nq + i))],
        out_shape=[jax.ShapeDtypeStruct((n, NSA_HEADS * NSA_DH), BF),
                   jax.ShapeDtypeStruct((NSA_GROUPS, n_slc, n), F32)],
        scratch_shapes=[pltpu.VMEM((NSA_HEADS, tq, LANE), BF)],
        compiler_params=_cparams(("arbitrary", "arbitrary", "arbitrary")),
    )(tab, qn, kvc, kvc, overlap)


LOG2E = math.log2(math.e)
DEN_ROWS = 16
MASK_ROWS = 16


def _nsa_attn_kernel(tab_ref, qt_ref, ks_ref, kw_ref, vst_ref, vwt_ref, sel_ref, gate_ref, gate_t_ref, oc_ref, o_ref,
                     d_scr, q_scr, sa_scr, sb_scr, sc_scr, m_scr, acc_scr):
    b, i = pl.program_id(0), pl.program_id(1)
    tq = NSA_TQ
    lanes = NSA_HPG * tq
    qs = i * tq
    near_w = 2 * tq
    far_w = FAR_CHUNK
    win_far_w = WIN - tq

    @pl.when((b == 0) & (i == 0))
    def _():
        kk = lax.broadcasted_iota(I32, (near_w, tq), 0)
        q_io = lax.broadcasted_iota(I32, (near_w, tq), 1)
        dist = jnp.maximum(q_io + tq - kk, 0)
        for g in range(NSA_GROUPS):
            d_scr[g] = jnp.concatenate(
                [(_rel_bias(dist, tab_ref, g * NSA_HPG + h) - tab_ref[REL_BUCKETS - 1, g * NSA_HPG + h]) * LOG2E
                 for h in range(NSA_HPG)], axis=1)

    gw = NSA_GROUPS * NSA_DH
    qt = qt_ref[...]
    for g in range(NSA_GROUPS):
        qg = jnp.concatenate([qt[(g * NSA_HPG + h) * NSA_DH:(g * NSA_HPG + h + 1) * NSA_DH, :]
                              for h in range(NSA_HPG)], axis=1)
        parts = [jnp.zeros((NSA_DH, lanes), BF)] * NSA_GROUPS
        parts[g] = qg
        for role in range(2):
            q_scr[role, g, :gw] = jnp.concatenate(parts, axis=0)
            q_scr[role, g, gw:] = jnp.zeros((gw, lanes), BF)

    def update(slot, s, vt):
        m_prev = m_scr[slot]
        m_new = jnp.maximum(m_prev, jnp.max(s, 0, keepdims=True))
        a = jnp.exp2(m_prev - m_new)
        e = jnp.exp2(s - m_new)
        vt_den = jnp.concatenate([vt, jnp.ones((DEN_ROWS, vt.shape[1]), BF)], axis=0)
        acc_scr[slot] = a * acc_scr[slot] + _dot(vt_den, e.astype(BF))
        m_scr[slot] = m_new

    def all_heads(x):
        return jnp.tile(x, (1, NSA_HPG))

    def sel_rows(g, first_block, n_blk):
        return jnp.concatenate(
            [jnp.broadcast_to(sel_ref[g, pl.ds(jnp.maximum(first_block + r, 0), 1), :], (SLC_LEN, tq))
             for r in range(n_blk)], axis=0)

    m_scr[...] = jnp.full(m_scr.shape, NEG, F32)
    acc_scr[...] = jnp.zeros(acc_scr.shape, F32)

    kk = lax.broadcasted_iota(I32, (near_w, tq), 0)
    q_io = lax.broadcasted_iota(I32, (near_w, tq), 1)
    kpos_near = qs - tq + kk
    causal_add = jnp.where((kpos_near >= 0) & (kpos_near <= qs + q_io), 0.0, NEG)
    start_a = pl.multiple_of(qs, tq)
    start_b = pl.multiple_of(jnp.maximum(qs - tq, 0), tq)

    def near_logits(br, s_scr):
        k_ref = ks_ref if br == 0 else kw_ref
        k = jnp.concatenate([k_ref[pl.ds(start_b, tq), :], k_ref[pl.ds(start_a, tq), :]], axis=0)
        for g in range(NSA_GROUPS):
            add = causal_add + sel_rows(g, 2 * i - 2, near_w // SLC_LEN) if br == 0 else causal_add
            s_scr[g, :near_w] = _dot(k, q_scr[0, g, :gw]) + d_scr[g] + all_heads(add)

    def near_consume(br, s_scr):
        vt_ref = vst_ref if br == 0 else vwt_ref
        for g in range(NSA_GROUPS):
            gs = slice(g * NSA_DH, (g + 1) * NSA_DH)
            vt = jnp.concatenate([vt_ref[gs, pl.ds(start_b, tq)], vt_ref[gs, pl.ds(start_a, tq)]], axis=1)
            update(2 * g + br, s_scr[g, :near_w], vt)

    ws = pl.multiple_of(jnp.maximum(qs - WIN, 0), tq)

    def win_logits(s_scr):
        kpos_w = ws + lax.broadcasted_iota(I32, (win_far_w, tq), 0)
        t_w = qs + lax.broadcasted_iota(I32, (win_far_w, tq), 1)
        add_w = all_heads(jnp.where((kpos_w < qs - tq) & (kpos_w > t_w - WIN), 0.0, NEG))
        k_w = kw_ref[pl.ds(ws, win_far_w), :]
        for g in range(NSA_GROUPS):
            s_scr[g, :win_far_w] = _dot(k_w, q_scr[0, g, :gw]) + add_w

    def win_consume(s_scr):
        for g in range(NSA_GROUPS):
            update(2 * g + 1, s_scr[g, :win_far_w], vwt_ref[g * NSA_DH:(g + 1) * NSA_DH, pl.ds(ws, win_far_w)])

    n_far = (jnp.maximum(i - 1, 0) + (far_w // tq - 1)) // (far_w // tq)
    last_chunk = ks_ref.shape[0] // far_w - 1
    n_blk = far_w // SLC_LEN
    assert n_blk + 1 <= MASK_ROWS
    krow = lax.broadcasted_iota(I32, (far_w, gw), 0)
    kcol = lax.broadcasted_iota(I32, (far_w, gw), 1)
    in_block = jnp.where(kcol == krow // SLC_LEN, 1.0, 0.0)

    def far_logits(c, s_scr, role):
        c = jnp.minimum(c, last_chunk)
        base = pl.multiple_of(c * far_w, far_w)
        past = jnp.where(base + krow >= qs - tq, 1.0, 0.0)
        k_aug = jnp.concatenate([ks_ref[pl.ds(base, far_w), :],
                                 jnp.where(kcol == n_blk, past, in_block).astype(BF)], axis=1)
        for g in range(NSA_GROUPS):
            sel_blk = sel_ref[g, pl.ds(pl.multiple_of(c * n_blk, n_blk), n_blk), :]
            rows = jnp.concatenate([all_heads(sel_blk), jnp.full((1, lanes), NEG, F32),
                                    jnp.zeros((MASK_ROWS - n_blk - 1, lanes), F32)], axis=0)
            q_scr[role, g, gw:gw + MASK_ROWS] = rows.astype(BF)
            s_scr[g] = _dot(k_aug, q_scr[role, g])

    def far_consume(c, s_scr):
        base = pl.multiple_of(c * far_w, far_w)
        for g in range(NSA_GROUPS):
            update(2 * g, s_scr[g], vst_ref[g * NSA_DH:(g + 1) * NSA_DH, pl.ds(base, far_w)])

    near_logits(0, sc_scr)
    near_logits(1, sb_scr)
    near_consume(0, sc_scr)
    win_logits(sc_scr)
    near_consume(1, sb_scr)
    far_logits(0, sa_scr, 0)
    win_consume(sc_scr)

    def far_body(p, carry):
        far_logits(2 * p + 1, sb_scr, 1)
        far_consume(2 * p, sa_scr)
        far_logits(2 * p + 2, sa_scr, 0)
        far_consume(2 * p + 1, sb_scr)
        return carry

    lax.fori_loop(0, n_far // 2, far_body, 0)

    @pl.when(n_far % 2 == 1)
    def _():
        far_consume(n_far - 1, sa_scr)

    gates = jax.nn.sigmoid(gate_ref[...].astype(F32))
    gates_t = jax.nn.sigmoid(gate_t_ref[...].astype(F32))
    for g in range(NSA_GROUPS):
        o_s = acc_scr[2 * g, :NSA_DH] / acc_scr[2 * g, NSA_DH:NSA_DH + 1]
        o_w = acc_scr[2 * g + 1, :NSA_DH] / acc_scr[2 * g + 1, NSA_DH:NSA_DH + 1]
        for h in range(NSA_HPG):
            head = g * NSA_HPG + h
            hl = slice(head * NSA_DH, (head + 1) * NSA_DH)
            cl = slice(h * tq, (h + 1) * tq)
            sw_t = (gates_t[3 * head + 1:3 * head + 2, :] * o_s[:, cl]
                    + gates_t[3 * head + 2:3 * head + 3, :] * o_w[:, cl])
            o = gates[:, 3 * head:3 * head + 1] * oc_ref[:, hl].astype(F32) + sw_t.T
            o_ref[:, hl] = o.astype(o_ref.dtype)


def _nsa_attn(tab, qn_t, kvn, vs_t, vw_t, sel_t, gn, gn_t, oc, batch, seq):
    n = kvn.shape[0]
    tq = NSA_TQ
    nq = seq // tq
    n_slc = sel_t.shape[1]
    hw = NSA_HEADS * NSA_DH
    gw = NSA_GROUPS * NSA_DH
    lanes = NSA_HPG * tq
    row = lambda b, i: (b * nq + i, 0)
    col = lambda b, i: (0, b * nq + i)
    k_spec = lambda kind: pl.BlockSpec((seq, gw), lambda b, i: (b, kind))
    vt_spec = pl.BlockSpec((gw, seq), lambda b, i: (0, b))
    n_state = 2 * NSA_GROUPS
    return pl.pallas_call(
        _nsa_attn_kernel, grid=(batch, nq),
        in_specs=[pl.BlockSpec(memory_space=pltpu.SMEM), pl.BlockSpec((hw, tq), col),
                  k_spec(2), k_spec(4), vt_spec, vt_spec,
                  pl.BlockSpec((NSA_GROUPS, n_slc, tq), lambda b, i: (0, 0, b * nq + i)),
                  pl.BlockSpec((tq, LANE), row), pl.BlockSpec((LANE, tq), col), pl.BlockSpec((tq, hw), row)],
        out_specs=pl.BlockSpec((tq, hw), row),
        out_shape=jax.ShapeDtypeStruct((n, hw), BF),
        scratch_shapes=[pltpu.VMEM((NSA_GROUPS, 2 * tq, lanes), F32), pltpu.VMEM((2, NSA_GROUPS, 2 * gw, lanes), BF),
                        pltpu.VMEM((NSA_GROUPS, FAR_CHUNK, lanes), F32), pltpu.VMEM((NSA_GROUPS, FAR_CHUNK, lanes), F32),
                        pltpu.VMEM((NSA_GROUPS, FAR_CHUNK, lanes), F32),
                        pltpu.VMEM((n_state, 1, lanes), F32), pltpu.VMEM((n_state, NSA_DH + DEN_ROWS, lanes), F32)],
        compiler_params=_cparams(("arbitrary", "arbitrary")),
    )(tab, qn_t, kvn, kvn, vs_t, vw_t, sel_t, gn, gn_t, oc)


def _mem_attn_kernel(q_ref, mem_ref, w_ref, o_ref, kv_scr):
    @pl.when(pl.program_id(1) == 0)
    def _():
        kv_scr[...] = _dot(mem_ref[...].astype(BF), w_ref[...]).astype(BF)

    hw = MEM_HEADS * MEM_DH
    for h in range(MEM_HEADS):
        sl = slice(h * MEM_DH, (h + 1) * MEM_DH)
        s = _dot_nt(q_ref[:, sl], kv_scr[:, sl])
        e = jnp.exp(s - jnp.max(s, 1, keepdims=True))
        p = e / jnp.sum(e, 1, keepdims=True)
        o_ref[:, sl] = _dot(p.astype(BF), kv_scr[:, hw + h * MEM_DH:hw + (h + 1) * MEM_DH]).astype(o_ref.dtype)


def _mem_attn(qm, mem2, w_kv, batch, seq, tq):
    n = qm.shape[0]
    nq = seq // tq
    m = mem2.shape[0] // batch
    hw = MEM_HEADS * MEM_DH
    return pl.pallas_call(
        _mem_attn_kernel, grid=(batch, nq),
        in_specs=[pl.BlockSpec((tq, hw), lambda b, i: (b * nq + i, 0)),
                  pl.BlockSpec((m, D_MODEL), lambda b, i: (b, 0)),
                  pl.BlockSpec((D_MODEL, 2 * hw), lambda b, i: (0, 0))],
        out_specs=pl.BlockSpec((tq, hw), lambda b, i: (b * nq + i, 0)),
        out_shape=jax.ShapeDtypeStruct((n, hw), BF),
        scratch_shapes=[pltpu.VMEM((m, 2 * hw), BF)],
        compiler_params=_cparams(("arbitrary", "arbitrary")),
    )(qm, mem2, w_kv)


def _merge_kernel(on_ref, ol_ref, om_ref, gm_ref, h_ref, wb_ref, wo_ref, g_ref, b_ref, h1_ref):
    merged = None
    for c, o_ref in enumerate((on_ref, ol_ref, om_ref)):
        gate = jax.nn.sigmoid(gm_ref[:, c * D_MODEL:(c + 1) * D_MODEL].astype(F32))
        term = gate * _dot(o_ref[...], wb_ref[c])
        merged = term if merged is None else merged + term
    y = ALPHA * h_ref[...] + _dot(merged.astype(BF), wo_ref[...])
    h1_ref[...] = _layer_norm(y, g_ref[...], b_ref[...])


def _merge(o_nsa, o_mla, o_mem, gm, h, wb, wo, g, b, tm):
    n = h.shape[0]
    row = lambda i: (i, 0)
    const = lambda i: (0, 0)
    return pl.pallas_call(
        _merge_kernel, grid=(n // tm,),
        in_specs=[pl.BlockSpec((tm, BRANCH_W), row)] * 3 + [
            pl.BlockSpec((tm, N_BRANCH * D_MODEL), row), pl.BlockSpec((tm, D_MODEL), row),
            pl.BlockSpec(wb.shape, lambda i: (0, 0, 0)), pl.BlockSpec(wo.shape, const),
            pl.BlockSpec((1, D_MODEL), const), pl.BlockSpec((1, D_MODEL), const)],
        out_specs=pl.BlockSpec((tm, D_MODEL), row),
        out_shape=jax.ShapeDtypeStruct((n, D_MODEL), F32),
        compiler_params=_cparams(("parallel",)),
    )(o_nsa, o_mla, o_mem, gm, h, wb, wo, g, b)


def _first_max(vals, idx, limit):
    mx = jnp.max(vals, 0, keepdims=True)
    first = jnp.min(jnp.where(vals == mx, idx, limit), 0, keepdims=True)
    return mx, first


def _router_kernel(h_ref, whi_ref, wlo_ref, b_ref, tri_ref, idx_ref, w_ref, rank_ref, cnt_ref):
    @pl.when(pl.program_id(0) == 0)
    def _():
        cnt_ref[...] = jnp.zeros(cnt_ref.shape, F32)

    h = h_ref[...]
    hhi = h.astype(BF)
    hlo = (h - hhi.astype(F32)).astype(BF)
    whi = whi_ref[...]
    logits = _dot_nt(whi, hhi) + _dot_nt(whi, hlo) + _dot_nt(wlo_ref[...], hhi)
    s = jax.nn.sigmoid(logits)
    sb = s + b_ref[...]
    tm = s.shape[1]
    gsz = N_EXPERTS // N_EXPERT_GROUPS
    e_io = lax.broadcasted_iota(I32, (gsz, tm), 0)
    scores = []
    for g in range(N_EXPERT_GROUPS):
        vals = sb[g * gsz:(g + 1) * gsz]
        m1, first = _first_max(vals, e_io, gsz)
        m2 = jnp.max(jnp.where(e_io == first, -jnp.inf, vals), 0, keepdims=True)
        scores.append(m1 + m2)
    gs = jnp.concatenate(scores, axis=0)
    g_io = lax.broadcasted_iota(I32, (N_EXPERT_GROUPS, tm), 0)
    x_io = lax.broadcasted_iota(I32, (N_EXPERTS, tm), 0)
    allowed = jnp.zeros((N_EXPERTS, tm), jnp.bool_)
    for _ in range(TOPK_GROUPS):
        _, first = _first_max(gs, g_io, N_EXPERT_GROUPS)
        gs = jnp.where(g_io == first, -jnp.inf, gs)
        allowed = allowed | (x_io // gsz == first)
    work = jnp.where(allowed, sb, NEG)
    base = cnt_ref[:, :1]
    tri = tri_ref[...]
    idxs, ws, ranks = [], [], []
    for _ in range(TOP_K):
        _, first = _first_max(work, x_io, N_EXPERTS)
        hit = x_io == first
        idxs.append(first)
        ws.append(jnp.sum(jnp.where(hit, s, 0.0), 0, keepdims=True))
        work = jnp.where(hit, -jnp.inf, work)
        onehot = jnp.where(hit, 1.0, 0.0)
        before = _dot(onehot.astype(BF), tri)
        ranks.append(jnp.sum(jnp.where(hit, base + before, 0.0), 0, keepdims=True))
        base = base + jnp.sum(onehot, 1, keepdims=True)
    wsel = jnp.concatenate(ws, axis=0)
    idx_ref[...] = jnp.concatenate(idxs, axis=0)
    w_ref[...] = wsel / jnp.sum(wsel, 0, keepdims=True) * ROUTE_SCALE
    rank_ref[...] = jnp.concatenate(ranks, axis=0).astype(I32)
    cnt_ref[...] = jnp.broadcast_to(base, cnt_ref.shape)


def _router(h1, whi, wlo, rb, tm):
    n = h1.shape[0]
    tri = jnp.asarray(np.triu(np.ones((tm, tm), np.float32), 1), BF)
    slot = pl.BlockSpec((TOP_K, tm), lambda i: (0, i))
    const = lambda i: (0, 0)
    return pl.pallas_call(
        _router_kernel, grid=(n // tm,),
        in_specs=[pl.BlockSpec((tm, D_MODEL), lambda i: (i, 0)),
                  pl.BlockSpec((N_EXPERTS, D_MODEL), const), pl.BlockSpec((N_EXPERTS, D_MODEL), const),
                  pl.BlockSpec((N_EXPERTS, 1), const), pl.BlockSpec((tm, tm), const)],
        out_specs=[slot, slot, slot, pl.BlockSpec((N_EXPERTS, LANE), const)],
        out_shape=[jax.ShapeDtypeStruct((TOP_K, n), I32), jax.ShapeDtypeStruct((TOP_K, n), F32),
                   jax.ShapeDtypeStruct((TOP_K, n), I32), jax.ShapeDtypeStruct((N_EXPERTS, LANE), F32)],
        compiler_params=_cparams(("arbitrary",)),
    )(h1, whi, wlo, rb, tri)


def _pos_kernel(idx_ref, rank_ref, start_ref, pos_ref):
    tm = idx_ref.shape[1]
    x_io = lax.broadcasted_iota(I32, (N_EXPERTS, tm), 0)
    start = start_ref[...]
    rows = [jnp.sum(jnp.where(x_io == idx_ref[k:k + 1, :], start, 0.0), 0, keepdims=True) for k in range(TOP_K)]
    pos_ref[...] = jnp.concatenate(rows, axis=0).astype(I32) + rank_ref[...]


def _positions(eidx_t, rank_t, pad_start, tm):
    n = eidx_t.shape[1]
    slot = pl.BlockSpec((TOP_K, tm), lambda i: (0, i))
    return pl.pallas_call(
        _pos_kernel, grid=(n // tm,),
        in_specs=[slot, slot, pl.BlockSpec((N_EXPERTS, 1), lambda i: (0, 0))],
        out_specs=slot, out_shape=jax.ShapeDtypeStruct((TOP_K, n), I32),
        compiler_params=_cparams(("parallel",)),
    )(eidx_t, rank_t, pad_start)


ROW_UNROLL = 8
PACKED_W = D_MODEL // 2
U32 = jnp.uint32


def _fill_groups():
    p = EXPERT_BLOCK // 2
    while p >= 1:
        yield p
        p //= 2


def _pack_rows(x):
    bits = lambda v: lax.bitcast_convert_type(v.astype(BF).astype(F32), U32)
    return (bits(x[:, :PACKED_W]) >> 16) | (bits(x[:, PACKED_W:]) & U32(0xFFFF0000))


def _unpack_rows(u):
    return (lax.bitcast_convert_type(u << 16, F32), lax.bitcast_convert_type(u & U32(0xFFFF0000), F32))


def _permute_kernel(fill_start_ref, fill_n_ref, pos_ref, h_ref, xs_hbm, xp, zbuf, row_sem, fill_sem, *, fills_per_step):
    i = pl.program_id(0)
    tm = h_ref.shape[0]
    n_fills = fill_n_ref.shape[0]
    zbuf[...] = jnp.zeros(zbuf.shape, zbuf.dtype)
    xp[...] = _pack_rows(h_ref[...])

    def fill_copies(e):
        n = fill_n_ref[e]
        start = fill_start_ref[e]
        for p in _fill_groups():
            @pl.when((n & p) != 0)
            def _(p=p):
                off = start + (n & (p - 1))
                if p < SUBLANES:
                    for r in range(p):
                        pltpu.make_async_copy(zbuf.at[pl.ds(r, 1)], xs_hbm.at[pl.ds(off + r, 1)], fill_sem).start()
                else:
                    pltpu.make_async_copy(zbuf.at[pl.ds(0, p)], xs_hbm.at[pl.ds(pl.multiple_of(off, SUBLANES), p)],
                                          fill_sem).start()

    def fill_waits(e):
        n = fill_n_ref[e]
        for p in _fill_groups():
            @pl.when((n & p) != 0)
            def _(p=p):
                pltpu.make_async_copy(zbuf.at[pl.ds(0, p)], zbuf.at[pl.ds(0, p)], fill_sem).wait()

    for q in range(fills_per_step):
        e = i * fills_per_step + q

        @pl.when(e < n_fills)
        def _(e=e):
            fill_copies(e)

    for t in range(tm):
        for k in range(TOP_K):
            pltpu.make_async_copy(xp.at[pl.ds(t, 1)], xs_hbm.at[pl.ds(pos_ref[k, t], 1)], row_sem).start()

    for q in range(fills_per_step):
        e = i * fills_per_step + q

        @pl.when(e < n_fills)
        def _(e=e):
            fill_waits(e)

    pltpu.make_async_copy(xs_hbm.at[pl.ds(0, TOP_K * tm)], xs_hbm.at[pl.ds(0, TOP_K * tm)], row_sem).wait()


def _permute(fill_start, fill_n, pos, h1, n_rows, tm):
    n = h1.shape[0]
    nt = n // tm
    grid_spec = pltpu.PrefetchScalarGridSpec(
        num_scalar_prefetch=2, grid=(nt,),
        in_specs=[pl.BlockSpec((TOP_K, tm), lambda i, fs, fn: (0, i), memory_space=pltpu.SMEM),
                  pl.BlockSpec((tm, D_MODEL), lambda i, fs, fn: (i, 0))],
        out_specs=pl.BlockSpec(memory_space=pl.ANY),
        scratch_shapes=[pltpu.VMEM((tm, PACKED_W), U32), pltpu.VMEM((EXPERT_BLOCK // 2, PACKED_W), U32),
                        pltpu.SemaphoreType.DMA(()), pltpu.SemaphoreType.DMA(())])
    return pl.pallas_call(
        functools.partial(_permute_kernel, fills_per_step=-(-fill_n.shape[0] // nt)), grid_spec=grid_spec,
        out_shape=jax.ShapeDtypeStruct((n_rows, PACKED_W), U32),
        compiler_params=_cparams(("arbitrary",)),
    )(fill_start, fill_n, pos, h1)


def _experts_kernel(blk_e_ref, nused_ref, next_e_ref, x_ref, w1_hbm, w3_hbm, w2_hbm, y_ref,
                    w1f, w3f, w2f, w1b, w3b, w2b, ord_ref, sem):
    j = pl.program_id(0)

    def weight_copies(e, slot):
        return (pltpu.make_async_copy(w1_hbm.at[e], w1f.at[slot], sem.at[slot, 0]),
                pltpu.make_async_copy(w3_hbm.at[e], w3f.at[slot], sem.at[slot, 1]),
                pltpu.make_async_copy(w2_hbm.at[e], w2f.at[slot], sem.at[slot, 2]))

    @pl.when(j < nused_ref[0])
    def _():
        e = blk_e_ref[j]
        first_of_expert = (j == 0) | (e != blk_e_ref[jnp.maximum(j - 1, 0)])

        @pl.when(first_of_expert)
        def _():
            @pl.when(j == 0)
            def _():
                ord_ref[0] = 0
                for c in weight_copies(e, 0):
                    c.start()

            n = ord_ref[0]
            slot = n % 2
            for c in weight_copies(e, slot):
                c.wait()
            nxt = next_e_ref[e]

            @pl.when(nxt >= 0)
            def _():
                for c in weight_copies(nxt, 1 - slot):
                    c.start()

            w1b[...] = w1f[slot].astype(BF)
            w3b[...] = w3f[slot].astype(BF)
            w2b[...] = w2f[slot].astype(BF)
            ord_ref[0] = n + 1

        x = jnp.concatenate([half.astype(BF) for half in _unpack_rows(x_ref[...])], axis=1)
        a = _dot(x, w1b[...])
        y_ref[...] = _pack_rows(_dot((a * jax.nn.sigmoid(a) * _dot(x, w3b[...])).astype(BF), w2b[...]))

    @pl.when(j >= nused_ref[0])
    def _():
        y_ref[...] = jnp.zeros(y_ref.shape, y_ref.dtype)


def _experts(blk_e, nused, next_e, xs, w1, w3, w2):
    n_blocks = blk_e.shape[0]
    rb = EXPERT_BLOCK
    anyspec = pl.BlockSpec(memory_space=pl.ANY)
    grid_spec = pltpu.PrefetchScalarGridSpec(
        num_scalar_prefetch=3, grid=(n_blocks,),
        in_specs=[pl.BlockSpec((rb, PACKED_W), lambda j, be, nu, ne: (jnp.minimum(j, nu[0] - 1), 0)),
                  anyspec, anyspec, anyspec],
        out_specs=pl.BlockSpec((rb, PACKED_W), lambda j, be, nu, ne: (j, 0)),
        scratch_shapes=[pltpu.VMEM((2, D_MODEL, D_EXPERT), F32), pltpu.VMEM((2, D_MODEL, D_EXPERT), F32),
                        pltpu.VMEM((2, D_EXPERT, D_MODEL), F32),
                        pltpu.VMEM((D_MODEL, D_EXPERT), BF), pltpu.VMEM((D_MODEL, D_EXPERT), BF),
                        pltpu.VMEM((D_EXPERT, D_MODEL), BF),
                        pltpu.SMEM((1,), I32), pltpu.SemaphoreType.DMA((2, 3))])
    return pl.pallas_call(
        _experts_kernel, grid_spec=grid_spec,
        out_shape=jax.ShapeDtypeStruct((n_blocks * rb, PACKED_W), U32),
        compiler_params=_cparams(("arbitrary",)),
    )(blk_e, nused, next_e, xs, w1, w3, w2)


def _combine_kernel(pos_ref, pos_next_ref, ys_hbm, w_ref, h_ref, s1_ref, s3_ref, s2_ref, g_ref, b_ref, o_ref, ybuf, sem):
    i = pl.program_id(0)
    nt = pl.num_programs(0)
    tm = h_ref.shape[0]

    def start_gather(p_ref, slot):
        def body(t, c):
            for k in range(TOP_K):
                pltpu.make_async_copy(ys_hbm.at[pl.ds(p_ref[k, t], 1)], ybuf.at[slot, k, pl.ds(t, 1)], sem.at[slot]).start()
            return c

        lax.fori_loop(0, tm, body, 0, unroll=ROW_UNROLL)

    @pl.when(i == 0)
    def _():
        start_gather(pos_ref, 0)

    slot = i % 2
    pltpu.make_async_copy(ybuf.at[slot], ybuf.at[slot], sem.at[slot]).wait()

    for t in range(tm):
        for k in range(TOP_K):
            pltpu.make_async_copy(ys_hbm.at[pl.ds(pos_next_ref[k, t], 1)], ybuf.at[1 - slot, k, pl.ds(t, 1)],
                                  sem.at[1 - slot]).start()

    h = h_ref[...]
    hb = h.astype(BF)
    a = _dot(hb, s1_ref[...])
    y = ALPHA * h + _dot((a * jax.nn.sigmoid(a) * _dot(hb, s3_ref[...])).astype(BF), s2_ref[...])
    w = w_ref[...]
    routed = None
    for k in range(TOP_K):
        terms = [w[:, k:k + 1] * half for half in _unpack_rows(ybuf[slot, k])]
        routed = terms if routed is None else [r + t for r, t in zip(routed, terms)]
    y = y + jnp.concatenate(routed, axis=1)
    o_ref[...] = _layer_norm(y, g_ref[...], b_ref[...])

    @pl.when(i == nt - 1)
    def _():
        pltpu.make_async_copy(ybuf.at[1 - slot], ybuf.at[1 - slot], sem.at[1 - slot]).wait()


def _combine(pos, ys, w, h1, s1, s3, s2, g, b, tm):
    n = h1.shape[0]
    nt = n // tm
    row = lambda i: (i, 0)
    const = lambda i: (0, 0)
    return pl.pallas_call(
        _combine_kernel, grid=(nt,),
        in_specs=[pl.BlockSpec((TOP_K, tm), lambda i: (0, i), memory_space=pltpu.SMEM),
                  pl.BlockSpec((TOP_K, tm), lambda i: (0, jnp.minimum(i + 1, nt - 1)), memory_space=pltpu.SMEM),
                  pl.BlockSpec(memory_space=pl.ANY),
                  pl.BlockSpec((tm, TOP_K), row), pl.BlockSpec((tm, D_MODEL), row),
                  pl.BlockSpec(s1.shape, const), pl.BlockSpec(s3.shape, const), pl.BlockSpec(s2.shape, const),
                  pl.BlockSpec((1, D_MODEL), const), pl.BlockSpec((1, D_MODEL), const)],
        out_specs=pl.BlockSpec((tm, D_MODEL), row),
        out_shape=jax.ShapeDtypeStruct((n, D_MODEL), F32),
        scratch_shapes=[pltpu.VMEM((2, TOP_K, tm, PACKED_W), U32), pltpu.SemaphoreType.DMA((2,))],
        compiler_params=_cparams(("arbitrary",)),
    )(pos, pos, ys, w, h1, s1, s3, s2, g, b)


def _overlap_matrix(nc, n_slc):
    cs = np.arange(nc) * CMP_STRIDE
    ce = cs + CMP_LEN - 1
    js = np.arange(n_slc) * SLC_LEN
    je = js + SLC_LEN - 1
    ov = ((cs[:, None] <= je[None, :]) & (ce[:, None] >= js[None, :])).astype(np.float32)
    ov[nc - 1] = 0.0
    return ov


def _block_layout(counts, n):
    rb = EXPERT_BLOCK
    counts = counts.astype(I32)
    padded = (counts + rb - 1) // rb * rb
    pad_end = jnp.cumsum(padded)
    pad_start = pad_end - padded
    n_blocks = -(-TOP_K * n // rb) + N_EXPERTS
    blk_start = jnp.arange(n_blocks, dtype=I32) * rb
    blk_e = jnp.minimum(jnp.sum(pad_end[None, :] <= blk_start[:, None], axis=1), N_EXPERTS - 1).astype(I32)
    nused = (pad_end[-1] // rb).astype(I32).reshape(1)
    half = rb // 2
    tail_start = pad_end[-1] + half * jnp.arange(2 * N_EXPERTS, dtype=I32)
    tail_n = jnp.where(tail_start < n_blocks * rb, half, 0).astype(I32)
    fill_start = jnp.concatenate([pad_start + counts, jnp.minimum(tail_start, n_blocks * rb - half)])
    fill_n = jnp.concatenate([padded - counts, tail_n])
    ids = jnp.where(counts > 0, jnp.arange(N_EXPERTS, dtype=I32), N_EXPERTS)
    at_or_after = lax.cummin(ids[::-1])[::-1]
    nxt = jnp.concatenate([at_or_after[1:], jnp.full((1,), N_EXPERTS, I32)])
    next_e = jnp.where(nxt < N_EXPERTS, nxt, -1).astype(I32)
    return pad_start, fill_start, fill_n, blk_e, nused, next_e


def kernel(x, mem, ln0_g, ln0_b, rel_bias, w_in, cmp_pos_k, cmp_pos_v, cmp_k_w1, cmp_k_w2, cmp_v_w1, cmp_v_w2, mla_q_norm, mla_w_uq, mla_kv_norm, mla_w_ukv, mem_w_kv, w_branch, w_out, ln1_g, ln1_b, router_w, router_b, exp_w1, exp_w3, exp_w2, sh_w1, sh_w3, sh_w2, ln2_g, ln2_b):
    batch, seq, d = x.shape
    n = batch * seq
    l = 0
    row2 = lambda v: v.reshape(1, -1)
    tm = min(256, seq)

    pts = np.cumsum((0,) + IN_SPLITS)
    wcol = lambda k: w_in[l][:, pts[k]:pts[k + 1]]
    pad_cols = lambda w, lo, tot: jnp.pad(w, ((0, 0), (lo, tot - lo - w.shape[1])))
    ws = [wcol(0) * (NSA_DH ** -0.5 * LOG2E), wcol(1), pad_cols(wcol(2), 0, LANE), wcol(3), wcol(4),
          pad_cols(wcol(5), ROPE_LANE0, LANE), wcol(6), wcol(7)]
    ws = [w.astype(BF) for w in ws]
    scales = [1.0] * 6 + [MEM_DH ** -0.5, 1.0]
    gw = NSA_GROUPS * NSA_DH
    ws_t = [wcol(0) * (NSA_DH ** -0.5 * LOG2E), wcol(1)[:, 3 * gw:4 * gw], wcol(1)[:, 5 * gw:6 * gw],
            pad_cols(wcol(2), 0, LANE)]
    ws_t = [w.T.astype(BF) for w in ws_t]
    hd = MLA_NOPE + MLA_ROPE
    wuq = jnp.pad(mla_w_uq[l].reshape(MLA_Q_LORA, MLA_HEADS, hd), ((0, 0), (0, 0), (0, HEAD_PAD - hd)))
    wuq = wuq.reshape(MLA_Q_LORA, MLA_HEADS * HEAD_PAD).astype(BF)
    wukv = mla_w_ukv[l].reshape(MLA_KV_LORA, MLA_HEADS, MLA_NOPE + MLA_DV)
    wuk = jnp.pad(wukv[:, :, :MLA_NOPE], ((0, 0), (0, 0), (0, HEAD_PAD - MLA_NOPE)))
    wuk = wuk.reshape(MLA_KV_LORA, MLA_HEADS * HEAD_PAD).astype(BF)
    wuv = wukv[:, :, MLA_NOPE:].reshape(MLA_KV_LORA, MLA_HEADS * MLA_DV).astype(BF)

    h, qn, kvn, gn, cq, ckv, kr, qm, gm, qn_t, vs_t, vw_t, gn_t = _ln_inproj(
        x.reshape(n, d), row2(ln0_g), row2(ln0_b), ws, scales, ws_t, tm)

    q_mla, k_mla, v_mla = _mla_prep(cq, ckv, kr, row2(mla_q_norm[l]), row2(mla_kv_norm[l]), wuq, wuk, wuv, seq, tm)
    o_mla = _mla_attn(q_mla, k_mla, v_mla, batch, seq, min(MLA_TQ, seq))

    nc = seq // CMP_STRIDE
    n_slc = seq // SLC_LEN
    cc = kvn[:, :2 * gw].reshape(batch, nc, CMP_STRIDE, 2 * NSA_GROUPS, NSA_DH)
    cc = cc.transpose(0, 3, 1, 2, 4).reshape(batch, 2 * NSA_GROUPS, nc, CMP_STRIDE * NSA_DH)
    pos = jnp.stack([cmp_pos_k[l], cmp_pos_v[l]]).reshape(2, 1, CMP_LEN * NSA_DH)
    w1c = jnp.stack([cmp_k_w1[l], cmp_v_w1[l]]).astype(BF)
    w2c = jnp.stack([cmp_k_w2[l], cmp_v_w2[l]]).astype(BF)
    kvc = _nsa_compress(cc, pos, w1c, w2c)
    overlap = jnp.asarray(_overlap_matrix(nc, n_slc), BF)
    o_cmp, sel_t = _nsa_cmp(rel_bias, qn, kvc, overlap, batch, seq)
    o_nsa = _nsa_attn(rel_bias, qn_t, kvn, vs_t, vw_t, sel_t, gn, gn_t, o_cmp, batch, seq)

    o_mem = _mem_attn(qm, mem.reshape(-1, d), mem_w_kv[l].astype(BF), batch, seq, min(512, seq))

    h1 = _merge(o_nsa, o_mla, o_mem, gm, h, w_branch[l].astype(BF), w_out[l].astype(BF),
                row2(ln1_g[l]), row2(ln1_b[l]), tm)

    rw_t = router_w[l].T
    rw_hi = rw_t.astype(BF)
    rw_lo = (rw_t - rw_hi.astype(F32)).astype(BF)
    eidx_t, w_t, rank_t, counts = _router(h1, rw_hi, rw_lo, router_b[l].reshape(N_EXPERTS, 1), tm)
    pad_start, fill_start, fill_n, blk_e, nused, next_e = _block_layout(counts[:, 0], n)
    pos = _positions(eidx_t, rank_t, pad_start.astype(F32).reshape(N_EXPERTS, 1), tm)
    xs = _permute(fill_start, fill_n, pos, h1, blk_e.shape[0] * EXPERT_BLOCK, tm)
    ys = _experts(blk_e, nused, next_e, xs, exp_w1[l], exp_w3[l], exp_w2[l])
    out = _combine(pos, ys, w_t.T, h1, sh_w1[l].astype(BF), sh_w3[l].astype(BF), sh_w2[l].astype(BF),
                   row2(ln2_g[l]), row2(ln2_b[l]), min(128, seq))
    return out.reshape(batch, seq, d)
```

```python
import functools
import math

import numpy as np
import jax
import jax.numpy as jnp
from jax import lax
from jax.experimental import pallas as pl
from jax.experimental.pallas import tpu as pltpu

BF = jnp.bfloat16
F32 = jnp.float32
I32 = jnp.int32

D_MODEL = 1024
DEPTH = 1
NSA_HEADS = 8
NSA_GROUPS = 2
NSA_HPG = NSA_HEADS // NSA_GROUPS
NSA_DH = 64
CMP_LEN = 32
CMP_STRIDE = 16
CMP_HID = 256
SLC_LEN = 64
SLC_TOPN = 16
WIN = 512
MLA_HEADS = 8
MLA_NOPE = 64
MLA_ROPE = 32
MLA_DV = 64
MLA_Q_LORA = 768
MLA_KV_LORA = 256
ROPE_THETA = 10000.0
MEM_HEADS = 4
MEM_DH = 128
N_BRANCH = 3
BRANCH_W = NSA_HEADS * NSA_DH
REL_BUCKETS = 32
REL_MAX_DIST = 128
N_EXPERTS = 256
TOP_K = 8
N_EXPERT_GROUPS = 8
TOPK_GROUPS = 4
D_EXPERT = 256
ROUTE_SCALE = 2.5
EXPERT_BLOCK = 512
LN_EPS = 1e-5
RMS_EPS = 1e-6
NEG = -1e30
BIG = 1e30
ALPHA = (2 * DEPTH) ** 0.25
IN_SPLITS = (NSA_HEADS * NSA_DH, 6 * NSA_GROUPS * NSA_DH, 3 * NSA_HEADS, MLA_Q_LORA, MLA_KV_LORA,
             MLA_ROPE, MEM_HEADS * MEM_DH, N_BRANCH * D_MODEL)

LANE = 128
SUBLANES = 8
HEAD_PAD = 128
ROPE_LANE0 = MLA_NOPE
ROPE_HALF = MLA_ROPE // 2
VMEM_LIMIT = 56 * 1024 * 1024
NSA_TQ = 128
MLA_TQ = 512
MLA_HEADS_PER_STEP = 4
FAR_CHUNK = 512


def _cparams(sem):
    return pltpu.CompilerParams(dimension_semantics=sem, vmem_limit_bytes=VMEM_LIMIT)


def _bucket_starts():
    max_exact = REL_BUCKETS // 2
    d = np.arange(0, 4 * REL_MAX_DIST)
    nf = np.maximum(d, 1).astype(np.float32)
    large = max_exact + (np.log(nf / np.float32(max_exact)) / np.float32(math.log(REL_MAX_DIST / max_exact))
                         * np.float32(REL_BUCKETS - max_exact)).astype(np.int32)
    large = np.minimum(large, REL_BUCKETS - 1)
    bucket = np.where(d < max_exact, d, large)
    return [int(np.argmax(bucket >= b)) for b in range(REL_BUCKETS)]


BUCKET_START = _bucket_starts()
FAR_DIST = BUCKET_START[REL_BUCKETS - 1]


def _rel_bias(dist, tab_ref, head):
    val = jnp.full(dist.shape, tab_ref[REL_BUCKETS - 1, head], F32)
    for b in range(REL_BUCKETS - 2, -1, -1):
        val = jnp.where(dist < BUCKET_START[b + 1], tab_ref[b, head], val)
    return val


def _layer_norm(x, g, b):
    mu = jnp.mean(x, -1, keepdims=True)
    xc = x - mu
    var = jnp.mean(xc * xc, -1, keepdims=True)
    return xc * lax.rsqrt(var + LN_EPS) * g + b


def _dot(a, b):
    return jnp.dot(a, b, preferred_element_type=F32)


def _dot_nt(a, b):
    return lax.dot_general(a, b, (((1,), (1,)), ((), ())), preferred_element_type=F32)


def _ln_inproj_kernel(*refs, scales, n_rowmajor):
    x_ref, g_ref, b_ref = refs[:3]
    n_w = (len(refs) - 4) // 2
    w_refs = refs[3:3 + n_w]
    h_ref = refs[3 + n_w]
    o_refs = refs[4 + n_w:]
    h = _layer_norm(x_ref[...], g_ref[...], b_ref[...])
    h_ref[...] = h
    hb = h.astype(BF)
    for j, (w, o) in enumerate(zip(w_refs, o_refs)):
        if j < n_rowmajor:
            y = _dot(hb, w[...])
            o[...] = (y if scales[j] == 1.0 else y * scales[j]).astype(o.dtype)
        else:
            o[...] = _dot_nt(w[...], hb).astype(o.dtype)


def _ln_inproj(x2, g, b, ws, scales, ws_t, tm):
    n = x2.shape[0]
    row = lambda i: (i, 0)
    col = lambda i: (0, i)
    const = lambda i: (0, 0)
    in_specs = [pl.BlockSpec((tm, D_MODEL), row), pl.BlockSpec((1, D_MODEL), const), pl.BlockSpec((1, D_MODEL), const)]
    in_specs += [pl.BlockSpec(w.shape, const) for w in ws + ws_t]
    out_shape = [jax.ShapeDtypeStruct((n, D_MODEL), F32)]
    out_shape += [jax.ShapeDtypeStruct((n, w.shape[1]), BF) for w in ws]
    out_shape += [jax.ShapeDtypeStruct((w.shape[0], n), BF) for w in ws_t]
    out_specs = [pl.BlockSpec((tm, D_MODEL), row)] + [pl.BlockSpec((tm, w.shape[1]), row) for w in ws]
    out_specs += [pl.BlockSpec((w.shape[0], tm), col) for w in ws_t]
    return pl.pallas_call(
        functools.partial(_ln_inproj_kernel, scales=tuple(scales), n_rowmajor=len(ws)),
        grid=(n // tm,), in_specs=in_specs, out_specs=out_specs, out_shape=out_shape,
        compiler_params=_cparams(("parallel",)),
    )(x2, g, b, *ws, *ws_t)


def _rope_lanes(x, c, s1, s2):
    return x * c + pltpu.roll(x, LANE - ROPE_HALF, 1) * s1 + pltpu.roll(x, ROPE_HALF, 1) * s2


def _mla_prep_kernel(cq_ref, ckv_ref, kr_ref, qn_ref, kvn_ref, wuq, wuk, wuv,
                     cq_t, s1q_t, s2q_t, ck_t, s1k_t, s2k_t, q_out, k_out, v_out):
    cq = cq_ref[...].astype(F32)
    rq = cq * lax.rsqrt(jnp.mean(cq * cq, -1, keepdims=True) + RMS_EPS) * qn_ref[...]
    q = _dot(rq.astype(BF), wuq[...])
    ckv = ckv_ref[...].astype(F32)
    rkv = (ckv * lax.rsqrt(jnp.mean(ckv * ckv, -1, keepdims=True) + RMS_EPS) * kvn_ref[...]).astype(BF)
    kn = _dot(rkv, wuk[...])
    v = _dot(rkv, wuv[...])
    pair_w = 2 * MLA_DV
    ones = jnp.ones((v.shape[0], pair_w), v_out.dtype)
    for p in range(MLA_HEADS // 2):
        v_out[:, 2 * p * pair_w:(2 * p + 1) * pair_w] = v[:, p * pair_w:(p + 1) * pair_w].astype(v_out.dtype)
        v_out[:, (2 * p + 1) * pair_w:(2 * p + 2) * pair_w] = ones
    kr = _rope_lanes(kr_ref[...].astype(F32), ck_t[...], s1k_t[...], s2k_t[...])
    cq_c, s1q, s2q = cq_t[...], s1q_t[...], s2q_t[...]
    for h in range(MLA_HEADS):
        sl = slice(h * HEAD_PAD, (h + 1) * HEAD_PAD)
        q_out[:, sl] = _rope_lanes(q[:, sl], cq_c, s1q, s2q).astype(q_out.dtype)
        k_out[:, sl] = (kn[:, sl] + kr).astype(k_out.dtype)


def _rope_tables(seq, scale):
    freq = ROPE_THETA ** (-jnp.arange(ROPE_HALF, dtype=F32) / ROPE_HALF)
    ang = jnp.arange(seq, dtype=F32)[:, None] * freq[None, :]
    cos, sin = jnp.cos(ang) * scale, jnp.sin(ang) * scale
    z = lambda w: jnp.zeros((seq, w), F32)
    tail = HEAD_PAD - ROPE_LANE0 - MLA_ROPE
    c = jnp.concatenate([jnp.full((seq, ROPE_LANE0), scale, F32), cos, cos, z(tail)], 1)
    s1 = jnp.concatenate([z(ROPE_LANE0), -sin, z(ROPE_HALF + tail)], 1)
    s2 = jnp.concatenate([z(ROPE_LANE0 + ROPE_HALF), sin, z(tail)], 1)
    return c, s1, s2


def _mla_prep(cq, ckv, kr, q_norm, kv_norm, wuq, wuk, wuv, seq, tm):
    n = cq.shape[0]
    nt = seq // tm
    row = lambda i: (i, 0)
    const = lambda i: (0, 0)
    pos = lambda i: (i % nt, 0)
    tabs = _rope_tables(seq, (MLA_NOPE + MLA_ROPE) ** -0.5 * math.log2(math.e)) + _rope_tables(seq, 1.0)
    in_specs = [pl.BlockSpec((tm, MLA_Q_LORA), row), pl.BlockSpec((tm, MLA_KV_LORA), row), pl.BlockSpec((tm, LANE), row),
                pl.BlockSpec((1, MLA_Q_LORA), const), pl.BlockSpec((1, MLA_KV_LORA), const),
                pl.BlockSpec(wuq.shape, const), pl.BlockSpec(wuk.shape, const), pl.BlockSpec(wuv.shape, const)]
    in_specs += [pl.BlockSpec((tm, LANE), pos)] * 6
    hq = MLA_HEADS * HEAD_PAD
    hv = 2 * MLA_HEADS * MLA_DV
    return pl.pallas_call(
        _mla_prep_kernel, grid=(n // tm,), in_specs=in_specs,
        out_specs=[pl.BlockSpec((tm, hq), row), pl.BlockSpec((tm, hq), row), pl.BlockSpec((tm, hv), row)],
        out_shape=[jax.ShapeDtypeStruct((n, hq), BF), jax.ShapeDtypeStruct((n, hq), BF), jax.ShapeDtypeStruct((n, hv), BF)],
        compiler_params=_cparams(("parallel",)),
    )(cq, ckv, kr, q_norm, kv_norm, wuq, wuk, wuv, *tabs)


def _mla_attn_kernel(q_ref, k_ref, v_ref, o_ref, sa_scr, sb_scr, m_scr, acc_scr):
    i = pl.program_id(2)
    t = q_ref.shape[0]
    reps = t // LANE
    pair_w = 2 * MLA_DV
    nh = q_ref.shape[1] // HEAD_PAD
    m_scr[...] = jnp.full(m_scr.shape, NEG, F32)
    acc_scr[...] = jnp.zeros(acc_scr.shape, F32)

    def logits(tile, s_scr):
        kstart = pl.multiple_of(tile * t, t)
        for hh in range(nh):
            sl = slice(hh * HEAD_PAD, (hh + 1) * HEAD_PAD)
            s_scr[hh] = _dot_nt(q_ref[:, sl], k_ref[pl.ds(kstart, t), sl])

    def consume(tile, s_scr, diagonal):
        vstart = pl.multiple_of(tile * t, t)
        for hh in range(nh):
            v = v_ref[pl.ds(vstart, t), (hh // 2) * 2 * pair_w:(hh // 2 + 1) * 2 * pair_w]
            s = s_scr[hh]
            if diagonal:
                row = lax.broadcasted_iota(I32, (t, t), 0)
                col = lax.broadcasted_iota(I32, (t, t), 1)
                s = jnp.where(col <= row, s, NEG)
            m_prev = m_scr[hh]
            m_new = jnp.maximum(m_prev, jnp.max(s, 1, keepdims=True))
            a = jnp.exp2(m_prev - m_new)
            e = jnp.exp2(s - jnp.tile(m_new, (1, reps)))
            acc_scr[hh] = jnp.tile(a, (1, 2)) * acc_scr[hh] + _dot(e.astype(BF), v)
            m_scr[hh] = m_new

    logits(0, sa_scr)

    def body(p, carry):
        logits(2 * p + 1, sb_scr)
        consume(2 * p, sa_scr, False)
        logits(2 * p + 2, sa_scr)
        consume(2 * p + 1, sb_scr, False)
        return carry

    lax.fori_loop(0, i // 2, body, 0)

    @pl.when(i % 2 == 0)
    def _():
        consume(i, sa_scr, True)

    @pl.when(i % 2 == 1)
    def _():
        logits(i, sb_scr)
        consume(i - 1, sa_scr, False)
        consume(i, sb_scr, True)

    lane = lax.broadcasted_iota(I32, (t, pair_w), 1)
    for p in range(nh // 2):
        o = jnp.where(lane < MLA_DV, acc_scr[2 * p, :, :pair_w] / acc_scr[2 * p, :, pair_w:],
                      acc_scr[2 * p + 1, :, :pair_w] / acc_scr[2 * p + 1, :, pair_w:])
        o_ref[:, p * pair_w:(p + 1) * pair_w] = o.astype(o_ref.dtype)


def _mla_attn(q, k, v, batch, seq, t):
    n = q.shape[0]
    nt = seq // t
    nh = MLA_HEADS_PER_STEP
    qmap = lambda b, hq, i: (b * nt + i, hq)
    kmap = lambda b, hq, i: (b, hq)
    return pl.pallas_call(
        _mla_attn_kernel, grid=(batch, MLA_HEADS // nh, nt),
        in_specs=[pl.BlockSpec((t, nh * HEAD_PAD), qmap), pl.BlockSpec((seq, nh * HEAD_PAD), kmap),
                  pl.BlockSpec((seq, 2 * nh * MLA_DV), kmap)],
        out_specs=pl.BlockSpec((t, nh * MLA_DV), qmap),
        out_shape=jax.ShapeDtypeStruct((n, MLA_HEADS * MLA_DV), BF),
        scratch_shapes=[pltpu.VMEM((nh, t, t), F32), pltpu.VMEM((nh, t, t), F32),
                        pltpu.VMEM((nh, t, LANE), F32), pltpu.VMEM((nh, t, 4 * MLA_DV), F32)],
        compiler_params=_cparams(("parallel", "parallel", "arbitrary")),
    )(q, k, v)


def _nsa_compress_kernel(c_ref, pos_ref, w1_ref, w2_ref, o_ref):
    nc = c_ref.shape[0]
    half = CMP_STRIDE * NSA_DH
    c = c_ref[...]
    top = _dot(c, w1_ref[:half, :])
    bot = _dot(c, w1_ref[half:, :])
    posb = _dot(jnp.broadcast_to(pos_ref[...], (8, 2 * half)).astype(BF), w1_ref[...])[:1]
    hid = top + pltpu.roll(bot, nc - 1, 0) + posb
    o_ref[...] = _dot(jax.nn.gelu(hid).astype(BF), w2_ref[...]).astype(o_ref.dtype)


def _nsa_compress(cc, pos, w1, w2):
    b, _, nc, half = cc.shape
    sq = pl.Squeezed()
    return pl.pallas_call(
        _nsa_compress_kernel, grid=(b, 2 * NSA_GROUPS),
        in_specs=[pl.BlockSpec((sq, sq, nc, half), lambda i, c: (i, c, 0, 0)),
                  pl.BlockSpec((sq, 1, 2 * half), lambda i, c: (c // NSA_GROUPS, 0, 0)),
                  pl.BlockSpec((sq, 2 * half, CMP_HID), lambda i, c: (c // NSA_GROUPS, 0, 0)),
                  pl.BlockSpec((sq, CMP_HID, NSA_DH), lambda i, c: (c // NSA_GROUPS, 0, 0))],
        out_specs=pl.BlockSpec((sq, sq, nc, NSA_DH), lambda i, c: (i, c, 0, 0)),
        out_shape=jax.ShapeDtypeStruct((b, 2 * NSA_GROUPS, nc, NSA_DH), BF),
        compiler_params=_cparams(("parallel", "parallel")),
    )(cc, pos, w1, w2)


CMP_TQ = 512
CMP_BIAS_COLS = LANE // 2
CMP_BIAS_BACK = -(-(FAR_DIST + CMP_LEN - 1) // CMP_STRIDE)
assert (CMP_TQ - CMP_LEN) // CMP_STRIDE + CMP_BIAS_BACK < CMP_BIAS_COLS


def _nsa_cmp_kernel(tab_ref, q_ref, kc_ref, vc_ref, ov_ref, oc_ref, sel_ref, e_scr, *, n_top):
    b, g, i = pl.program_id(0), pl.program_id(1), pl.program_id(2)
    tq = q_ref.shape[0]
    nc = kc_ref.shape[0]
    n_slc = ov_ref.shape[1]
    qs = i * tq

    @pl.when((b == 0) & (g == 0) & (i == 0))
    def _():
        q_io = lax.broadcasted_iota(I32, (tq, LANE), 0)
        lane = lax.broadcasted_iota(I32, (tq, LANE), 1)
        jj = lane % CMP_BIAS_COLS
        dist = q_io - CMP_STRIDE * (jj - CMP_BIAS_BACK) - (CMP_LEN - 1)
        live = (dist >= 0) & (lane < 2 * CMP_BIAS_COLS)
        for h in range(NSA_HEADS):
            e = jnp.where(live, (_rel_bias(dist, tab_ref, h) - tab_ref[REL_BUCKETS - 1, h]) * LOG2E, 0.0)
            hi = e.astype(BF)
            lo = (e - hi.astype(F32)).astype(BF)
            e_scr[h] = jnp.where(lane < CMP_BIAS_COLS, hi, lo)

    n0 = qs // CMP_STRIDE
    nrow = lax.broadcasted_iota(I32, (nc, LANE), 0)
    jcol = lax.broadcasted_iota(I32, (nc, LANE), 1)
    near_blk = jnp.where(nrow == n0 + (jcol % CMP_BIAS_COLS) - CMP_BIAS_BACK, 1.0, 0.0).astype(BF)
    kc_aug = jnp.concatenate([near_blk, kc_ref[...]], axis=1)

    t = qs + lax.broadcasted_iota(I32, (tq, nc), 0)
    n_io = lax.broadcasted_iota(I32, (tq, nc), 1)
    mask_add = jnp.where((t >= n_io * CMP_STRIDE + (CMP_LEN - 1)) & (n_io < nc - 1), 0.0, NEG)
    has_key = qs + lax.broadcasted_iota(I32, (tq, 1), 0) >= CMP_LEN - 1
    vc = vc_ref[...]
    psum = jnp.zeros((tq, nc), F32)
    for h in range(NSA_HPG):
        head = g * NSA_HPG + h
        q_aug = jnp.concatenate([e_scr[head], q_ref[:, h * NSA_DH:(h + 1) * NSA_DH]], axis=1)
        s = _dot_nt(q_aug, kc_aug) + mask_add
        e = jnp.exp2(s - jnp.max(s, 1, keepdims=True))
        p = e * jnp.where(has_key, 1.0 / jnp.sum(e, 1, keepdims=True), 0.0)
        oc_ref[:, h * NSA_DH:(h + 1) * NSA_DH] = _dot(p.astype(BF), vc).astype(oc_ref.dtype)
        psum = psum + p

    ov = ov_ref[...]
    p0 = psum.astype(BF)
    r1 = psum - p0.astype(F32)
    p1 = r1.astype(BF)
    p2 = (r1 - p1.astype(F32)).astype(BF)
    imp = (_dot(p0, ov) + _dot(p1, ov) + _dot(p2, ov)).T

    tj = qs + lax.broadcasted_iota(I32, (n_slc, tq), 1)
    j = lax.broadcasted_iota(I32, (n_slc, tq), 0)
    cur = tj // SLC_LEN
    forced = (j == 0) | (j == cur) | (j == cur - 1)
    work = jnp.where(j * SLC_LEN > tj, NEG, jnp.where(forced, BIG, imp))
    sel = jnp.full((n_slc, tq), NEG, F32)
    for _ in range(n_top):
        mx, first = _first_max(work, j, n_slc)
        hit = j == first
        sel = jnp.where(hit & (mx > 0.5 * NEG), 0.0, sel)
        work = jnp.where(hit, -jnp.inf, work)
    sel_ref[...] = sel


def _nsa_cmp(tab, qn, kvc, overlap, batch, seq):
    n = qn.shape[0]
    tq = min(CMP_TQ, seq)
    nq = seq // tq
    nc = kvc.shape[2]
    n_slc = overlap.shape[1]
    sq = pl.Squeezed()
    gw = NSA_HPG * NSA_DH
    return pl.pallas_call(
        functools.partial(_nsa_cmp_kernel, n_top=min(SLC_TOPN, n_slc)), grid=(batch, NSA_GROUPS, nq),
        in_specs=[pl.BlockSpec(memory_space=pltpu.SMEM),
                  pl.BlockSpec((tq, gw), lambda b, g, i: (b * nq + i, g)),
                  pl.BlockSpec((sq, sq, nc, NSA_DH), lambda b, g, i: (b, g, 0, 0)),
                  pl.BlockSpec((sq, sq, nc, NSA_DH), lambda b, g, i: (b, NSA_GROUPS + g, 0, 0)),
                  pl.BlockSpec((nc, n_slc), lambda b, g, i: (0, 0))],
        out_specs=[pl.BlockSpec((tq, gw), lambda b, g, i: (b * nq + i, g)),
                   pl.BlockSpec((sq, n_slc, tq), lambda b, g, i: (g, 0, b * nq + i))],
        out_shape=[jax.ShapeDtypeStruct((n, NSA_HEADS * NSA_DH), BF),
                   jax.ShapeDtypeStruct((NSA_GROUPS, n_slc, n), F32)],
        scratch_shapes=[pltpu.VMEM((NSA_HEADS, tq, LANE), BF)],
        compiler_params=_cparams(("arbitrary", "arbitrary", "arbitrary")),
    )(tab, qn, kvc, kvc, overlap)


LOG2E = math.log2(math.e)
DEN_ROWS = 16
MASK_ROWS = 16


def _nsa_attn_kernel(tab_ref, qt_ref, ks_ref, kw_ref, vst_ref, vwt_ref, sel_ref, gate_ref, gate_t_ref, oc_ref, o_ref,
                     d_scr, q_scr, sa_scr, sb_scr, sc_scr, m_scr, acc_scr):
    b, i = pl.program_id(0), pl.program_id(1)
    tq = NSA_TQ
    lanes = NSA_HPG * tq
    qs = i * tq
    near_w = 2 * tq
    far_w = FAR_CHUNK
    win_far_w = WIN - tq

    @pl.when((b == 0) & (i == 0))
    def _():
        kk = lax.broadcasted_iota(I32, (near_w, tq), 0)
        q_io = lax.broadcasted_iota(I32, (near_w, tq), 1)
        dist = jnp.maximum(q_io + tq - kk, 0)
        for g in range(NSA_GROUPS):
            d_scr[g] = jnp.concatenate(
                [(_rel_bias(dist, tab_ref, g * NSA_HPG + h) - tab_ref[REL_BUCKETS - 1, g * NSA_HPG + h]) * LOG2E
                 for h in range(NSA_HPG)], axis=1)

    gw = NSA_GROUPS * NSA_DH
    qt = qt_ref[...]
    for g in range(NSA_GROUPS):
        qg = jnp.concatenate([qt[(g * NSA_HPG + h) * NSA_DH:(g * NSA_HPG + h + 1) * NSA_DH, :]
                              for h in range(NSA_HPG)], axis=1)
        parts = [jnp.zeros((NSA_DH, lanes), BF)] * NSA_GROUPS
        parts[g] = qg
        for role in range(2):
            q_scr[role, g, :gw] = jnp.concatenate(parts, axis=0)
            q_scr[role, g, gw:] = jnp.zeros((gw, lanes), BF)

    def update(slot, s, vt):
        m_prev = m_scr[slot]
        m_new = jnp.maximum(m_prev, jnp.max(s, 0, keepdims=True))
        a = jnp.exp2(m_prev - m_new)
        e = jnp.exp2(s - m_new)
        vt_den = jnp.concatenate([vt, jnp.ones((DEN_ROWS, vt.shape[1]), BF)], axis=0)
        acc_scr[slot] = a * acc_scr[slot] + _dot(vt_den, e.astype(BF))
        m_scr[slot] = m_new

    def all_heads(x):
        return jnp.tile(x, (1, NSA_HPG))

    def sel_rows(g, first_block, n_blk):
        return jnp.concatenate(
            [jnp.broadcast_to(sel_ref[g, pl.ds(jnp.maximum(first_block + r, 0), 1), :], (SLC_LEN, tq))
             for r in range(n_blk)], axis=0)

    m_scr[...] = jnp.full(m_scr.shape, NEG, F32)
    acc_scr[...] = jnp.zeros(acc_scr.shape, F32)

    kk = lax.broadcasted_iota(I32, (near_w, tq), 0)
    q_io = lax.broadcasted_iota(I32, (near_w, tq), 1)
    kpos_near = qs - tq + kk
    causal_add = jnp.where((kpos_near >= 0) & (kpos_near <= qs + q_io), 0.0, NEG)
    start_a = pl.multiple_of(qs, tq)
    start_b = pl.multiple_of(jnp.maximum(qs - tq, 0), tq)

    def near_logits(br, s_scr):
        k_ref = ks_ref if br == 0 else kw_ref
        k = jnp.concatenate([k_ref[pl.ds(start_b, tq), :], k_ref[pl.ds(start_a, tq), :]], axis=0)
        for g in range(NSA_GROUPS):
            add = causal_add + sel_rows(g, 2 * i - 2, near_w // SLC_LEN) if br == 0 else causal_add
            s_scr[g, :near_w] = _dot(k, q_scr[0, g, :gw]) + d_scr[g] + all_heads(add)

    def near_consume(br, s_scr):
        vt_ref = vst_ref if br == 0 else vwt_ref
        for g in range(NSA_GROUPS):
            gs = slice(g * NSA_DH, (g + 1) * NSA_DH)
            vt = jnp.concatenate([vt_ref[gs, pl.ds(start_b, tq)], vt_ref[gs, pl.ds(start_a, tq)]], axis=1)
            update(2 * g + br, s_scr[g, :near_w], vt)

    ws = pl.multiple_of(jnp.maximum(qs - WIN, 0), tq)

    def win_logits(s_scr):
        kpos_w = ws + lax.broadcasted_iota(I32, (win_far_w, tq), 0)
        t_w = qs + lax.broadcasted_iota(I32, (win_far_w, tq), 1)
        add_w = all_heads(jnp.where((kpos_w < qs - tq) & (kpos_w > t_w - WIN), 0.0, NEG))
        k_w = kw_ref[pl.ds(ws, win_far_w), :]
        for g in range(NSA_GROUPS):
            s_scr[g, :win_far_w] = _dot(k_w, q_scr[0, g, :gw]) + add_w

    def win_consume(s_scr):
        for g in range(NSA_GROUPS):
            update(2 * g + 1, s_scr[g, :win_far_w], vwt_ref[g * NSA_DH:(g + 1) * NSA_DH, pl.ds(ws, win_far_w)])

    n_far = (jnp.maximum(i - 1, 0) + (far_w // tq - 1)) // (far_w // tq)
    last_chunk = ks_ref.shape[0] // far_w - 1
    n_blk = far_w // SLC_LEN
    assert n_blk + 1 <= MASK_ROWS
    krow = lax.broadcasted_iota(I32, (far_w, gw), 0)
    kcol = lax.broadcasted_iota(I32, (far_w, gw), 1)
    in_block = jnp.where(kcol == krow // SLC_LEN, 1.0, 0.0)

    def far_logits(c, s_scr, role):
        c = jnp.minimum(c, last_chunk)
        base = pl.multiple_of(c * far_w, far_w)
        past = jnp.where(base + krow >= qs - tq, 1.0, 0.0)
        k_aug = jnp.concatenate([ks_ref[pl.ds(base, far_w), :],
                                 jnp.where(kcol == n_blk, past, in_block).astype(BF)], axis=1)
        for g in range(NSA_GROUPS):
            sel_blk = sel_ref[g, pl.ds(pl.multiple_of(c * n_blk, n_blk), n_blk), :]
            rows = jnp.concatenate([all_heads(sel_blk), jnp.full((1, lanes), NEG, F32),
                                    jnp.zeros((MASK_ROWS - n_blk - 1, lanes), F32)], axis=0)
            q_scr[role, g, gw:gw + MASK_ROWS] = rows.astype(BF)
            s_scr[g] = _dot(k_aug, q_scr[role, g])

    def far_consume(c, s_scr):
        base = pl.multiple_of(c * far_w, far_w)
        for g in range(NSA_GROUPS):
            update(2 * g, s_scr[g], vst_ref[g * NSA_DH:(g + 1) * NSA_DH, pl.ds(base, far_w)])

    near_logits(0, sc_scr)
    near_logits(1, sb_scr)
    near_consume(0, sc_scr)
    win_logits(sc_scr)
    near_consume(1, sb_scr)
    far_logits(0, sa_scr, 0)
    win_consume(sc_scr)

    def far_body(p, carry):
        far_logits(2 * p + 1, sb_scr, 1)
        far_consume(2 * p, sa_scr)
        far_logits(2 * p + 2, sa_scr, 0)
        far_consume(2 * p + 1, sb_scr)
        return carry

    lax.fori_loop(0, n_far // 2, far_body, 0)

    @pl.when(n_far % 2 == 1)
    def _():
        far_consume(n_far - 1, sa_scr)

    gates = jax.nn.sigmoid(gate_ref[...].astype(F32))
    gates_t = jax.nn.sigmoid(gate_t_ref[...].astype(F32))
    for g in range(NSA_GROUPS):
        o_s = acc_scr[2 * g, :NSA_DH] / acc_scr[2 * g, NSA_DH:NSA_DH + 1]
        o_w = acc_scr[2 * g + 1, :NSA_DH] / acc_scr[2 * g + 1, NSA_DH:NSA_DH + 1]
        for h in range(NSA_HPG):
            head = g * NSA_HPG + h
            hl = slice(head * NSA_DH, (head + 1) * NSA_DH)
            cl = slice(h * tq, (h + 1) * tq)
            sw_t = (gates_t[3 * head + 1:3 * head + 2, :] * o_s[:, cl]
                    + gates_t[3 * head + 2:3 * head + 3, :] * o_w[:, cl])
            o = gates[:, 3 * head:3 * head + 1] * oc_ref[:, hl].astype(F32) + sw_t.T
            o_ref[:, hl] = o.astype(o_ref.dtype)


def _nsa_attn(tab, qn_t, kvn, vs_t, vw_t, sel_t, gn, gn_t, oc, batch, seq):
    n = kvn.shape[0]
    tq = NSA_TQ
    nq = seq // tq
    n_slc = sel_t.shape[1]
    hw = NSA_HEADS * NSA_DH
    gw = NSA_GROUPS * NSA_DH
    lanes = NSA_HPG * tq
    row = lambda b, i: (b * nq + i, 0)
    col = lambda b, i: (0, b * nq + i)
    k_spec = lambda kind: pl.BlockSpec((seq, gw), lambda b, i: (b, kind))
    vt_spec = pl.BlockSpec((gw, seq), lambda b, i: (0, b))
    n_state = 2 * NSA_GROUPS
    return pl.pallas_call(
        _nsa_attn_kernel, grid=(batch, nq),
        in_specs=[pl.BlockSpec(memory_space=pltpu.SMEM), pl.BlockSpec((hw, tq), col),
                  k_spec(2), k_spec(4), vt_spec, vt_spec,
                  pl.BlockSpec((NSA_GROUPS, n_slc, tq), lambda b, i: (0, 0, b * nq + i)),
                  pl.BlockSpec((tq, LANE), row), pl.BlockSpec((LANE, tq), col), pl.BlockSpec((tq, hw), row)],
        out_specs=pl.BlockSpec((tq, hw), row),
        out_shape=jax.ShapeDtypeStruct((n, hw), BF),
        scratch_shapes=[pltpu.VMEM((NSA_GROUPS, 2 * tq, lanes), F32), pltpu.VMEM((2, NSA_GROUPS, 2 * gw, lanes), BF),
                        pltpu.VMEM((NSA_GROUPS, FAR_CHUNK, lanes), F32), pltpu.VMEM((NSA_GROUPS, FAR_CHUNK, lanes), F32),
                        pltpu.VMEM((NSA_GROUPS, FAR_CHUNK, lanes), F32),
                        pltpu.VMEM((n_state, 1, lanes), F32), pltpu.VMEM((n_state, NSA_DH + DEN_ROWS, lanes), F32)],
        compiler_params=_cparams(("arbitrary", "arbitrary")),
    )(tab, qn_t, kvn, kvn, vs_t, vw_t, sel_t, gn, gn_t, oc)


def _mem_attn_kernel(q_ref, mem_ref, w_ref, o_ref, kv_scr):
    @pl.when(pl.program_id(1) == 0)
    def _():
        kv_scr[...] = _dot(mem_ref[...].astype(BF), w_ref[...]).astype(BF)

    hw = MEM_HEADS * MEM_DH
    for h in range(MEM_HEADS):
        sl = slice(h * MEM_DH, (h + 1) * MEM_DH)
        s = _dot_nt(q_ref[:, sl], kv_scr[:, sl])
        e = jnp.exp(s - jnp.max(s, 1, keepdims=True))
        p = e / jnp.sum(e, 1, keepdims=True)
        o_ref[:, sl] = _dot(p.astype(BF), kv_scr[:, hw + h * MEM_DH:hw + (h + 1) * MEM_DH]).astype(o_ref.dtype)


def _mem_attn(qm, mem2, w_kv, batch, seq, tq):
    n = qm.shape[0]
    nq = seq // tq
    m = mem2.shape[0] // batch
    hw = MEM_HEADS * MEM_DH
    return pl.pallas_call(
        _mem_attn_kernel, grid=(batch, nq),
        in_specs=[pl.BlockSpec((tq, hw), lambda b, i: (b * nq + i, 0)),
                  pl.BlockSpec((m, D_MODEL), lambda b, i: (b, 0)),
                  pl.BlockSpec((D_MODEL, 2 * hw), lambda b, i: (0, 0))],
        out_specs=pl.BlockSpec((tq, hw), lambda b, i: (b * nq + i, 0)),
        out_shape=jax.ShapeDtypeStruct((n, hw), BF),
        scratch_shapes=[pltpu.VMEM((m, 2 * hw), BF)],
        compiler_params=_cparams(("arbitrary", "arbitrary")),
    )(qm, mem2, w_kv)


def _merge_kernel(on_ref, ol_ref, om_ref, gm_ref, h_ref, wb_ref, wo_ref, g_ref, b_ref, h1_ref):
    merged = None
    for c, o_ref in enumerate((on_ref, ol_ref, om_ref)):
        gate = jax.nn.sigmoid(gm_ref[:, c * D_MODEL:(c + 1) * D_MODEL].astype(F32))
        term = gate * _dot(o_ref[...], wb_ref[c])
        merged = term if merged is None else merged + term
    y = ALPHA * h_ref[...] + _dot(merged.astype(BF), wo_ref[...])
    h1_ref[...] = _layer_norm(y, g_ref[...], b_ref[...])


def _merge(o_nsa, o_mla, o_mem, gm, h, wb, wo, g, b, tm):
    n = h.shape[0]
    row = lambda i: (i, 0)
    const = lambda i: (0, 0)
    return pl.pallas_call(
        _merge_kernel, grid=(n // tm,),
        in_specs=[pl.BlockSpec((tm, BRANCH_W), row)] * 3 + [
            pl.BlockSpec((tm, N_BRANCH * D_MODEL), row), pl.BlockSpec((tm, D_MODEL), row),
            pl.BlockSpec(wb.shape, lambda i: (0, 0, 0)), pl.BlockSpec(wo.shape, const),
            pl.BlockSpec((1, D_MODEL), const), pl.BlockSpec((1, D_MODEL), const)],
        out_specs=pl.BlockSpec((tm, D_MODEL), row),
        out_shape=jax.ShapeDtypeStruct((n, D_MODEL), F32),
        compiler_params=_cparams(("parallel",)),
    )(o_nsa, o_mla, o_mem, gm, h, wb, wo, g, b)


def _first_max(vals, idx, limit):
    mx = jnp.max(vals, 0, keepdims=True)
    first = jnp.min(jnp.where(vals == mx, idx, limit), 0, keepdims=True)
    return mx, first


def _router_kernel(h_ref, whi_ref, wlo_ref, b_ref, tri_ref, idx_ref, w_ref, rank_ref, cnt_ref):
    @pl.when(pl.program_id(0) == 0)
    def _():
        cnt_ref[...] = jnp.zeros(cnt_ref.shape, F32)

    h = h_ref[...]
    hhi = h.astype(BF)
    hlo = (h - hhi.astype(F32)).astype(BF)
    whi = whi_ref[...]
    logits = _dot_nt(whi, hhi) + _dot_nt(whi, hlo) + _dot_nt(wlo_ref[...], hhi)
    s = jax.nn.sigmoid(logits)
    sb = s + b_ref[...]
    tm = s.shape[1]
    gsz = N_EXPERTS // N_EXPERT_GROUPS
    e_io = lax.broadcasted_iota(I32, (gsz, tm), 0)
    scores = []
    for g in range(N_EXPERT_GROUPS):
        vals = sb[g * gsz:(g + 1) * gsz]
        m1, first = _first_max(vals, e_io, gsz)
        m2 = jnp.max(jnp.where(e_io == first, -jnp.inf, vals), 0, keepdims=True)
        scores.append(m1 + m2)
    gs = jnp.concatenate(scores, axis=0)
    g_io = lax.broadcasted_iota(I32, (N_EXPERT_GROUPS, tm), 0)
    x_io = lax.broadcasted_iota(I32, (N_EXPERTS, tm), 0)
    allowed = jnp.zeros((N_EXPERTS, tm), jnp.bool_)
    for _ in range(TOPK_GROUPS):
        _, first = _first_max(gs, g_io, N_EXPERT_GROUPS)
        gs = jnp.where(g_io == first, -jnp.inf, gs)
        allowed = allowed | (x_io // gsz == first)
    work = jnp.where(allowed, sb, NEG)
    base = cnt_ref[:, :1]
    tri = tri_ref[...]
    idxs, ws, ranks = [], [], []
    for _ in range(TOP_K):
        _, first = _first_max(work, x_io, N_EXPERTS)
        hit = x_io == first
        idxs.append(first)
        ws.append(jnp.sum(jnp.where(hit, s, 0.0), 0, keepdims=True))
        work = jnp.where(hit, -jnp.inf, work)
        onehot = jnp.where(hit, 1.0, 0.0)
        before = _dot(onehot.astype(BF), tri)
        ranks.append(jnp.sum(jnp.where(hit, base + before, 0.0), 0, keepdims=True))
        base = base + jnp.sum(onehot, 1, keepdims=True)
    wsel = jnp.concatenate(ws, axis=0)
    idx_ref[...] = jnp.concatenate(idxs, axis=0)
    w_ref[...] = wsel / jnp.sum(wsel, 0, keepdims=True) * ROUTE_SCALE
    rank_ref[...] = jnp.concatenate(ranks, axis=0).astype(I32)
    cnt_ref[...] = jnp.broadcast_to(base, cnt_ref.shape)


def _router(h1, whi, wlo, rb, tm):
    n = h1.shape[0]
    tri = jnp.asarray(np.triu(np.ones((tm, tm), np.float32), 1), BF)
    slot = pl.BlockSpec((TOP_K, tm), lambda i: (0, i))
    const = lambda i: (0, 0)
    return pl.pallas_call(
        _router_kernel, grid=(n // tm,),
        in_specs=[pl.BlockSpec((tm, D_MODEL), lambda i: (i, 0)),
                  pl.BlockSpec((N_EXPERTS, D_MODEL), const), pl.BlockSpec((N_EXPERTS, D_MODEL), const),
                  pl.BlockSpec((N_EXPERTS, 1), const), pl.BlockSpec((tm, tm), const)],
        out_specs=[slot, slot, slot, pl.BlockSpec((N_EXPERTS, LANE), const)],
        out_shape=[jax.ShapeDtypeStruct((TOP_K, n), I32), jax.ShapeDtypeStruct((TOP_K, n), F32),
                   jax.ShapeDtypeStruct((TOP_K, n), I32), jax.ShapeDtypeStruct((N_EXPERTS, LANE), F32)],
        compiler_params=_cparams(("arbitrary",)),
    )(h1, whi, wlo, rb, tri)


def _pos_kernel(idx_ref, rank_ref, start_ref, pos_ref):
    tm = idx_ref.shape[1]
    x_io = lax.broadcasted_iota(I32, (N_EXPERTS, tm), 0)
    start = start_ref[...]
    rows = [jnp.sum(jnp.where(x_io == idx_ref[k:k + 1, :], start, 0.0), 0, keepdims=True) for k in range(TOP_K)]
    pos_ref[...] = jnp.concatenate(rows, axis=0).astype(I32) + rank_ref[...]


def _positions(eidx_t, rank_t, pad_start, tm):
    n = eidx_t.shape[1]
    slot = pl.BlockSpec((TOP_K, tm), lambda i: (0, i))
    return pl.pallas_call(
        _pos_kernel, grid=(n // tm,),
        in_specs=[slot, slot, pl.BlockSpec((N_EXPERTS, 1), lambda i: (0, 0))],
        out_specs=slot, out_shape=jax.ShapeDtypeStruct((TOP_K, n), I32),
        compiler_params=_cparams(("parallel",)),
    )(eidx_t, rank_t, pad_start)


ROW_UNROLL = 8
PACKED_W = D_MODEL // 2
U32 = jnp.uint32


def _fill_groups():
    p = EXPERT_BLOCK // 2
    while p >= 1:
        yield p
        p //= 2


def _pack_rows(x):
    bits = lambda v: lax.bitcast_convert_type(v.astype(BF).astype(F32), U32)
    return (bits(x[:, :PACKED_W]) >> 16) | (bits(x[:, PACKED_W:]) & U32(0xFFFF0000))


def _unpack_rows(u):
    return (lax.bitcast_convert_type(u << 16, F32), lax.bitcast_convert_type(u & U32(0xFFFF0000), F32))


def _permute_kernel(fill_start_ref, fill_n_ref, pos_ref, h_ref, xs_hbm, xp, zbuf, row_sem, fill_sem, *, fills_per_step):
    i = pl.program_id(0)
    tm = h_ref.shape[0]
    n_fills = fill_n_ref.shape[0]
    zbuf[...] = jnp.zeros(zbuf.shape, zbuf.dtype)
    xp[...] = _pack_rows(h_ref[...])

    def fill_copies(e):
        n = fill_n_ref[e]
        start = fill_start_ref[e]
        for p in _fill_groups():
            @pl.when((n & p) != 0)
            def _(p=p):
                off = start + (n & (p - 1))
                if p < SUBLANES:
                    for r in range(p):
                        pltpu.make_async_copy(zbuf.at[pl.ds(r, 1)], xs_hbm.at[pl.ds(off + r, 1)], fill_sem).start()
                else:
                    pltpu.make_async_copy(zbuf.at[pl.ds(0, p)], xs_hbm.at[pl.ds(pl.multiple_of(off, SUBLANES), p)],
                                          fill_sem).start()

    def fill_waits(e):
        n = fill_n_ref[e]
        for p in _fill_groups():
            @pl.when((n & p) != 0)
            def _(p=p):
                pltpu.make_async_copy(zbuf.at[pl.ds(0, p)], zbuf.at[pl.ds(0, p)], fill_sem).wait()

    for q in range(fills_per_step):
        e = i * fills_per_step + q

        @pl.when(e < n_fills)
        def _(e=e):
            fill_copies(e)

    for t in range(tm):
        for k in range(TOP_K):
            pltpu.make_async_copy(xp.at[pl.ds(t, 1)], xs_hbm.at[pl.ds(pos_ref[k, t], 1)], row_sem).start()

    for q in range(fills_per_step):
        e = i * fills_per_step + q

        @pl.when(e < n_fills)
        def _(e=e):
            fill_waits(e)

    pltpu.make_async_copy(xs_hbm.at[pl.ds(0, TOP_K * tm)], xs_hbm.at[pl.ds(0, TOP_K * tm)], row_sem).wait()


def _permute(fill_start, fill_n, pos, h1, n_rows, tm):
    n = h1.shape[0]
    nt = n // tm
    grid_spec = pltpu.PrefetchScalarGridSpec(
        num_scalar_prefetch=2, grid=(nt,),
        in_specs=[pl.BlockSpec((TOP_K, tm), lambda i, fs, fn: (0, i), memory_space=pltpu.SMEM),
                  pl.BlockSpec((tm, D_MODEL), lambda i, fs, fn: (i, 0))],
        out_specs=pl.BlockSpec(memory_space=pl.ANY),
        scratch_shapes=[pltpu.VMEM((tm, PACKED_W), U32), pltpu.VMEM((EXPERT_BLOCK // 2, PACKED_W), U32),
                        pltpu.SemaphoreType.DMA(()), pltpu.SemaphoreType.DMA(())])
    return pl.pallas_call(
        functools.partial(_permute_kernel, fills_per_step=-(-fill_n.shape[0] // nt)), grid_spec=grid_spec,
        out_shape=jax.ShapeDtypeStruct((n_rows, PACKED_W), U32),
        compiler_params=_cparams(("arbitrary",)),
    )(fill_start, fill_n, pos, h1)


def _experts_kernel(blk_e_ref, nused_ref, next_e_ref, x_ref, w1_hbm, w3_hbm, w2_hbm, y_ref,
                    w1f, w3f, w2f, w1b, w3b, w2b, ord_ref, sem):
    j = pl.program_id(0)

    def weight_copies(e, slot):
        return (pltpu.make_async_copy(w1_hbm.at[e], w1f.at[slot], sem.at[slot, 0]),
                pltpu.make_async_copy(w3_hbm.at[e], w3f.at[slot], sem.at[slot, 1]),
                pltpu.make_async_copy(w2_hbm.at[e], w2f.at[slot], sem.at[slot, 2]))

    @pl.when(j < nused_ref[0])
    def _():
        e = blk_e_ref[j]
        first_of_expert = (j == 0) | (e != blk_e_ref[jnp.maximum(j - 1, 0)])

        @pl.when(first_of_expert)
        def _():
            @pl.when(j == 0)
            def _():
                ord_ref[0] = 0
                for c in weight_copies(e, 0):
                    c.start()

            n = ord_ref[0]
            slot = n % 2
            for c in weight_copies(e, slot):
                c.wait()
            nxt = next_e_ref[e]

            @pl.when(nxt >= 0)
            def _():
                for c in weight_copies(nxt, 1 - slot):
                    c.start()

            w1b[...] = w1f[slot].astype(BF)
            w3b[...] = w3f[slot].astype(BF)
            w2b[...] = w2f[slot].astype(BF)
            ord_ref[0] = n + 1

        x = jnp.concatenate([half.astype(BF) for half in _unpack_rows(x_ref[...])], axis=1)
        a = _dot(x, w1b[...])
        y_ref[...] = _pack_rows(_dot((a * jax.nn.sigmoid(a) * _dot(x, w3b[...])).astype(BF), w2b[...]))

    @pl.when(j >= nused_ref[0])
    def _():
        y_ref[...] = jnp.zeros(y_ref.shape, y_ref.dtype)


def _experts(blk_e, nused, next_e, xs, w1, w3, w2):
    n_blocks = blk_e.shape[0]
    rb = EXPERT_BLOCK
    anyspec = pl.BlockSpec(memory_space=pl.ANY)
    grid_spec = pltpu.PrefetchScalarGridSpec(
        num_scalar_prefetch=3, grid=(n_blocks,),
        in_specs=[pl.BlockSpec((rb, PACKED_W), lambda j, be, nu, ne: (jnp.minimum(j, nu[0] - 1), 0)),
                  anyspec, anyspec, anyspec],
        out_specs=pl.BlockSpec((rb, PACKED_W), lambda j, be, nu, ne: (j, 0)),
        scratch_shapes=[pltpu.VMEM((2, D_MODEL, D_EXPERT), F32), pltpu.VMEM((2, D_MODEL, D_EXPERT), F32),
                        pltpu.VMEM((2, D_EXPERT, D_MODEL), F32),
                        pltpu.VMEM((D_MODEL, D_EXPERT), BF), pltpu.VMEM((D_MODEL, D_EXPERT), BF),
                        pltpu.VMEM((D_EXPERT, D_MODEL), BF),
                        pltpu.SMEM((1,), I32), pltpu.SemaphoreType.DMA((2, 3))])
    return pl.pallas_call(
        _experts_kernel, grid_spec=grid_spec,
        out_shape=jax.ShapeDtypeStruct((n_blocks * rb, PACKED_W), U32),
        compiler_params=_cparams(("arbitrary",)),
    )(blk_e, nused, next_e, xs, w1, w3, w2)


def _combine_kernel(pos_ref, pos_next_ref, ys_hbm, w_ref, h_ref, s1_ref, s3_ref, s2_ref, g_ref, b_ref, o_ref, ybuf, sem):
    i = pl.program_id(0)
    nt = pl.num_programs(0)
    tm = h_ref.shape[0]

    def start_gather(p_ref, slot):
        def body(t, c):
            for k in range(TOP_K):
                pltpu.make_async_copy(ys_hbm.at[pl.ds(p_ref[k, t], 1)], ybuf.at[slot, k, pl.ds(t, 1)], sem.at[slot]).start()
            return c

        lax.fori_loop(0, tm, body, 0, unroll=ROW_UNROLL)

    @pl.when(i == 0)
    def _():
        start_gather(pos_ref, 0)

    slot = i % 2
    pltpu.make_async_copy(ybuf.at[slot], ybuf.at[slot], sem.at[slot]).wait()

    for t in range(tm):
        for k in range(TOP_K):
            pltpu.make_async_copy(ys_hbm.at[pl.ds(pos_next_ref[k, t], 1)], ybuf.at[1 - slot, k, pl.ds(t, 1)],
                                  sem.at[1 - slot]).start()

    h = h_ref[...]
    hb = h.astype(BF)
    a = _dot(hb, s1_ref[...])
    y = ALPHA * h + _dot((a * jax.nn.sigmoid(a) * _dot(hb, s3_ref[...])).astype(BF), s2_ref[...])
    w = w_ref[...]
    routed = None
    for k in range(TOP_K):
        terms = [w[:, k:k + 1] * half for half in _unpack_rows(ybuf[slot, k])]
        routed = terms if routed is None else [r + t for r, t in zip(routed, terms)]
    y = y + jnp.concatenate(routed, axis=1)
    o_ref[...] = _layer_norm(y, g_ref[...], b_ref[...])

    @pl.when(i == nt - 1)
    def _():
        pltpu.make_async_copy(ybuf.at[1 - slot], ybuf.at[1 - slot], sem.at[1 - slot]).wait()


def _combine(pos, ys, w, h1, s1, s3, s2, g, b, tm):
    n = h1.shape[0]
    nt = n // tm
    row = lambda i: (i, 0)
    const = lambda i: (0, 0)
    return pl.pallas_call(
        _combine_kernel, grid=(nt,),
        in_specs=[pl.BlockSpec((TOP_K, tm), lambda i: (0, i), memory_space=pltpu.SMEM),
                  pl.BlockSpec((TOP_K, tm), lambda i: (0, jnp.minimum(i + 1, nt - 1)), memory_space=pltpu.SMEM),
                  pl.BlockSpec(memory_space=pl.ANY),
                  pl.BlockSpec((tm, TOP_K), row), pl.BlockSpec((tm, D_MODEL), row),
                  pl.BlockSpec(s1.shape, const), pl.BlockSpec(s3.shape, const), pl.BlockSpec(s2.shape, const),
                  pl.BlockSpec((1, D_MODEL), const), pl.BlockSpec((1, D_MODEL), const)],
        out_specs=pl.BlockSpec((tm, D_MODEL), row),
        out_shape=jax.ShapeDtypeStruct((n, D_MODEL), F32),
        scratch_shapes=[pltpu.VMEM((2, TOP_K, tm, PACKED_W), U32), pltpu.SemaphoreType.DMA((2,))],
        compiler_params=_cparams(("arbitrary",)),
    )(pos, pos, ys, w, h1, s1, s3, s2, g, b)


def _overlap_matrix(nc, n_slc):
    cs = np.arange(nc) * CMP_STRIDE
    ce = cs + CMP_LEN - 1
    js = np.arange(n_slc) * SLC_LEN
    je = js + SLC_LEN - 1
    ov = ((cs[:, None] <= je[None, :]) & (ce[:, None] >= js[None, :])).astype(np.float32)
    ov[nc - 1] = 0.0
    return ov


def _block_layout(counts, n):
    rb = EXPERT_BLOCK
    counts = counts.astype(I32)
    padded = (counts + rb - 1) // rb * rb
    pad_end = jnp.cumsum(padded)
    pad_start = pad_end - padded
    n_blocks = -(-TOP_K * n // rb) + N_EXPERTS
    blk_start = jnp.arange(n_blocks, dtype=I32) * rb
    blk_e = jnp.minimum(jnp.sum(pad_end[None, :] <= blk_start[:, None], axis=1), N_EXPERTS - 1).astype(I32)
    nused = (pad_end[-1] // rb).astype(I32).reshape(1)
    half = rb // 2
    tail_start = pad_end[-1] + half * jnp.arange(2 * N_EXPERTS, dtype=I32)
    tail_n = jnp.where(tail_start < n_blocks * rb, half, 0).astype(I32)
    fill_start = jnp.concatenate([pad_start + counts, jnp.minimum(tail_start, n_blocks * rb - half)])
    fill_n = jnp.concatenate([padded - counts, tail_n])
    ids = jnp.where(counts > 0, jnp.arange(N_EXPERTS, dtype=I32), N_EXPERTS)
    at_or_after = lax.cummin(ids[::-1])[::-1]
    nxt = jnp.concatenate([at_or_after[1:], jnp.full((1,), N_EXPERTS, I32)])
    next_e = jnp.where(nxt < N_EXPERTS, nxt, -1).astype(I32)
    return pad_start, fill_start, fill_n, blk_e, nused, next_e


def kernel(x, mem, ln0_g, ln0_b, rel_bias, w_in, cmp_pos_k, cmp_pos_v, cmp_k_w1, cmp_k_w2, cmp_v_w1, cmp_v_w2, mla_q_norm, mla_w_uq, mla_kv_norm, mla_w_ukv, mem_w_kv, w_branch, w_out, ln1_g, ln1_b, router_w, router_b, exp_w1, exp_w3, exp_w2, sh_w1, sh_w3, sh_w2, ln2_g, ln2_b):
    batch, seq, d = x.shape
    n = batch * seq
    l = 0
    row2 = lambda v: v.reshape(1, -1)
    tm = min(256, seq)

    pts = np.cumsum((0,) + IN_SPLITS)
    wcol = lambda k: w_in[l][:, pts[k]:pts[k + 1]]
    pad_cols = lambda w, lo, tot: jnp.pad(w, ((0, 0), (lo, tot - lo - w.shape[1])))
    ws = [wcol(0) * (NSA_DH ** -0.5 * LOG2E), wcol(1), pad_cols(wcol(2), 0, LANE), wcol(3), wcol(4),
          pad_cols(wcol(5), ROPE_LANE0, LANE), wcol(6), wcol(7)]
    ws = [w.astype(BF) for w in ws]
    scales = [1.0] * 6 + [MEM_DH ** -0.5, 1.0]
    gw = NSA_GROUPS * NSA_DH
    ws_t = [wcol(0) * (NSA_DH ** -0.5 * LOG2E), wcol(1)[:, 3 * gw:4 * gw], wcol(1)[:, 5 * gw:6 * gw],
            pad_cols(wcol(2), 0, LANE)]
    ws_t = [w.T.astype(BF) for w in ws_t]
    hd = MLA_NOPE + MLA_ROPE
    wuq = jnp.pad(mla_w_uq[l].reshape(MLA_Q_LORA, MLA_HEADS, hd), ((0, 0), (0, 0), (0, HEAD_PAD - hd)))
    wuq = wuq.reshape(MLA_Q_LORA, MLA_HEADS * HEAD_PAD).astype(BF)
    wukv = mla_w_ukv[l].reshape(MLA_KV_LORA, MLA_HEADS, MLA_NOPE + MLA_DV)
    wuk = jnp.pad(wukv[:, :, :MLA_NOPE], ((0, 0), (0, 0), (0, HEAD_PAD - MLA_NOPE)))
    wuk = wuk.reshape(MLA_KV_LORA, MLA_HEADS * HEAD_PAD).astype(BF)
    wuv = wukv[:, :, MLA_NOPE:].reshape(MLA_KV_LORA, MLA_HEADS * MLA_DV).astype(BF)

    h, qn, kvn, gn, cq, ckv, kr, qm, gm, qn_t, vs_t, vw_t, gn_t = _ln_inproj(
        x.reshape(n, d), row2(ln0_g), row2(ln0_b), ws, scales, ws_t, tm)

    q_mla, k_mla, v_mla = _mla_prep(cq, ckv, kr, row2(mla_q_norm[l]), row2(mla_kv_norm[l]), wuq, wuk, wuv, seq, tm)
    o_mla = _mla_attn(q_mla, k_mla, v_mla, batch, seq, min(MLA_TQ, seq))

    nc = seq // CMP_STRIDE
    n_slc = seq // SLC_LEN
    cc = kvn[:, :2 * gw].reshape(batch, nc, CMP_STRIDE, 2 * NSA_GROUPS, NSA_DH)
    cc = cc.transpose(0, 3, 1, 2, 4).reshape(batch, 2 * NSA_GROUPS, nc, CMP_STRIDE * NSA_DH)
    pos = jnp.stack([cmp_pos_k[l], cmp_pos_v[l]]).reshape(2, 1, CMP_LEN * NSA_DH)
    w1c = jnp.stack([cmp_k_w1[l], cmp_v_w1[l]]).astype(BF)
    w2c = jnp.stack([cmp_k_w2[l], cmp_v_w2[l]]).astype(BF)
    kvc = _nsa_compress(cc, pos, w1c, w2c)
    overlap = jnp.asarray(_overlap_matrix(nc, n_slc), BF)
    o_cmp, sel_t = _nsa_cmp(rel_bias, qn, kvc, overlap, batch, seq)
    o_nsa = _nsa_attn(rel_bias, qn_t, kvn, vs_t, vw_t, sel_t, gn, gn_t, o_cmp, batch, seq)

    o_mem = _mem_attn(qm, mem.reshape(-1, d), mem_w_kv[l].astype(BF), batch, seq, min(512, seq))

    h1 = _merge(o_nsa, o_mla, o_mem, gm, h, w_branch[l].astype(BF), w_out[l].astype(BF),
                row2(ln1_g[l]), row2(ln1_b[l]), tm)

    rw_t = router_w[l].T
    rw_hi = rw_t.astype(BF)
    rw_lo = (rw_t - rw_hi.astype(F32)).astype(BF)
    eidx_t, w_t, rank_t, counts = _router(h1, rw_hi, rw_lo, router_b[l].reshape(N_EXPERTS, 1), tm)
    pad_start, fill_start, fill_n, blk_e, nused, next_e = _block_layout(counts[:, 0], n)
    pos = _positions(eidx_t, rank_t, pad_start.astype(F32).reshape(N_EXPERTS, 1), tm)
    xs = _permute(fill_start, fill_n, pos, h1, blk_e.shape[0] * EXPERT_BLOCK, tm)
    ys = _experts(blk_e, nused, next_e, xs, exp_w1[l], exp_w3[l], exp_w2[l])
    out = _combine(pos, ys, w_t.T, h1, sh_w1[l].astype(BF), sh_w3[l].astype(BF), sh_w2[l].astype(BF),
                   row2(ln2_g[l]), row2(ln2_b[l]), min(128, seq))
    return out.reshape(batch, seq, d)
```
